```python
import jax, jax.numpy as jnp
from jax import lax
import numpy as np

D_MODEL = 1024
BATCH = 8
SEQ = 2048
DEPTH = 1
DEC_BATCH = 128
DEC_SEQ = 1
PAST_LEN = 16384
PAGE_SIZE = 128

PLE_DIM = 256
MIX_WIDTH = D_MODEL
GLA_HEADS = 4
GLA_WIDTH = MIX_WIDTH // 2
GLA_DV = GLA_WIDTH // GLA_HEADS
GLA_DK = GLA_DV // 2
GLA_RANK = 16
GLA_GATE_NORM = 16.0
GLA_CHUNK = 64
CONV_WIDTH = MIX_WIDTH - GLA_WIDTH
CONV_GROUPS = 8
CONV_K = 3
D_FF = ((8 * D_MODEL // 3 + 127) // 128) * 128
EPS = 1e-6
N_NORMS = 8
SPLIT_SIZES = (GLA_HEADS * GLA_DK, GLA_HEADS * GLA_DK, GLA_WIDTH, GLA_WIDTH, GLA_RANK, CONV_WIDTH, CONV_WIDTH, CONV_WIDTH)
IN_COLS = sum(SPLIT_SIZES)

kernel_name = "hymba_gla_shortconv_macaron_decoder_step"


def rms_norm(x, g):
    x32 = x.astype(jnp.float32)
    y = x32 * lax.rsqrt(jnp.mean(x32 * x32, axis=-1, keepdims=True) + EPS)
    return (y * g.astype(jnp.float32)).astype(x.dtype)


def swiglu(x, w_gate, w_up, w_down):
    return (jax.nn.silu(x @ w_gate) * (x @ w_up)) @ w_down


def gla_chunked(q, k, v, log_a, s0):
    out_dtype = v.dtype
    q, k, v, log_a, s0 = (a.astype(jnp.float32) for a in (q, k, v, log_a, s0))
    bsz, t_len, h, dk = q.shape
    dv = v.shape[-1]
    c = min(GLA_CHUNK, t_len)
    n = -(-t_len // c)
    pad = n * c - t_len

    def blocks(a):
        a = jnp.pad(a, ((0, 0), (0, pad), (0, 0), (0, 0)))
        return a.reshape(bsz, n, c, h, a.shape[-1])

    qb, kb, vb, lb = blocks(q), blocks(k), blocks(v), blocks(log_a)
    bcum = jnp.cumsum(lb, axis=2)
    to_scan = lambda a: jnp.moveaxis(a, 1, 0)
    mask = jnp.tril(jnp.ones((c, c), dtype=bool))[None, :, :, None, None]

    def step(s, inp):
        qc, kc, vc, bc = inp
        o_inter = jnp.einsum('bthk,bhkv->bthv', qc * jnp.exp(bc), s)
        diff = jnp.where(mask, bc[:, :, None] - bc[:, None, :], -jnp.inf)
        att = jnp.sum(qc[:, :, None] * kc[:, None, :] * jnp.exp(diff), axis=-1)
        o_intra = jnp.einsum('btsh,bshv->bthv', att, vc)
        b_last = bc[:, -1]
        k_dec = kc * jnp.exp(b_last[:, None] - bc)
        s_new = jnp.exp(b_last)[..., None] * s + jnp.einsum('bshk,bshv->bhkv', k_dec, vc)
        return s_new, o_inter + o_intra

    s_fin, o = lax.scan(step, s0, (to_scan(qb), to_scan(kb), to_scan(vb), to_scan(bcum)))
    o = jnp.moveaxis(o, 0, 1).reshape(bsz, n * c, h, dv)[:, :t_len]
    return o.astype(out_dtype), s_fin.astype(out_dtype)


def token_mixer(u, s0, conv0, w_in, w_gate_up, b_gate, gla_norm_g, conv_w, w_out):
    bsz, t_len, _ = u.shape
    z = u @ w_in
    points, acc = [], 0
    for sz in SPLIT_SIZES[:-1]:
        acc += sz
        points.append(acc)
    q, k, v, g, r, cb, cc, ch = jnp.split(z, points, axis=-1)
    q = q.reshape(bsz, t_len, GLA_HEADS, GLA_DK) * (GLA_DK ** -0.5)
    k = k.reshape(bsz, t_len, GLA_HEADS, GLA_DK)
    v = v.reshape(bsz, t_len, GLA_HEADS, GLA_DV)
    gate_logit = (r @ w_gate_up + b_gate).astype(jnp.float32)
    log_a = (jax.nn.log_sigmoid(gate_logit) / GLA_GATE_NORM).reshape(bsz, t_len, GLA_HEADS, GLA_DK)
    o, s_new = gla_chunked(q, k, v, log_a, s0)
    o = rms_norm(o, gla_norm_g) * jax.nn.silu(g.reshape(bsz, t_len, GLA_HEADS, GLA_DV))
    o = o.reshape(bsz, t_len, GLA_WIDTH)
    uc = cc * ch
    full = jnp.concatenate([conv0.astype(uc.dtype), uc], axis=1)
    y = sum(conv_w[j] * full[:, j:j + t_len] for j in range(CONV_K))
    oc = cb * y
    conv_new = full[:, -(CONV_K - 1):]
    out = jnp.concatenate([o, oc], axis=-1) @ w_out
    return out, s_new, conv_new


def decoder_layer(x, p, s0, conv0, ng, w_in, w_gate_up, b_gate, gla_norm_g, conv_w, w_out,
                  f1_gate, f1_up, f1_down, f2_gate, f2_up, f2_down, w_ple_proj, w_ple_gate):
    h = x + 0.5 * rms_norm(swiglu(rms_norm(x, ng[0]), f1_gate, f1_up, f1_down), ng[1])
    mix, s_new, conv_new = token_mixer(rms_norm(h, ng[2]), s0, conv0, w_in, w_gate_up, b_gate,
                                       gla_norm_g, conv_w, w_out)
    h = h + rms_norm(mix, ng[3])
    h = h + 0.5 * rms_norm(swiglu(rms_norm(h, ng[4]), f2_gate, f2_up, f2_down), ng[5])
    gate = jax.nn.sigmoid(rms_norm(h, ng[6]) @ w_ple_gate)
    h = h + rms_norm((p @ w_ple_proj) * gate, ng[7])
    return h, s_new, conv_new


def setup_inputs(seed: int = 0) -> dict:
    key = jax.random.key(seed)
    ks = jax.random.split(key, 24)
    nrm = lambda k, shape, scale: jax.random.normal(k, shape, jnp.float32) * scale
    D = D_MODEL
    return {
        "x_prompt": nrm(ks[0], (BATCH, SEQ, D), 1.0),
        "x_sample": nrm(ks[1], (DEC_BATCH, DEC_SEQ, D), 1.0),
        "state_gla": nrm(ks[2], (DEPTH, DEC_BATCH, GLA_HEADS, GLA_DK, GLA_DV), 1.0),
        "state_conv": nrm(ks[3], (DEPTH, DEC_BATCH, CONV_K - 1, CONV_WIDTH), 1.0),
        "p_prompt": nrm(ks[4], (DEPTH, BATCH, SEQ, PLE_DIM), 1.0),
        "p_sample": nrm(ks[5], (DEPTH, DEC_BATCH, DEC_SEQ, PLE_DIM), 1.0),
        "norm_g": 1.0 + nrm(ks[6], (DEPTH, N_NORMS, D), 0.05),
        "w_in": nrm(ks[7], (DEPTH, D, IN_COLS), D ** -0.5),
        "w_gate_up": nrm(ks[8], (DEPTH, GLA_RANK, GLA_HEADS * GLA_DK), GLA_RANK ** -0.5),
        "b_gate": nrm(ks[9], (DEPTH, GLA_HEADS * GLA_DK), 0.1),
        "gla_norm_g": 1.0 + nrm(ks[10], (DEPTH, GLA_DV), 0.05),
        "conv_w": nrm(ks[11], (DEPTH, CONV_K, CONV_WIDTH), CONV_K ** -0.5),
        "w_out": nrm(ks[12], (DEPTH, MIX_WIDTH, D), MIX_WIDTH ** -0.5),
        "ffn1_gate": nrm(ks[13], (DEPTH, D, D_FF), D ** -0.5),
        "ffn1_up": nrm(ks[14], (DEPTH, D, D_FF), D ** -0.5),
        "ffn1_down": nrm(ks[15], (DEPTH, D_FF, D), D_FF ** -0.5),
        "ffn2_gate": nrm(ks[16], (DEPTH, D, D_FF), D ** -0.5),
        "ffn2_up": nrm(ks[17], (DEPTH, D, D_FF), D ** -0.5),
        "ffn2_down": nrm(ks[18], (DEPTH, D_FF, D), D_FF ** -0.5),
        "w_ple_proj": nrm(ks[19], (DEPTH, PLE_DIM, D), PLE_DIM ** -0.5),
        "w_ple_gate": nrm(ks[20], (DEPTH, D, D), D ** -0.5),
    }


def reference(x_prompt, x_sample, state_gla, state_conv, p_prompt, p_sample, norm_g, w_in,
              w_gate_up, b_gate, gla_norm_g, conv_w, w_out, ffn1_gate, ffn1_up, ffn1_down,
              ffn2_gate, ffn2_up, ffn2_down, w_ple_proj, w_ple_gate):
    hp, hs = x_prompt, x_sample
    gla_p, conv_p, gla_s, conv_s = [], [], [], []
    for i in range(DEPTH):
        weights = (norm_g[i], w_in[i], w_gate_up[i], b_gate[i], gla_norm_g[i], conv_w[i], w_out[i],
                   ffn1_gate[i], ffn1_up[i], ffn1_down[i], ffn2_gate[i], ffn2_up[i], ffn2_down[i],
                   w_ple_proj[i], w_ple_gate[i])
        s0_p = jnp.zeros((hp.shape[0], GLA_HEADS, GLA_DK, GLA_DV), hp.dtype)
        c0_p = jnp.zeros((hp.shape[0], CONV_K - 1, CONV_WIDTH), hp.dtype)
        hp, sp, cp = decoder_layer(hp, p_prompt[i], s0_p, c0_p, *weights)
        hs, ss, cs = decoder_layer(hs, p_sample[i], state_gla[i], state_conv[i], *weights)
        gla_p.append(sp); conv_p.append(cp); gla_s.append(ss); conv_s.append(cs)
    return (hp, hs, jnp.stack(gla_p), jnp.stack(conv_p), jnp.stack(gla_s), jnp.stack(conv_s))
```

```python
import functools

import numpy as np
import jax
import jax.numpy as jnp
from jax import lax
from jax.experimental import pallas as pl
from jax.experimental.pallas import tpu as pltpu

F32 = jnp.float32
BF16 = jnp.bfloat16

D_MODEL = 1024
PLE_DIM = 256
GLA_HEADS = 4
GLA_DK = 64
GLA_DV = 128
HK = GLA_HEADS * GLA_DK
HV = GLA_HEADS * GLA_DV
GLA_RANK = 16
GLA_GATE_NORM = 16.0
CONV_WIDTH = 512
CONV_K = 3
D_FF = 2816
EPS = 1e-6

LANES = 128
MXU_COLS = 256
CHUNK = 64
N_LEVELS = 7
VMEM_CAP = 56 * 1024 * 1024

NT_DIMS = (((1,), (1,)), ((), ()))
TN_DIMS = (((0,), (0,)), ((), ()))


def _rms(x, g):
    return x * lax.rsqrt(jnp.mean(x * x, axis=-1, keepdims=True) + EPS) * g


def _sigmoid(x):
    return 1.0 / (1.0 + jnp.exp(-x))


def _log_sigmoid(x):
    return jnp.minimum(x, 0.0) - jnp.log1p(jnp.exp(-jnp.abs(x)))


def _split3(x):
    hi = x.astype(BF16)
    r1 = x - hi.astype(F32)
    mid = r1.astype(BF16)
    lo = (r1 - mid.astype(F32)).astype(BF16)
    return hi, mid, lo


def _resident(shape):
    nd = len(shape)
    return pl.BlockSpec(shape, lambda *_: (0,) * nd, pipeline_mode=pl.Buffered(1))


def _nbytes(shape, dtype):
    return int(np.prod(shape)) * jnp.dtype(dtype).itemsize


def _vmem_limit(est_bytes):
    return int(min(VMEM_CAP, max(est_bytes * 5 // 4, 16 * 1024 * 1024)))


def _ffn_body(pre, post, ple, *refs):
    if ple:
        x_ref, ng_ref, wg_ref, wu_ref, wd_ref, p_ref, wpp_ref, wpg_ref, o_ref, h_scr = refs
    else:
        x_ref, ng_ref, wg_ref, wu_ref, wd_ref, o_ref, h_scr = refs
    x = x_ref[...]
    xn = _rms(x, ng_ref[pre:pre + 1, :]).astype(BF16)
    for c in range(D_FF // MXU_COLS):
        sl = slice(c * MXU_COLS, (c + 1) * MXU_COLS)
        g = jnp.dot(xn, wg_ref[:, sl], preferred_element_type=F32)
        u = jnp.dot(xn, wu_ref[:, sl], preferred_element_type=F32)
        h_scr[:, sl] = (g * _sigmoid(g) * u).astype(BF16)
    y = jnp.dot(h_scr[...], wd_ref[...], preferred_element_type=F32)
    h = x + 0.5 * _rms(y, ng_ref[post:post + 1, :])
    if ple:
        hn = _rms(h, ng_ref[6:7, :]).astype(BF16)
        gate = _sigmoid(jnp.dot(hn, wpg_ref[...], preferred_element_type=F32))
        proj = jnp.dot(p_ref[...].astype(BF16), wpp_ref[...], preferred_element_type=F32)
        h = h + _rms(proj * gate, ng_ref[7:8, :])
    o_ref[...] = h


def _ffn(x, ng, wg, wu, wd, pre, post, tm, ple_args=None):
    rows = x.shape[0]
    assert rows % tm == 0
    ple = ple_args is not None
    row_spec = pl.BlockSpec((tm, D_MODEL), lambda i: (i, 0))
    in_specs = [row_spec, _resident(ng.shape), _resident(wg.shape), _resident(wu.shape), _resident(wd.shape)]
    args = [x, ng, wg, wu, wd]
    est = (4 * _nbytes((tm, D_MODEL), F32) + 3 * _nbytes(wg.shape, BF16) + _nbytes((tm, D_FF), BF16)
           + 6 * _nbytes((tm, MXU_COLS), F32) + 3 * _nbytes((tm, D_MODEL), F32))
    if ple:
        p, wpp, wpg = ple_args
        in_specs += [pl.BlockSpec((tm, PLE_DIM), lambda i: (i, 0)), _resident(wpp.shape), _resident(wpg.shape)]
        args += [p, wpp, wpg]
        est += 2 * _nbytes((tm, PLE_DIM), F32) + _nbytes(wpp.shape, BF16) + _nbytes(wpg.shape, BF16)
        est += 3 * _nbytes((tm, D_MODEL), F32)
    return pl.pallas_call(
        functools.partial(_ffn_body, pre, post, ple),
        grid=(rows // tm,),
        in_specs=in_specs,
        out_specs=row_spec,
        out_shape=jax.ShapeDtypeStruct((rows, D_MODEL), F32),
        scratch_shapes=[pltpu.VMEM((tm, D_FF), BF16)],
        compiler_params=pltpu.CompilerParams(
            dimension_semantics=("arbitrary",), vmem_limit_bytes=_vmem_limit(est)),
        name="ffn_ple" if ple else "ffn",
    )(*args)


def _mixer_inputs(h, ng_ref, wa_ref, wr_ref, wgu_ref, bg_ref, wc_ref):
    u = _rms(h, ng_ref[2:3, :]).astype(BF16)
    za = jnp.dot(u, wa_ref[...], preferred_element_type=F32)
    r = jnp.dot(u, wr_ref[...], preferred_element_type=F32)
    logit = jnp.dot(r.astype(BF16), wgu_ref[...], preferred_element_type=F32) + bg_ref[...]
    la = _log_sigmoid(logit) * (1.0 / GLA_GATE_NORM)
    zc = jnp.dot(u, wc_ref[...], preferred_element_type=F32)
    q = za[:, 0:HK] * (GLA_DK ** -0.5)
    k = za[:, HK:2 * HK]
    v = za[:, 2 * HK:2 * HK + HV]
    g = za[:, 2 * HK + HV:2 * HK + 2 * HV]
    cb = zc[:, 0:CONV_WIDTH]
    cc = zc[:, CONV_WIDTH:2 * CONV_WIDTH]
    ch = zc[:, 2 * CONV_WIDTH:3 * CONV_WIDTH]
    return q, k, v, g, la, cb, cc, ch


def _mixer_output(h, o, g, oc, ng_ref, gng_ref, wout_ref):
    heads = []
    for hd in range(GLA_HEADS):
        sl = slice(hd * GLA_DV, (hd + 1) * GLA_DV)
        heads.append(_rms(o[:, sl], gng_ref[...]))
    on = jnp.concatenate(heads, axis=1) * (g * _sigmoid(g))
    cat = jnp.concatenate([on.astype(BF16), oc.astype(BF16)], axis=1)
    mix = jnp.dot(cat, wout_ref[...], preferred_element_type=F32)
    return h + _rms(mix, ng_ref[3:4, :])


def _chunk_constants():
    t = np.arange(CHUNK)[:, None]
    s = np.arange(CHUNK)[None, :]
    blocks = []
    for j in range(N_LEVELS - 1):
        half = 1 << j
        mid = (t // (2 * half)) * (2 * half) + half
        blocks.append((s <= t).astype(np.float32) - (s < mid).astype(np.float32))
    blocks.append((s <= t).astype(np.float32))
    blocks.append((s > t).astype(np.float32))
    m = np.concatenate(blocks, axis=0)
    mst = np.concatenate([m, m, m], axis=1)
    x = np.bitwise_xor(t, s)
    top = np.floor(np.log2(np.maximum(x, 1))).astype(np.int32) + 1
    lvl = np.where(s > t, -1, np.where(s == t, 0, top)).astype(np.int32)
    return jnp.asarray(mst, BF16), jnp.asarray(np.tile(lvl, (1, GLA_HEADS)), jnp.int32)


def _mixer_prompt_body(tm, h_ref, ng_ref, wa_ref, wr_ref, wgu_ref, bg_ref, wc_ref, gng_ref, cw_ref,
                       wout_ref, mst_ref, lvl_ref, o_ref, sg_ref, sc_ref,
                       s_scr, carry_scr, q_scr, k_scr, la_scr, v_scr, og_scr):
    j = pl.program_id(1)

    @pl.when(j == 0)
    def _():
        s_scr[...] = jnp.zeros_like(s_scr)
        carry_scr[...] = jnp.zeros_like(carry_scr)

    h = h_ref[...]
    q, k, v, g, la, cb, cc, ch = _mixer_inputs(h, ng_ref, wa_ref, wr_ref, wgu_ref, bg_ref, wc_ref)
    q_scr[...] = q
    k_scr[...] = k
    la_scr[...] = la
    v_scr[...] = v

    lvl = lvl_ref[...]
    lane_head_k = lax.broadcasted_iota(jnp.int32, (CHUNK, HK), 1) // GLA_DK
    lane_head_v = lax.broadcasted_iota(jnp.int32, (CHUNK, HV), 1) // GLA_DV
    row_head = lax.broadcasted_iota(jnp.int32, (HK, HV), 0) // GLA_DK
    col_head = lax.broadcasted_iota(jnp.int32, (HK, HV), 1) // GLA_DV
    state_mask = row_head == col_head
    row_head_s = lax.broadcasted_iota(jnp.int32, (HK, GLA_DV), 0) // GLA_DK
    ones_tn = jnp.ones((3 * CHUNK, GLA_DV), BF16)

    def block_diag_rows(x, lane_head):
        zero = jnp.zeros_like(x)
        return jnp.concatenate([jnp.where(lane_head == hd, x, zero) for hd in range(GLA_HEADS)], axis=0)

    def chunk_step(n, carry):
        r0 = pl.multiple_of(n * CHUNK, CHUNK)
        qc = q_scr[pl.ds(r0, CHUNK), :]
        kc = k_scr[pl.ds(r0, CHUNK), :]
        vc = v_scr[pl.ds(r0, CHUNK), :].astype(BF16)
        la3 = jnp.concatenate(_split3(la_scr[pl.ds(r0, CHUNK), :]), axis=0)
        e = jnp.dot(mst_ref[...], la3, preferred_element_type=F32)
        s_prev = s_scr[...]

        x0 = lax.dot_general(qc.astype(BF16), block_diag_rows(kc.astype(BF16), lane_head_k), NT_DIMS,
                             preferred_element_type=F32)
        att = jnp.where(lvl == 0, x0, 0.0)
        for lv in range(1, N_LEVELS):
            el = e[(lv - 1) * CHUNK:lv * CHUNK, :]
            ql = (qc * jnp.exp(jnp.minimum(el, 0.0))).astype(BF16)
            kl = (kc * jnp.exp(jnp.minimum(-el, 0.0))).astype(BF16)
            xl = lax.dot_general(ql, block_diag_rows(kl, lane_head_k), NT_DIMS, preferred_element_type=F32)
            att = jnp.where(lvl == lv, xl, att)

        bc = e[6 * CHUNK:7 * CHUNK, :]
        q_in = (qc * jnp.exp(bc)).astype(BF16)
        s_bd = jnp.where(state_mask, jnp.concatenate([s_prev] * GLA_HEADS, axis=1), 0.0).astype(BF16)
        o = (jnp.dot(q_in, s_bd, preferred_element_type=F32)
             + jnp.dot(att.astype(BF16), block_diag_rows(vc, lane_head_v), preferred_element_type=F32))
        og_scr[pl.ds(r0, CHUNK), :] = o

        k_dec = (kc * jnp.exp(e[7 * CHUNK:8 * CHUNK, :])).astype(BF16)
        upd_all = lax.dot_general(k_dec, vc, TN_DIMS, preferred_element_type=F32)
        upd = jnp.zeros((HK, GLA_DV), F32)
        for hd in range(GLA_HEADS):
            upd = jnp.where(row_head_s == hd, upd_all[:, hd * GLA_DV:(hd + 1) * GLA_DV], upd)
        b_last_t = lax.dot_general(la3, ones_tn, TN_DIMS, preferred_element_type=F32)
        s_scr[...] = jnp.exp(b_last_t) * s_prev + upd
        return carry

    lax.fori_loop(0, tm // CHUNK, chunk_step, 0)

    uc = cc * ch
    row = lax.broadcasted_iota(jnp.int32, uc.shape, 0)
    c1 = carry_scr[7:8, :]
    c2 = carry_scr[6:7, :]
    prev1 = jnp.where(row == 0, c1, pltpu.roll(uc, 1, axis=0))
    prev2 = jnp.where(row == 0, c2, jnp.where(row == 1, c1, pltpu.roll(uc, 2, axis=0)))
    y = cw_ref[0:1, :] * prev2 + cw_ref[1:2, :] * prev1 + cw_ref[2:3, :] * uc
    carry_scr[...] = uc[tm - 8:tm, :]

    o_ref[...] = _mixer_output(h, og_scr[...], g, cb * y, ng_ref, gng_ref, wout_ref)

    @pl.when(j == pl.num_programs(1) - 1)
    def _():
        sg_ref[0] = s_scr[...]
        sc_ref[0] = carry_scr[6:8, :]


def _mixer_prompt(h, batch, seq, tm, ng, wa, wr, wgu, bg, wc, gng, cw, wout):
    assert seq % tm == 0 and tm % CHUNK == 0
    nt = seq // tm
    mst, lvl = _chunk_constants()
    consts = [ng, wa, wr, wgu, bg, wc, gng, cw, wout, mst, lvl]
    row_spec = pl.BlockSpec((tm, D_MODEL), lambda b, j: (b * nt + j, 0))
    est = (4 * _nbytes((tm, D_MODEL), F32) + sum(_nbytes(c.shape, c.dtype) for c in consts)
           + 2 * _nbytes((tm, 3 * HV), F32) + _nbytes((tm, 3 * HK + 2 * HV), F32)
           + 6 * _nbytes((tm, D_MODEL), F32))
    return pl.pallas_call(
        functools.partial(_mixer_prompt_body, tm),
        grid=(batch, nt),
        in_specs=[row_spec] + [_resident(c.shape) for c in consts],
        out_specs=[row_spec,
                   pl.BlockSpec((1, HK, GLA_DV), lambda b, j: (b, 0, 0)),
                   pl.BlockSpec((1, CONV_K - 1, CONV_WIDTH), lambda b, j: (b, 0, 0))],
        out_shape=[jax.ShapeDtypeStruct((batch * seq, D_MODEL), F32),
                   jax.ShapeDtypeStruct((batch, HK, GLA_DV), F32),
                   jax.ShapeDtypeStruct((batch, CONV_K - 1, CONV_WIDTH), F32)],
        scratch_shapes=[pltpu.VMEM((HK, GLA_DV), F32), pltpu.VMEM((8, CONV_WIDTH), F32),
                        pltpu.VMEM((tm, HK), F32), pltpu.VMEM((tm, HK), F32), pltpu.VMEM((tm, HK), F32),
                        pltpu.VMEM((tm, HV), F32), pltpu.VMEM((tm, HV), F32)],
        compiler_params=pltpu.CompilerParams(
            dimension_semantics=("arbitrary", "arbitrary"), vmem_limit_bytes=_vmem_limit(est)),
        name="mixer_prompt",
    )(h, *consts)


def _mixer_sample_body(nb, bs, h_ref, s_ref, c0_ref, ng_ref, wa_ref, wr_ref, wgu_ref, bg_ref, wc_ref,
                       gng_ref, cw_ref, wout_ref, o_ref, so_ref, co_ref,
                       q_scr, k_scr, la3_scr, v_scr, g_scr, oc_scr, og_scr):
    i = pl.program_id(0)

    @pl.when(i == 0)
    def _():
        q, k, v, g, la, cb, cc, ch = _mixer_inputs(h_ref[...], ng_ref, wa_ref, wr_ref, wgu_ref, bg_ref, wc_ref)
        q_scr[...] = q.astype(BF16)
        k_scr[...] = k.astype(BF16)
        la3_scr[...] = jnp.concatenate(_split3(la), axis=0)
        v_scr[...] = v
        g_scr[...] = g
        uc = cc * ch
        c0 = c0_ref[...]
        old2 = c0[:, 0:CONV_WIDTH]
        old1 = c0[:, CONV_WIDTH:2 * CONV_WIDTH]
        y = cw_ref[0:1, :] * old2 + cw_ref[1:2, :] * old1 + cw_ref[2:3, :] * uc
        oc_scr[...] = cb * y
        co_ref[...] = jnp.concatenate([old1, uc], axis=1)

    seq_row = lax.broadcasted_iota(jnp.int32, (nb, LANES), 0)

    def seq_step(t, carry):
        b = i * bs + t
        pick = jnp.where(seq_row == b, 1.0, 0.0).astype(BF16)
        pick3 = jnp.concatenate([pick, pick, pick], axis=0)
        la_t = lax.dot_general(la3_scr[...], pick3, TN_DIMS, preferred_element_type=F32)
        k_t = lax.dot_general(k_scr[...], pick, TN_DIMS, preferred_element_type=F32)
        q_t = lax.dot_general(q_scr[...], pick, TN_DIMS, preferred_element_type=F32)
        v_row = v_scr[pl.ds(b, 1), :]
        v_b = jnp.concatenate(
            [jnp.broadcast_to(v_row[:, hd * GLA_DV:(hd + 1) * GLA_DV], (GLA_DK, GLA_DV))
             for hd in range(GLA_HEADS)], axis=0)
        s_new = jnp.exp(la_t) * s_ref[t] + k_t * v_b
        so_ref[t] = s_new
        qs = q_t * s_new
        o_row = jnp.concatenate(
            [jnp.sum(qs[hd * GLA_DK:(hd + 1) * GLA_DK, :], axis=0, keepdims=True)
             for hd in range(GLA_HEADS)], axis=1)
        og_scr[pl.ds(b, 1), :] = o_row
        return carry

    lax.fori_loop(0, bs, seq_step, 0)

    @pl.when(i == pl.num_programs(0) - 1)
    def _():
        o_ref[...] = _mixer_output(h_ref[...], og_scr[...], g_scr[...], oc_scr[...], ng_ref, gng_ref, wout_ref)


def _mixer_sample(h, s0, c0, bs, ng, wa, wr, wgu, bg, wc, gng, cw, wout):
    nb = h.shape[0]
    assert nb == LANES and nb % bs == 0
    consts = [ng, wa, wr, wgu, bg, wc, gng, cw, wout]
    state_spec = pl.BlockSpec((bs, HK, GLA_DV), lambda i: (i, 0, 0))
    est = (sum(_nbytes(c.shape, c.dtype) for c in consts) + 4 * _nbytes((bs, HK, GLA_DV), F32)
           + 12 * _nbytes((nb, D_MODEL), F32) + 2 * _nbytes((nb, 3 * HV), F32))
    return pl.pallas_call(
        functools.partial(_mixer_sample_body, nb, bs),
        grid=(nb // bs,),
        in_specs=[_resident(h.shape), state_spec, _resident(c0.shape)] + [_resident(c.shape) for c in consts],
        out_specs=[pl.BlockSpec((nb, D_MODEL), lambda i: (0, 0)), state_spec,
                   pl.BlockSpec(c0.shape, lambda i: (0, 0))],
        out_shape=[jax.ShapeDtypeStruct((nb, D_MODEL), F32),
                   jax.ShapeDtypeStruct(s0.shape, F32),
                   jax.ShapeDtypeStruct(c0.shape, F32)],
        scratch_shapes=[pltpu.VMEM((nb, HK), BF16), pltpu.VMEM((nb, HK), BF16), pltpu.VMEM((3 * nb, HK), BF16),
                        pltpu.VMEM((nb, HV), F32), pltpu.VMEM((nb, HV), F32), pltpu.VMEM((nb, CONV_WIDTH), F32),
                        pltpu.VMEM((nb, HV), F32)],
        compiler_params=pltpu.CompilerParams(
            dimension_semantics=("arbitrary",), vmem_limit_bytes=_vmem_limit(est)),
        name="mixer_sample",
    )(h, s0, c0, *consts)


TM_PROMPT = 512
BS_SAMPLE = 16


def kernel(x_prompt, x_sample, state_gla, state_conv, p_prompt, p_sample, norm_g, w_in, w_gate_up, b_gate,
           gla_norm_g, conv_w, w_out, ffn1_gate, ffn1_up, ffn1_down, ffn2_gate, ffn2_up, ffn2_down,
           w_ple_proj, w_ple_gate):
    batch, seq, _ = x_prompt.shape
    nb = x_sample.shape[0]
    assert norm_g.shape[0] == 1 and x_sample.shape[1] == 1

    ng = norm_g[0]
    wi = w_in[0]
    r0 = 2 * HK + 2 * HV
    wa = wi[:, :r0].astype(BF16)
    wr = jnp.pad(wi[:, r0:r0 + GLA_RANK], ((0, 0), (0, LANES - GLA_RANK))).astype(BF16)
    wc = wi[:, r0 + GLA_RANK:].astype(BF16)
    wgu = jnp.pad(w_gate_up[0], ((0, LANES - GLA_RANK), (0, 0))).astype(BF16)
    bg = b_gate[0].reshape(1, HK)
    gng = gla_norm_g[0].reshape(1, GLA_DV)
    cw = conv_w[0]
    wout = w_out[0].astype(BF16)
    f1 = (ffn1_gate[0].astype(BF16), ffn1_up[0].astype(BF16), ffn1_down[0].astype(BF16))
    f2 = (ffn2_gate[0].astype(BF16), ffn2_up[0].astype(BF16), ffn2_down[0].astype(BF16))
    wpp = w_ple_proj[0].astype(BF16)
    wpg = w_ple_gate[0].astype(BF16)
    mix_w = (ng, wa, wr, wgu, bg, wc, gng, cw, wout)

    xp = x_prompt.reshape(batch * seq, D_MODEL)
    pp = p_prompt[0].reshape(batch * seq, PLE_DIM)
    hp = _ffn(xp, ng, *f1, 0, 1, TM_PROMPT)
    hp, sg_p, sc_p = _mixer_prompt(hp, batch, seq, TM_PROMPT, *mix_w)
    hp = _ffn(hp, ng, *f2, 4, 5, TM_PROMPT, ple_args=(pp, wpp, wpg))

    xs = x_sample.reshape(nb, D_MODEL)
    ps = p_sample[0].reshape(nb, PLE_DIM)
    s0 = state_gla[0].reshape(nb, HK, GLA_DV)
    c0 = state_conv[0].reshape(nb, (CONV_K - 1) * CONV_WIDTH)
    hs = _ffn(xs, ng, *f1, 0, 1, nb)
    hs, sg_s, sc_s = _mixer_sample(hs, s0, c0, BS_SAMPLE, *mix_w)
    hs = _ffn(hs, ng, *f2, 4, 5, nb, ple_args=(ps, wpp, wpg))

    return (hp.reshape(batch, seq, D_MODEL),
            hs.reshape(nb, 1, D_MODEL),
            sg_p.reshape(1, batch, GLA_HEADS, GLA_DK, GLA_DV),
            sc_p.reshape(1, batch, CONV_K - 1, CONV_WIDTH),
            sg_s.reshape(1, nb, GLA_HEADS, GLA_DK, GLA_DV),
            sc_s.reshape(1, nb, CONV_K - 1, CONV_WIDTH))
```

```python
import functools

import numpy as np
import jax
import jax.numpy as jnp
from jax import lax
from jax.experimental import pallas as pl
from jax.experimental.pallas import tpu as pltpu

F32 = jnp.float32
BF16 = jnp.bfloat16

D_MODEL = 1024
PLE_DIM = 256
GLA_HEADS = 4
GLA_DK = 64
GLA_DV = 128
HK = GLA_HEADS * GLA_DK
HV = GLA_HEADS * GLA_DV
GLA_RANK = 16
GLA_GATE_NORM = 16.0
CONV_WIDTH = 512
CONV_K = 3
D_FF = 2816
EPS = 1e-6

LANES = 128
MXU_COLS = 256
CHUNK = 64
N_LEVELS = 7
VMEM_CAP = 56 * 1024 * 1024

NT_DIMS = (((1,), (1,)), ((), ()))
TN_DIMS = (((0,), (0,)), ((), ()))


def _rms(x, g):
    return x * lax.rsqrt(jnp.mean(x * x, axis=-1, keepdims=True) + EPS) * g


def _sigmoid(x):
    return 1.0 / (1.0 + jnp.exp(-x))


def _log_sigmoid(x):
    return jnp.minimum(x, 0.0) - jnp.log1p(jnp.exp(-jnp.abs(x)))


def _split3(x):
    hi = x.astype(BF16)
    r1 = x - hi.astype(F32)
    mid = r1.astype(BF16)
    lo = (r1 - mid.astype(F32)).astype(BF16)
    return hi, mid, lo


def _resident(shape):
    nd = len(shape)
    return pl.BlockSpec(shape, lambda *_: (0,) * nd, pipeline_mode=pl.Buffered(1))


def _nbytes(shape, dtype):
    return int(np.prod(shape)) * jnp.dtype(dtype).itemsize


def _vmem_limit(est_bytes):
    return int(min(VMEM_CAP, max(est_bytes * 5 // 4, 16 * 1024 * 1024)))


def _ffn_body(pre, post, ple, *refs):
    if ple:
        x_ref, ng_ref, wg_ref, wu_ref, wd_ref, p_ref, wpp_ref, wpg_ref, o_ref, h_scr = refs
    else:
        x_ref, ng_ref, wg_ref, wu_ref, wd_ref, o_ref, h_scr = refs
    x = x_ref[...]
    xn = _rms(x, ng_ref[pre:pre + 1, :]).astype(BF16)
    for c in range(D_FF // MXU_COLS):
        sl = slice(c * MXU_COLS, (c + 1) * MXU_COLS)
        g = jnp.dot(xn, wg_ref[:, sl], preferred_element_type=F32)
        u = jnp.dot(xn, wu_ref[:, sl], preferred_element_type=F32)
        h_scr[:, sl] = (g * _sigmoid(g) * u).astype(BF16)
    y = jnp.dot(h_scr[...], wd_ref[...], preferred_element_type=F32)
    h = x + 0.5 * _rms(y, ng_ref[post:post + 1, :])
    if ple:
        hn = _rms(h, ng_ref[6:7, :]).astype(BF16)
        gate = _sigmoid(jnp.dot(hn, wpg_ref[...], preferred_element_type=F32))
        proj = jnp.dot(p_ref[...].astype(BF16), wpp_ref[...], preferred_element_type=F32)
        h = h + _rms(proj * gate, ng_ref[7:8, :])
    o_ref[...] = h


def _ffn(x, ng, wg, wu, wd, pre, post, tm, ple_args=None):
    rows = x.shape[0]
    assert rows % tm == 0
    ple = ple_args is not None
    row_spec = pl.BlockSpec((tm, D_MODEL), lambda i: (i, 0))
    in_specs = [row_spec, _resident(ng.shape), _resident(wg.shape), _resident(wu.shape), _resident(wd.shape)]
    args = [x, ng, wg, wu, wd]
    est = (4 * _nbytes((tm, D_MODEL), F32) + 3 * _nbytes(wg.shape, BF16) + _nbytes((tm, D_FF), BF16)
           + 6 * _nbytes((tm, MXU_COLS), F32) + 3 * _nbytes((tm, D_MODEL), F32))
    if ple:
        p, wpp, wpg = ple_args
        in_specs += [pl.BlockSpec((tm, PLE_DIM), lambda i: (i, 0)), _resident(wpp.shape), _resident(wpg.shape)]
        args += [p, wpp, wpg]
        est += 2 * _nbytes((tm, PLE_DIM), F32) + _nbytes(wpp.shape, BF16) + _nbytes(wpg.shape, BF16)
        est += 3 * _nbytes((tm, D_MODEL), F32)
    return pl.pallas_call(
        functools.partial(_ffn_body, pre, post, ple),
        grid=(rows // tm,),
        in_specs=in_specs,
        out_specs=row_spec,
        out_shape=jax.ShapeDtypeStruct((rows, D_MODEL), F32),
        scratch_shapes=[pltpu.VMEM((tm, D_FF), BF16)],
        compiler_params=pltpu.CompilerParams(
            dimension_semantics=("arbitrary",), vmem_limit_bytes=_vmem_limit(est)),
        name="ffn_ple" if ple else "ffn",
    )(*args)


def _dot_by_cols(x, w_ref):
    n = w_ref.shape[1]
    return jnp.concatenate(
        [jnp.dot(x, w_ref[:, c:c + MXU_COLS], preferred_element_type=F32) for c in range(0, n, MXU_COLS)],
        axis=1)


def _mixer_inputs(h, ng_ref, wa_ref, wr_ref, wgu_ref, bg_ref, wc_ref):
    u = _rms(h, ng_ref[2:3, :]).astype(BF16)
    r = jnp.dot(u, wr_ref[...], preferred_element_type=F32)
    logit = jnp.dot(r.astype(BF16), wgu_ref[...], preferred_element_type=F32) + bg_ref[...]
    la = _log_sigmoid(logit) * (1.0 / GLA_GATE_NORM)
    za = _dot_by_cols(u, wa_ref)
    zc = _dot_by_cols(u, wc_ref)
    q = za[:, 0:HK] * (GLA_DK ** -0.5)
    k = za[:, HK:2 * HK]
    v = za[:, 2 * HK:2 * HK + HV]
    g = za[:, 2 * HK + HV:2 * HK + 2 * HV]
    cb = zc[:, 0:CONV_WIDTH]
    cc = zc[:, CONV_WIDTH:2 * CONV_WIDTH]
    ch = zc[:, 2 * CONV_WIDTH:3 * CONV_WIDTH]
    return q, k, v, g, la, cb, cc, ch


def _mixer_output(h, o, g, oc, ng_ref, gng_ref, wout_ref):
    heads = []
    for hd in range(GLA_HEADS):
        sl = slice(hd * GLA_DV, (hd + 1) * GLA_DV)
        heads.append(_rms(o[:, sl], gng_ref[...]))
    on = jnp.concatenate(heads, axis=1) * (g * _sigmoid(g))
    cat = jnp.concatenate([on.astype(BF16), oc.astype(BF16)], axis=1)
    mix = _dot_by_cols(cat, wout_ref)
    return h + _rms(mix, ng_ref[3:4, :])


def _chunk_constants():
    t = np.arange(CHUNK)[:, None]
    s = np.arange(CHUNK)[None, :]
    blocks = []
    for j in range(1, N_LEVELS - 1):
        half = 1 << j
        mid = (t // (2 * half)) * (2 * half) + half
        blocks.append((s <= t).astype(np.float32) - (s < mid).astype(np.float32))
    blocks.append((s <= t).astype(np.float32))
    m = np.concatenate(blocks, axis=0)
    mst = np.concatenate([m, m, m], axis=1)
    x = np.bitwise_xor(t, s)
    top = np.floor(np.log2(np.maximum(x, 1))).astype(np.int32) + 1
    lvl = np.where(s > t, -1, np.where(s == t, 0, top)).astype(np.int32)
    return jnp.asarray(mst, BF16), jnp.asarray(np.tile(lvl, (1, GLA_HEADS)), jnp.int32)


def _block_diag_rows(x, lane_head):
    zero = jnp.zeros_like(x)
    return jnp.concatenate([jnp.where(lane_head == hd, x, zero) for hd in range(GLA_HEADS)], axis=0)


def _gla_tile(q, k, v, la, st, mst, lvl, lane_head_k, lane_head_v, lane_head_s):
    c = CHUNK
    chunks = [slice(r, r + c) for r in range(0, q.shape[0], c)]
    cat = lambda xs: jnp.concatenate(xs, axis=0)

    hi, mid, lo = _split3(la)
    e = [jnp.dot(mst, cat([hi[r, :], mid[r, :], lo[r, :]]), preferred_element_type=F32) for r in chunks]
    bc = cat([en[5 * c:6 * c, :] for en in e])
    b_last = [en[6 * c - 1:6 * c, :] for en in e]
    b_last_rows = cat([jnp.broadcast_to(b, (c, HK)) for b in b_last])

    vb = v.astype(BF16)
    k_dec = (k * jnp.exp(b_last_rows - bc)).astype(BF16)
    upd_all = [lax.dot_general(vb[r, :], k_dec[r, :], TN_DIMS, preferred_element_type=F32) for r in chunks]
    states = []
    for n in range(len(chunks)):
        states.append(st)
        upd = jnp.zeros_like(st)
        for hd in range(GLA_HEADS):
            upd = jnp.where(lane_head_s == hd, upd_all[n][hd * GLA_DV:(hd + 1) * GLA_DV, :], upd)
        st = st * jnp.exp(b_last[n]) + upd

    qb = q.astype(BF16)
    q1 = (q * jnp.exp(la)).astype(BF16)
    kb = k.astype(BF16)
    x01 = [lax.dot_general(cat([qb[r, :], q1[r, :]]), _block_diag_rows(kb[r, :], lane_head_k), NT_DIMS,
                           preferred_element_type=F32) for r in chunks]
    att = [jnp.where(lvl == 1, x[c:, :], jnp.where(lvl == 0, x[:c, :], 0.0)) for x in x01]
    for lv in range(2, N_LEVELS):
        el = cat([en[(lv - 2) * c:(lv - 1) * c, :] for en in e])
        ql = (q * jnp.exp(jnp.minimum(el, 0.0))).astype(BF16)
        kl = (k * jnp.exp(jnp.minimum(-el, 0.0))).astype(BF16)
        xl = [lax.dot_general(ql[r, :], _block_diag_rows(kl[r, :], lane_head_k), NT_DIMS,
                              preferred_element_type=F32) for r in chunks]
        att = [jnp.where(lvl == lv, x, a) for x, a in zip(xl, att)]

    q_in = (q * jnp.exp(bc)).astype(BF16)
    o = [lax.dot_general(q_in[r, :], _block_diag_rows(s, lane_head_s).astype(BF16), NT_DIMS,
                         preferred_element_type=F32)
         + jnp.dot(a.astype(BF16), _block_diag_rows(vb[r, :], lane_head_v), preferred_element_type=F32)
         for r, s, a in zip(chunks, states, att)]
    return cat(o), st


def _mixer_prompt_body(tm, h_ref, ng_ref, wa_ref, wr_ref, wgu_ref, bg_ref, wc_ref, gng_ref, cw_ref,
                       wout_ref, mst_ref, lvl_ref, o_ref, sg_ref, sc_ref, st_scr, carry_scr):
    j = pl.program_id(1)

    @pl.when(j == 0)
    def _():
        st_scr[...] = jnp.zeros_like(st_scr)
        carry_scr[...] = jnp.zeros_like(carry_scr)

    h = h_ref[...]
    q, k, v, g, la, cb, cc, ch = _mixer_inputs(h, ng_ref, wa_ref, wr_ref, wgu_ref, bg_ref, wc_ref)

    lane_head_k = lax.broadcasted_iota(jnp.int32, (CHUNK, HK), 1) // GLA_DK
    lane_head_v = lax.broadcasted_iota(jnp.int32, (CHUNK, HV), 1) // GLA_DV
    lane_head_s = lax.broadcasted_iota(jnp.int32, (GLA_DV, HK), 1) // GLA_DK

    og, st = _gla_tile(q, k, v, la, st_scr[...], mst_ref[...], lvl_ref[...],
                       lane_head_k, lane_head_v, lane_head_s)
    st_scr[...] = st

    uc = cc * ch
    row = lax.broadcasted_iota(jnp.int32, uc.shape, 0)
    c1 = carry_scr[7:8, :]
    c2 = carry_scr[6:7, :]
    prev1 = jnp.where(row == 0, c1, pltpu.roll(uc, 1, axis=0))
    prev2 = jnp.where(row == 0, c2, jnp.where(row == 1, c1, pltpu.roll(uc, 2, axis=0)))
    y = cw_ref[0:1, :] * prev2 + cw_ref[1:2, :] * prev1 + cw_ref[2:3, :] * uc
    carry_scr[...] = uc[tm - 8:tm, :]

    o_ref[...] = _mixer_output(h, og, g, cb * y, ng_ref, gng_ref, wout_ref)

    @pl.when(j == pl.num_programs(1) - 1)
    def _():
        sg_ref[0] = st.T
        sc_ref[0] = carry_scr[6:8, :]


def _mixer_prompt(h, batch, seq, tm, ng, wa, wr, wgu, bg, wc, gng, cw, wout):
    assert seq % tm == 0 and tm % CHUNK == 0
    nt = seq // tm
    mst, lvl = _chunk_constants()
    consts = [ng, wa, wr, wgu, bg, wc, gng, cw, wout, mst, lvl]
    row_spec = pl.BlockSpec((tm, D_MODEL), lambda b, j: (b * nt + j, 0))
    est = (4 * _nbytes((tm, D_MODEL), F32) + sum(_nbytes(c.shape, c.dtype) for c in consts)
           + 2 * _nbytes((tm, 3 * HV), F32) + _nbytes((tm, 3 * HK + 2 * HV), F32)
           + 6 * _nbytes((tm, D_MODEL), F32))
    return pl.pallas_call(
        functools.partial(_mixer_prompt_body, tm),
        grid=(batch, nt),
        in_specs=[row_spec] + [_resident(c.shape) for c in consts],
        out_specs=[row_spec,
                   pl.BlockSpec((1, HK, GLA_DV), lambda b, j: (b, 0, 0)),
                   pl.BlockSpec((1, CONV_K - 1, CONV_WIDTH), lambda b, j: (b, 0, 0))],
        out_shape=[jax.ShapeDtypeStruct((batch * seq, D_MODEL), F32),
                   jax.ShapeDtypeStruct((batch, HK, GLA_DV), F32),
                   jax.ShapeDtypeStruct((batch, CONV_K - 1, CONV_WIDTH), F32)],
        scratch_shapes=[pltpu.VMEM((GLA_DV, HK), F32), pltpu.VMEM((8, CONV_WIDTH), F32)],
        compiler_params=pltpu.CompilerParams(
            dimension_semantics=("arbitrary", "arbitrary"), vmem_limit_bytes=_vmem_limit(est)),
        name="mixer_prompt",
    )(h, *consts)


def _mixer_sample_body(nb, bs, h_ref, s_ref, c0_ref, ng_ref, wa_ref, wr_ref, wgu_ref, bg_ref, wc_ref,
                       gng_ref, cw_ref, wout_ref, o_ref, so_ref, co_ref,
                       q_scr, k_scr, la3_scr, v_scr, g_scr, oc_scr, og_scr):
    i = pl.program_id(0)

    @pl.when(i == 0)
    def _():
        q, k, v, g, la, cb, cc, ch = _mixer_inputs(h_ref[...], ng_ref, wa_ref, wr_ref, wgu_ref, bg_ref, wc_ref)
        q_scr[...] = q.astype(BF16)
        k_scr[...] = k.astype(BF16)
        la3_scr[...] = jnp.concatenate(_split3(la), axis=0)
        v_scr[...] = v
        g_scr[...] = g
        uc = cc * ch
        c0 = c0_ref[...]
        old2 = c0[:, 0:CONV_WIDTH]
        old1 = c0[:, CONV_WIDTH:2 * CONV_WIDTH]
        y = cw_ref[0:1, :] * old2 + cw_ref[1:2, :] * old1 + cw_ref[2:3, :] * uc
        oc_scr[...] = cb * y
        co_ref[...] = jnp.concatenate([old1, uc], axis=1)

    seq_row = lax.broadcasted_iota(jnp.int32, (nb, LANES), 0)

    def seq_step(t, carry):
        b = i * bs + t
        pick = jnp.where(seq_row == b, 1.0, 0.0).astype(BF16)
        pick3 = jnp.concatenate([pick, pick, pick], axis=0)
        la_t = lax.dot_general(la3_scr[...], pick3, TN_DIMS, preferred_element_type=F32)
        k_t = lax.dot_general(k_scr[...], pick, TN_DIMS, preferred_element_type=F32)
        q_t = lax.dot_general(q_scr[...], pick, TN_DIMS, preferred_element_type=F32)
        v_row = v_scr[pl.ds(b, 1), :]
        v_b = jnp.concatenate(
            [jnp.broadcast_to(v_row[:, hd * GLA_DV:(hd + 1) * GLA_DV], (GLA_DK, GLA_DV))
             for hd in range(GLA_HEADS)], axis=0)
        s_new = jnp.exp(la_t) * s_ref[t] + k_t * v_b
        so_ref[t] = s_new
        qs = q_t * s_new
        o_row = jnp.concatenate(
            [jnp.sum(qs[hd * GLA_DK:(hd + 1) * GLA_DK, :], axis=0, keepdims=True)
             for hd in range(GLA_HEADS)], axis=1)
        og_scr[pl.ds(b, 1), :] = o_row
        return carry

    lax.fori_loop(0, bs, seq_step, 0)

    @pl.when(i == pl.num_programs(0) - 1)
    def _():
        o_ref[...] = _mixer_output(h_ref[...], og_scr[...], g_scr[...], oc_scr[...], ng_ref, gng_ref, wout_ref)


def _mixer_sample(h, s0, c0, bs, ng, wa, wr, wgu, bg, wc, gng, cw, wout):
    nb = h.shape[0]
    assert nb == LANES and nb % bs == 0
    consts = [ng, wa, wr, wgu, bg, wc, gng, cw, wout]
    state_spec = pl.BlockSpec((bs, HK, GLA_DV), lambda i: (i, 0, 0))
    est = (sum(_nbytes(c.shape, c.dtype) for c in consts) + 4 * _nbytes((bs, HK, GLA_DV), F32)
           + 12 * _nbytes((nb, D_MODEL), F32) + 2 * _nbytes((nb, 3 * HV), F32))
    return pl.pallas_call(
        functools.partial(_mixer_sample_body, nb, bs),
        grid=(nb // bs,),
        in_specs=[_resident(h.shape), state_spec, _resident(c0.shape)] + [_resident(c.shape) for c in consts],
        out_specs=[pl.BlockSpec((nb, D_MODEL), lambda i: (0, 0)), state_spec,
                   pl.BlockSpec(c0.shape, lambda i: (0, 0))],
        out_shape=[jax.ShapeDtypeStruct((nb, D_MODEL), F32),
                   jax.ShapeDtypeStruct(s0.shape, F32),
                   jax.ShapeDtypeStruct(c0.shape, F32)],
        scratch_shapes=[pltpu.VMEM((nb, HK), BF16), pltpu.VMEM((nb, HK), BF16), pltpu.VMEM((3 * nb, HK), BF16),
                        pltpu.VMEM((nb, HV), F32), pltpu.VMEM((nb, HV), F32), pltpu.VMEM((nb, CONV_WIDTH), F32),
                        pltpu.VMEM((nb, HV), F32)],
        compiler_params=pltpu.CompilerParams(
            dimension_semantics=("arbitrary",), vmem_limit_bytes=_vmem_limit(est)),
        name="mixer_sample",
    )(h, s0, c0, *consts)


TM_PROMPT = 512
BS_SAMPLE = 16


def kernel(x_prompt, x_sample, state_gla, state_conv, p_prompt, p_sample, norm_g, w_in, w_gate_up, b_gate,
           gla_norm_g, conv_w, w_out, ffn1_gate, ffn1_up, ffn1_down, ffn2_gate, ffn2_up, ffn2_down,
           w_ple_proj, w_ple_gate):
    batch, seq, _ = x_prompt.shape
    nb = x_sample.shape[0]
    assert norm_g.shape[0] == 1 and x_sample.shape[1] == 1

    ng = norm_g[0]
    wi = w_in[0]
    r0 = 2 * HK + 2 * HV
    wa = wi[:, :r0].astype(BF16)
    wr = jnp.pad(wi[:, r0:r0 + GLA_RANK], ((0, 0), (0, LANES - GLA_RANK))).astype(BF16)
    wc = wi[:, r0 + GLA_RANK:].astype(BF16)
    wgu = jnp.pad(w_gate_up[0], ((0, LANES - GLA_RANK), (0, 0))).astype(BF16)
    bg = b_gate[0].reshape(1, HK)
    gng = gla_norm_g[0].reshape(1, GLA_DV)
    cw = conv_w[0]
    wout = w_out[0].astype(BF16)
    f1 = (ffn1_gate[0].astype(BF16), ffn1_up[0].astype(BF16), ffn1_down[0].astype(BF16))
    f2 = (ffn2_gate[0].astype(BF16), ffn2_up[0].astype(BF16), ffn2_down[0].astype(BF16))
    wpp = w_ple_proj[0].astype(BF16)
    wpg = w_ple_gate[0].astype(BF16)
    mix_w = (ng, wa, wr, wgu, bg, wc, gng, cw, wout)

    xp = x_prompt.reshape(batch * seq, D_MODEL)
    pp = p_prompt[0].reshape(batch * seq, PLE_DIM)
    hp = _ffn(xp, ng, *f1, 0, 1, TM_PROMPT)
    hp, sg_p, sc_p = _mixer_prompt(hp, batch, seq, TM_PROMPT, *mix_w)
    hp = _ffn(hp, ng, *f2, 4, 5, TM_PROMPT, ple_args=(pp, wpp, wpg))

    xs = x_sample.reshape(nb, D_MODEL)
    ps = p_sample[0].reshape(nb, PLE_DIM)
    s0 = state_gla[0].reshape(nb, HK, GLA_DV)
    c0 = state_conv[0].reshape(nb, (CONV_K - 1) * CONV_WIDTH)
    hs = _ffn(xs, ng, *f1, 0, 1, nb)
    hs, sg_s, sc_s = _mixer_sample(hs, s0, c0, BS_SAMPLE, *mix_w)
    hs = _ffn(hs, ng, *f2, 4, 5, nb, ple_args=(ps, wpp, wpg))

    return (hp.reshape(batch, seq, D_MODEL),
            hs.reshape(nb, 1, D_MODEL),
            sg_p.reshape(1, batch, GLA_HEADS, GLA_DK, GLA_DV),
            sc_p.reshape(1, batch, CONV_K - 1, CONV_WIDTH),
            sg_s.reshape(1, nb, GLA_HEADS, GLA_DK, GLA_DV),
            sc_s.reshape(1, nb, CONV_K - 1, CONV_WIDTH))
```

```python
import functools

import numpy as np
import jax
import jax.numpy as jnp
from jax import lax
from jax.experimental import pallas as pl
from jax.experimental.pallas import tpu as pltpu

F32 = jnp.float32
BF16 = jnp.bfloat16

D_MODEL = 1024
PLE_DIM = 256
GLA_HEADS = 4
GLA_DK = 64
GLA_DV = 128
HK = GLA_HEADS * GLA_DK
HV = GLA_HEADS * GLA_DV
GLA_RANK = 16
GLA_GATE_NORM = 16.0
CONV_WIDTH = 512
CONV_K = 3
D_FF = 2816
EPS = 1e-6

LANES = 128
MXU_COLS = 256
CHUNK = 64
N_LEVELS = 7
VMEM_CAP = 56 * 1024 * 1024

NT_DIMS = (((1,), (1,)), ((), ()))
TN_DIMS = (((0,), (0,)), ((), ()))


def _inv_rms(x):
    return lax.rsqrt(jnp.mean(x * x, axis=-1, keepdims=True) + EPS)


def _rms(x, g):
    return x * _inv_rms(x) * g


def _gain_into_rows(w, g):
    return (g[:, None] * w).astype(BF16)


def _sigmoid(x):
    return 1.0 / (1.0 + jnp.exp(-x))


def _log_sigmoid(x):
    return jnp.minimum(x, 0.0) - jnp.log1p(jnp.exp(-jnp.abs(x)))


def _split3(x):
    hi = x.astype(BF16)
    r1 = x - hi.astype(F32)
    mid = r1.astype(BF16)
    lo = (r1 - mid.astype(F32)).astype(BF16)
    return hi, mid, lo


def _resident(shape):
    nd = len(shape)
    return pl.BlockSpec(shape, lambda *_: (0,) * nd, pipeline_mode=pl.Buffered(1))


def _nbytes(shape, dtype):
    return int(np.prod(shape)) * jnp.dtype(dtype).itemsize


def _vmem_limit(est_bytes):
    return int(min(VMEM_CAP, max(est_bytes * 5 // 4, 16 * 1024 * 1024)))


def _ffn_body(post, ple, ts, nsub, *refs):
    if ple:
        x_ref, ng_ref, wg_ref, wu_ref, wd_ref, p_ref, wpp_ref, wpg_ref, o_ref, h_scr = refs
    else:
        x_ref, ng_ref, wg_ref, wu_ref, wd_ref, o_ref, h_scr = refs

    def gate_up(s, beside):
        x = x_ref[s * ts:(s + 1) * ts, :]
        xb = x.astype(BF16)
        inv = _inv_rms(x)
        for c in range(D_FF // MXU_COLS):
            sl = slice(c * MXU_COLS, (c + 1) * MXU_COLS)
            g = jnp.dot(xb, wg_ref[:, sl], preferred_element_type=F32) * inv
            u = jnp.dot(xb, wu_ref[:, sl], preferred_element_type=F32) * inv
            h_scr[s % 2, :, sl] = (g * _sigmoid(g) * u).astype(BF16)
            if c == 0 and beside is not None:
                beside()

    def down(s):
        return jnp.dot(h_scr[s % 2], wd_ref[...], preferred_element_type=F32)

    def epilogue(s, y):
        rows = slice(s * ts, (s + 1) * ts)
        h = x_ref[rows, :] + 0.5 * _rms(y, ng_ref[post:post + 1, :])
        if ple:
            gate = _sigmoid(jnp.dot(h.astype(BF16), wpg_ref[...], preferred_element_type=F32) * _inv_rms(h))
            proj = jnp.dot(p_ref[rows, :].astype(BF16), wpp_ref[...], preferred_element_type=F32)
            h = h + _rms(proj * gate, ng_ref[7:8, :])
        o_ref[rows, :] = h

    pending = None
    for s in range(nsub):
        gate_up(s, pending)
        y = down(s)
        pending = functools.partial(epilogue, s, y)
    pending()


def _ffn(x, ng, wg, wu, wd, post, ts, nsub, ple_args=None):
    rows = x.shape[0]
    tm = nsub * ts
    assert rows % tm == 0
    ple = ple_args is not None
    row_spec = pl.BlockSpec((tm, D_MODEL), lambda i: (i, 0))
    in_specs = [row_spec, _resident(ng.shape), _resident(wg.shape), _resident(wu.shape), _resident(wd.shape)]
    args = [x, ng, wg, wu, wd]
    est = (4 * _nbytes((tm, D_MODEL), F32) + 3 * _nbytes(wg.shape, BF16)
           + 2 * _nbytes((ts, D_FF), BF16) + _nbytes((ts, D_MODEL), BF16)
           + 6 * _nbytes((ts, MXU_COLS), F32) + 6 * _nbytes((ts, D_MODEL), F32))
    if ple:
        p, wpp, wpg = ple_args
        in_specs += [pl.BlockSpec((tm, PLE_DIM), lambda i: (i, 0)), _resident(wpp.shape), _resident(wpg.shape)]
        args += [p, wpp, wpg]
        est += 2 * _nbytes((tm, PLE_DIM), F32) + _nbytes(wpp.shape, BF16) + _nbytes(wpg.shape, BF16)
        est += 3 * _nbytes((ts, D_MODEL), F32)
    return pl.pallas_call(
        functools.partial(_ffn_body, post, ple, ts, nsub),
        grid=(rows // tm,),
        in_specs=in_specs,
        out_specs=row_spec,
        out_shape=jax.ShapeDtypeStruct((rows, D_MODEL), F32),
        scratch_shapes=[pltpu.VMEM((2, ts, D_FF), BF16)],
        compiler_params=pltpu.CompilerParams(
            dimension_semantics=("arbitrary",), vmem_limit_bytes=_vmem_limit(est)),
        name="ffn_ple" if ple else "ffn",
    )(*args)


def _dot_by_cols(x, w_ref):
    n = w_ref.shape[1]
    return jnp.concatenate(
        [jnp.dot(x, w_ref[:, c:c + MXU_COLS], preferred_element_type=F32) for c in range(0, n, MXU_COLS)],
        axis=1)


def _mixer_inputs(h, ng_ref, wa_ref, wr_ref, wgu_ref, bg_ref, wc_ref):
    hb = h.astype(BF16)
    inv = _inv_rms(h)
    r = jnp.dot(hb, wr_ref[...], preferred_element_type=F32) * inv
    logit = jnp.dot(r.astype(BF16), wgu_ref[...], preferred_element_type=F32) + bg_ref[...]
    la = _log_sigmoid(logit) * (1.0 / GLA_GATE_NORM)
    za = _dot_by_cols(hb, wa_ref) * inv
    zc = _dot_by_cols(hb, wc_ref) * inv
    q = za[:, 0:HK] * (GLA_DK ** -0.5)
    k = za[:, HK:2 * HK]
    v = za[:, 2 * HK:2 * HK + HV]
    g = za[:, 2 * HK + HV:2 * HK + 2 * HV]
    cb = zc[:, 0:CONV_WIDTH]
    cc = zc[:, CONV_WIDTH:2 * CONV_WIDTH]
    ch = zc[:, 2 * CONV_WIDTH:3 * CONV_WIDTH]
    return q, k, v, g, la, cb, cc, ch


def _mixer_output(h, o, g, oc, ng_ref, gng_ref, wout_ref):
    heads = []
    for hd in range(GLA_HEADS):
        sl = slice(hd * GLA_DV, (hd + 1) * GLA_DV)
        heads.append(_rms(o[:, sl], gng_ref[...]))
    on = jnp.concatenate(heads, axis=1) * (g * _sigmoid(g))
    cat = jnp.concatenate([on.astype(BF16), oc.astype(BF16)], axis=1)
    mix = _dot_by_cols(cat, wout_ref)
    return h + _rms(mix, ng_ref[3:4, :])


def _chunk_constants():
    t = np.arange(CHUNK)[:, None]
    s = np.arange(CHUNK)[None, :]
    blocks = []
    for j in range(1, N_LEVELS - 1):
        half = 1 << j
        mid = (t // (2 * half)) * (2 * half) + half
        blocks.append((s <= t).astype(np.float32) - (s < mid).astype(np.float32))
    blocks.append((s <= t).astype(np.float32))
    m = np.concatenate(blocks, axis=0)
    mst = np.concatenate([m, m, m], axis=1)
    x = np.bitwise_xor(t, s)
    top = np.floor(np.log2(np.maximum(x, 1))).astype(np.int32) + 1
    lvl = np.where(s > t, -1, np.where(s == t, 0, top)).astype(np.int32)
    return jnp.asarray(mst, BF16), jnp.asarray(np.tile(lvl, (1, GLA_HEADS)), jnp.int32)


def _block_diag_rows(x, lane_head):
    zero = jnp.zeros_like(x)
    return jnp.concatenate([jnp.where(lane_head == hd, x, zero) for hd in range(GLA_HEADS)], axis=0)


def _gla_tile(q, k, v, la, st, mst, lvl, lane_head_k, lane_head_v, lane_head_s):
    c = CHUNK
    chunks = [slice(r, r + c) for r in range(0, q.shape[0], c)]
    cat = lambda xs: jnp.concatenate(xs, axis=0)

    hi, mid, lo = _split3(la)
    e = [jnp.dot(mst, cat([hi[r, :], mid[r, :], lo[r, :]]), preferred_element_type=F32) for r in chunks]
    bc = cat([en[5 * c:6 * c, :] for en in e])
    b_last = [en[6 * c - 1:6 * c, :] for en in e]
    b_last_rows = cat([jnp.broadcast_to(b, (c, HK)) for b in b_last])

    vb = v.astype(BF16)
    k_dec = (k * jnp.exp(b_last_rows - bc)).astype(BF16)
    upd_all = [lax.dot_general(vb[r, :], k_dec[r, :], TN_DIMS, preferred_element_type=F32) for r in chunks]
    states = []
    for n in range(len(chunks)):
        states.append(st)
        upd = jnp.zeros_like(st)
        for hd in range(GLA_HEADS):
            upd = jnp.where(lane_head_s == hd, upd_all[n][hd * GLA_DV:(hd + 1) * GLA_DV, :], upd)
        st = st * jnp.exp(b_last[n]) + upd

    qb = q.astype(BF16)
    q1 = (q * jnp.exp(la)).astype(BF16)
    kb = k.astype(BF16)
    x01 = [lax.dot_general(cat([qb[r, :], q1[r, :]]), _block_diag_rows(kb[r, :], lane_head_k), NT_DIMS,
                           preferred_element_type=F32) for r in chunks]
    att = [jnp.where(lvl == 1, x[c:, :], jnp.where(lvl == 0, x[:c, :], 0.0)) for x in x01]
    for lv in range(2, N_LEVELS):
        el = cat([en[(lv - 2) * c:(lv - 1) * c, :] for en in e])
        ql = (q * jnp.exp(jnp.minimum(el, 0.0))).astype(BF16)
        kl = (k * jnp.exp(jnp.minimum(-el, 0.0))).astype(BF16)
        xl = [lax.dot_general(ql[r, :], _block_diag_rows(kl[r, :], lane_head_k), NT_DIMS,
                              preferred_element_type=F32) for r in chunks]
        att = [jnp.where(lvl == lv, x, a) for x, a in zip(xl, att)]

    q_in = (q * jnp.exp(bc)).astype(BF16)
    o = [lax.dot_general(q_in[r, :], _block_diag_rows(s, lane_head_s).astype(BF16), NT_DIMS,
                         preferred_element_type=F32)
         + jnp.dot(a.astype(BF16), _block_diag_rows(vb[r, :], lane_head_v), preferred_element_type=F32)
         for r, s, a in zip(chunks, states, att)]
    return cat(o), st


def _mixer_prompt_body(tm, h_ref, ng_ref, wa_ref, wr_ref, wgu_ref, bg_ref, wc_ref, gng_ref, cw_ref,
                       wout_ref, mst_ref, lvl_ref, o_ref, sg_ref, sc_ref, st_scr, carry_scr):
    j = pl.program_id(1)

    @pl.when(j == 0)
    def _():
        st_scr[...] = jnp.zeros_like(st_scr)
        carry_scr[...] = jnp.zeros_like(carry_scr)

    h = h_ref[...]
    q, k, v, g, la, cb, cc, ch = _mixer_inputs(h, ng_ref, wa_ref, wr_ref, wgu_ref, bg_ref, wc_ref)

    lane_head_k = lax.broadcasted_iota(jnp.int32, (CHUNK, HK), 1) // GLA_DK
    lane_head_v = lax.broadcasted_iota(jnp.int32, (CHUNK, HV), 1) // GLA_DV
    lane_head_s = lax.broadcasted_iota(jnp.int32, (GLA_DV, HK), 1) // GLA_DK

    og, st = _gla_tile(q, k, v, la, st_scr[...], mst_ref[...], lvl_ref[...],
                       lane_head_k, lane_head_v, lane_head_s)
    st_scr[...] = st

    uc = cc * ch
    row = lax.broadcasted_iota(jnp.int32, uc.shape, 0)
    c1 = carry_scr[7:8, :]
    c2 = carry_scr[6:7, :]
    prev1 = jnp.where(row == 0, c1, pltpu.roll(uc, 1, axis=0))
    prev2 = jnp.where(row == 0, c2, jnp.where(row == 1, c1, pltpu.roll(uc, 2, axis=0)))
    y = cw_ref[0:1, :] * prev2 + cw_ref[1:2, :] * prev1 + cw_ref[2:3, :] * uc
    carry_scr[...] = uc[tm - 8:tm, :]

    o_ref[...] = _mixer_output(h, og, g, cb * y, ng_ref, gng_ref, wout_ref)

    @pl.when(j == pl.num_programs(1) - 1)
    def _():
        sg_ref[0] = st.T
        sc_ref[0] = carry_scr[6:8, :]


def _mixer_prompt(h, batch, seq, tm, ng, wa, wr, wgu, bg, wc, gng, cw, wout):
    assert seq % tm == 0 and tm % CHUNK == 0
    nt = seq // tm
    mst, lvl = _chunk_constants()
    consts = [ng, wa, wr, wgu, bg, wc, gng, cw, wout, mst, lvl]
    row_spec = pl.BlockSpec((tm, D_MODEL), lambda b, j: (b * nt + j, 0))
    est = (4 * _nbytes((tm, D_MODEL), F32) + sum(_nbytes(c.shape, c.dtype) for c in consts)
           + 2 * _nbytes((tm, 3 * HV), F32) + _nbytes((tm, 3 * HK + 2 * HV), F32)
           + 6 * _nbytes((tm, D_MODEL), F32))
    return pl.pallas_call(
        functools.partial(_mixer_prompt_body, tm),
        grid=(batch, nt),
        in_specs=[row_spec] + [_resident(c.shape) for c in consts],
        out_specs=[row_spec,
                   pl.BlockSpec((1, HK, GLA_DV), lambda b, j: (b, 0, 0)),
                   pl.BlockSpec((1, CONV_K - 1, CONV_WIDTH), lambda b, j: (b, 0, 0))],
        out_shape=[jax.ShapeDtypeStruct((batch * seq, D_MODEL), F32),
                   jax.ShapeDtypeStruct((batch, HK, GLA_DV), F32),
                   jax.ShapeDtypeStruct((batch, CONV_K - 1, CONV_WIDTH), F32)],
        scratch_shapes=[pltpu.VMEM((GLA_DV, HK), F32), pltpu.VMEM((8, CONV_WIDTH), F32)],
        compiler_params=pltpu.CompilerParams(
            dimension_semantics=("arbitrary", "arbitrary"), vmem_limit_bytes=_vmem_limit(est)),
        name="mixer_prompt",
    )(h, *consts)


def _mixer_sample_body(nb, bs, h_ref, s_ref, c0_ref, ng_ref, wa_ref, wr_ref, wgu_ref, bg_ref, wc_ref,
                       gng_ref, cw_ref, wout_ref, o_ref, so_ref, co_ref,
                       q_scr, k_scr, la3_scr, v_scr, g_scr, oc_scr, og_scr):
    i = pl.program_id(0)

    @pl.when(i == 0)
    def _():
        q, k, v, g, la, cb, cc, ch = _mixer_inputs(h_ref[...], ng_ref, wa_ref, wr_ref, wgu_ref, bg_ref, wc_ref)
        q_scr[...] = q.astype(BF16)
        k_scr[...] = k.astype(BF16)
        la3_scr[...] = jnp.concatenate(_split3(la), axis=0)
        v_scr[...] = v
        g_scr[...] = g
        uc = cc * ch
        c0 = c0_ref[...]
        old2 = c0[:, 0:CONV_WIDTH]
        old1 = c0[:, CONV_WIDTH:2 * CONV_WIDTH]
        y = cw_ref[0:1, :] * old2 + cw_ref[1:2, :] * old1 + cw_ref[2:3, :] * uc
        oc_scr[...] = cb * y
        co_ref[...] = jnp.concatenate([old1, uc], axis=1)

    seq_row = lax.broadcasted_iota(jnp.int32, (nb, LANES), 0)

    def seq_step(t, carry):
        b = i * bs + t
        pick = jnp.where(seq_row == b, 1.0, 0.0).astype(BF16)
        pick3 = jnp.concatenate([pick, pick, pick], axis=0)
        la_t = lax.dot_general(la3_scr[...], pick3, TN_DIMS, preferred_element_type=F32)
        k_t = lax.dot_general(k_scr[...], pick, TN_DIMS, preferred_element_type=F32)
        q_t = lax.dot_general(q_scr[...], pick, TN_DIMS, preferred_element_type=F32)
        v_row = v_scr[pl.ds(b, 1), :]
        v_b = jnp.concatenate(
            [jnp.broadcast_to(v_row[:, hd * GLA_DV:(hd + 1) * GLA_DV], (GLA_DK, GLA_DV))
             for hd in range(GLA_HEADS)], axis=0)
        s_new = jnp.exp(la_t) * s_ref[t] + k_t * v_b
        so_ref[t] = s_new
        qs = q_t * s_new
        o_row = jnp.concatenate(
            [jnp.sum(qs[hd * GLA_DK:(hd + 1) * GLA_DK, :], axis=0, keepdims=True)
             for hd in range(GLA_HEADS)], axis=1)
        og_scr[pl.ds(b, 1), :] = o_row
        return carry

    lax.fori_loop(0, bs, seq_step, 0)

    @pl.when(i == pl.num_programs(0) - 1)
    def _():
        o_ref[...] = _mixer_output(h_ref[...], og_scr[...], g_scr[...], oc_scr[...], ng_ref, gng_ref, wout_ref)


def _mixer_sample(h, s0, c0, bs, ng, wa, wr, wgu, bg, wc, gng, cw, wout):
    nb = h.shape[0]
    assert nb == LANES and nb % bs == 0
    consts = [ng, wa, wr, wgu, bg, wc, gng, cw, wout]
    state_spec = pl.BlockSpec((bs, HK, GLA_DV), lambda i: (i, 0, 0))
    est = (sum(_nbytes(c.shape, c.dtype) for c in consts) + 4 * _nbytes((bs, HK, GLA_DV), F32)
           + 12 * _nbytes((nb, D_MODEL), F32) + 2 * _nbytes((nb, 3 * HV), F32))
    return pl.pallas_call(
        functools.partial(_mixer_sample_body, nb, bs),
        grid=(nb // bs,),
        in_specs=[_resident(h.shape), state_spec, _resident(c0.shape)] + [_resident(c.shape) for c in consts],
        out_specs=[pl.BlockSpec((nb, D_MODEL), lambda i: (0, 0)), state_spec,
                   pl.BlockSpec(c0.shape, lambda i: (0, 0))],
        out_shape=[jax.ShapeDtypeStruct((nb, D_MODEL), F32),
                   jax.ShapeDtypeStruct(s0.shape, F32),
                   jax.ShapeDtypeStruct(c0.shape, F32)],
        scratch_shapes=[pltpu.VMEM((nb, HK), BF16), pltpu.VMEM((nb, HK), BF16), pltpu.VMEM((3 * nb, HK), BF16),
                        pltpu.VMEM((nb, HV), F32), pltpu.VMEM((nb, HV), F32), pltpu.VMEM((nb, CONV_WIDTH), F32),
                        pltpu.VMEM((nb, HV), F32)],
        compiler_params=pltpu.CompilerParams(
            dimension_semantics=("arbitrary",), vmem_limit_bytes=_vmem_limit(est)),
        name="mixer_sample",
    )(h, s0, c0, *consts)


TM_PROMPT = 512
TS_FFN = 512
NSUB_FFN = 2
BS_SAMPLE = 16


def kernel(x_prompt, x_sample, state_gla, state_conv, p_prompt, p_sample, norm_g, w_in, w_gate_up, b_gate,
           gla_norm_g, conv_w, w_out, ffn1_gate, ffn1_up, ffn1_down, ffn2_gate, ffn2_up, ffn2_down,
           w_ple_proj, w_ple_gate):
    batch, seq, _ = x_prompt.shape
    nb = x_sample.shape[0]
    assert norm_g.shape[0] == 1 and x_sample.shape[1] == 1

    ng = norm_g[0]
    wi = w_in[0]
    r0 = 2 * HK + 2 * HV
    wa = _gain_into_rows(wi[:, :r0], ng[2])
    wr = _gain_into_rows(jnp.pad(wi[:, r0:r0 + GLA_RANK], ((0, 0), (0, LANES - GLA_RANK))), ng[2])
    wc = _gain_into_rows(wi[:, r0 + GLA_RANK:], ng[2])
    wgu = jnp.pad(w_gate_up[0], ((0, LANES - GLA_RANK), (0, 0))).astype(BF16)
    bg = b_gate[0].reshape(1, HK)
    gng = gla_norm_g[0].reshape(1, GLA_DV)
    cw = conv_w[0]
    wout = w_out[0].astype(BF16)
    f1 = (_gain_into_rows(ffn1_gate[0], ng[0]), _gain_into_rows(ffn1_up[0], ng[0]), ffn1_down[0].astype(BF16))
    f2 = (_gain_into_rows(ffn2_gate[0], ng[4]), _gain_into_rows(ffn2_up[0], ng[4]), ffn2_down[0].astype(BF16))
    wpp = w_ple_proj[0].astype(BF16)
    wpg = _gain_into_rows(w_ple_gate[0], ng[6])
    mix_w = (ng, wa, wr, wgu, bg, wc, gng, cw, wout)

    xp = x_prompt.reshape(batch * seq, D_MODEL)
    pp = p_prompt[0].reshape(batch * seq, PLE_DIM)
    hp = _ffn(xp, ng, *f1, 1, TS_FFN, NSUB_FFN)
    hp, sg_p, sc_p = _mixer_prompt(hp, batch, seq, TM_PROMPT, *mix_w)
    hp = _ffn(hp, ng, *f2, 5, TS_FFN, NSUB_FFN, ple_args=(pp, wpp, wpg))

    xs = x_sample.reshape(nb, D_MODEL)
    ps = p_sample[0].reshape(nb, PLE_DIM)
    s0 = state_gla[0].reshape(nb, HK, GLA_DV)
    c0 = state_conv[0].reshape(nb, (CONV_K - 1) * CONV_WIDTH)
    hs = _ffn(xs, ng, *f1, 1, nb // 2, 2)
    hs, sg_s, sc_s = _mixer_sample(hs, s0, c0, BS_SAMPLE, *mix_w)
    hs = _ffn(hs, ng, *f2, 5, nb // 2, 2, ple_args=(ps, wpp, wpg))

    return (hp.reshape(batch, seq, D_MODEL),
            hs.reshape(nb, 1, D_MODEL),
            sg_p.reshape(1, batch, GLA_HEADS, GLA_DK, GLA_DV),
            sc_p.reshape(1, batch, CONV_K - 1, CONV_WIDTH),
            sg_s.reshape(1, nb, GLA_HEADS, GLA_DK, GLA_DV),
            sc_s.reshape(1, nb, CONV_K - 1, CONV_WIDTH))
```

```python
import functools

import numpy as np
import jax
import jax.numpy as jnp
from jax import lax
from jax.experimental import pallas as pl
from jax.experimental.pallas import tpu as pltpu

F32 = jnp.float32
BF16 = jnp.bfloat16

D_MODEL = 1024
PLE_DIM = 256
GLA_HEADS = 4
GLA_DK = 64
GLA_DV = 128
HK = GLA_HEADS * GLA_DK
HV = GLA_HEADS * GLA_DV
GLA_RANK = 16
GLA_GATE_NORM = 16.0
CONV_WIDTH = 512
CONV_K = 3
D_FF = 2816
EPS = 1e-6

LANES = 128
MXU_COLS = 256
CHUNK = 64
N_LEVELS = 7
VMEM_CAP = 56 * 1024 * 1024

NT_DIMS = (((1,), (1,)), ((), ()))
TN_DIMS = (((0,), (0,)), ((), ()))


def _inv_rms(x):
    return lax.rsqrt(jnp.mean(x * x, axis=-1, keepdims=True) + EPS)


def _rms(x, g):
    return x * _inv_rms(x) * g


def _gain_into_rows(w, g):
    return (g[:, None] * w).astype(BF16)


def _sigmoid(x):
    return 1.0 / (1.0 + jnp.exp(-x))


def _log_sigmoid(x):
    return jnp.minimum(x, 0.0) - jnp.log1p(jnp.exp(-jnp.abs(x)))


def _split3(x):
    hi = x.astype(BF16)
    r1 = x - hi.astype(F32)
    mid = r1.astype(BF16)
    lo = (r1 - mid.astype(F32)).astype(BF16)
    return hi, mid, lo


def _resident(shape):
    nd = len(shape)
    return pl.BlockSpec(shape, lambda *_: (0,) * nd, pipeline_mode=pl.Buffered(1))


def _nbytes(shape, dtype):
    return int(np.prod(shape)) * jnp.dtype(dtype).itemsize


def _vmem_limit(est_bytes):
    return int(min(VMEM_CAP, max(est_bytes * 5 // 4, 16 * 1024 * 1024)))


def _ffn_body(post, ple, ts, nsub, *refs):
    if ple:
        x_ref, ng_ref, wg_ref, wu_ref, wd_ref, p_ref, wpp_ref, wpg_ref, o_ref, h_scr = refs
    else:
        x_ref, ng_ref, wg_ref, wu_ref, wd_ref, o_ref, h_scr = refs

    def gate_up(s, beside):
        x = x_ref[s * ts:(s + 1) * ts, :]
        xb = x.astype(BF16)
        inv = _inv_rms(x)
        for c in range(D_FF // MXU_COLS):
            sl = slice(c * MXU_COLS, (c + 1) * MXU_COLS)
            g = jnp.dot(xb, wg_ref[:, sl], preferred_element_type=F32) * inv
            u = jnp.dot(xb, wu_ref[:, sl], preferred_element_type=F32) * inv
            h_scr[s % 2, :, sl] = (g * _sigmoid(g) * u).astype(BF16)
            if c == 0 and beside is not None:
                beside()

    def down(s):
        return jnp.dot(h_scr[s % 2], wd_ref[...], preferred_element_type=F32)

    def epilogue(s, y):
        rows = slice(s * ts, (s + 1) * ts)
        h = x_ref[rows, :] + 0.5 * _rms(y, ng_ref[post:post + 1, :])
        if ple:
            gate = _sigmoid(jnp.dot(h.astype(BF16), wpg_ref[...], preferred_element_type=F32) * _inv_rms(h))
            proj = jnp.dot(p_ref[rows, :].astype(BF16), wpp_ref[...], preferred_element_type=F32)
            h = h + _rms(proj * gate, ng_ref[7:8, :])
        o_ref[rows, :] = h

    pending = None
    for s in range(nsub):
        gate_up(s, pending)
        y = down(s)
        pending = functools.partial(epilogue, s, y)
    pending()


def _ffn(x, ng, wg, wu, wd, post, ts, nsub, ple_args=None):
    rows = x.shape[0]
    tm = nsub * ts
    assert rows % tm == 0
    ple = ple_args is not None
    row_spec = pl.BlockSpec((tm, D_MODEL), lambda i: (i, 0))
    in_specs = [row_spec, _resident(ng.shape), _resident(wg.shape), _resident(wu.shape), _resident(wd.shape)]
    args = [x, ng, wg, wu, wd]
    est = (4 * _nbytes((tm, D_MODEL), F32) + 3 * _nbytes(wg.shape, BF16)
           + 2 * _nbytes((ts, D_FF), BF16) + _nbytes((ts, D_MODEL), BF16)
           + 6 * _nbytes((ts, MXU_COLS), F32) + 6 * _nbytes((ts, D_MODEL), F32))
    if ple:
        p, wpp, wpg = ple_args
        in_specs += [pl.BlockSpec((tm, PLE_DIM), lambda i: (i, 0)), _resident(wpp.shape), _resident(wpg.shape)]
        args += [p, wpp, wpg]
        est += 2 * _nbytes((tm, PLE_DIM), F32) + _nbytes(wpp.shape, BF16) + _nbytes(wpg.shape, BF16)
        est += 3 * _nbytes((ts, D_MODEL), F32)
    return pl.pallas_call(
        functools.partial(_ffn_body, post, ple, ts, nsub),
        grid=(rows // tm,),
        in_specs=in_specs,
        out_specs=row_spec,
        out_shape=jax.ShapeDtypeStruct((rows, D_MODEL), F32),
        scratch_shapes=[pltpu.VMEM((2, ts, D_FF), BF16)],
        compiler_params=pltpu.CompilerParams(
            dimension_semantics=("arbitrary",), vmem_limit_bytes=_vmem_limit(est)),
        name="ffn_ple" if ple else "ffn",
    )(*args)


def _run_together(*stages):
    results = [None] * len(stages)
    live = list(enumerate(stages))
    while live:
        still = []
        for idx, gen in live:
            try:
                next(gen)
                still.append((idx, gen))
            except StopIteration as stop:
                results[idx] = stop.value
        live = still
    return results


def _finish(stages):
    return _run_together(stages)[0]


def _mixer_inputs(h, ng_ref, wa_ref, wr_ref, wgu_ref, bg_ref, wc_ref):
    hb = h.astype(BF16)
    inv = _inv_rms(h)
    yield
    r = jnp.dot(hb, wr_ref[...], preferred_element_type=F32) * inv
    logit = jnp.dot(r.astype(BF16), wgu_ref[...], preferred_element_type=F32) + bg_ref[...]
    la = _log_sigmoid(logit) * (1.0 / GLA_GATE_NORM)
    yield
    cols = []
    for w_ref in (wa_ref, wc_ref):
        for c in range(0, w_ref.shape[1], MXU_COLS):
            cols.append(jnp.dot(hb, w_ref[:, c:c + MXU_COLS], preferred_element_type=F32) * inv)
            yield
    z = jnp.concatenate(cols, axis=1)
    q = z[:, 0:HK] * (GLA_DK ** -0.5)
    k = z[:, HK:2 * HK]
    v = z[:, 2 * HK:2 * HK + HV]
    g = z[:, 2 * HK + HV:2 * HK + 2 * HV]
    zc = z[:, 2 * HK + 2 * HV:]
    cb = zc[:, 0:CONV_WIDTH]
    cc = zc[:, CONV_WIDTH:2 * CONV_WIDTH]
    ch = zc[:, 2 * CONV_WIDTH:3 * CONV_WIDTH]
    return q, k, v, g, la, cb, cc, ch


def _short_conv(cb, uc, tail, cw_ref):
    row = lax.broadcasted_iota(jnp.int32, uc.shape, 0)
    c1 = tail[7:8, :]
    c2 = tail[6:7, :]
    prev1 = jnp.where(row == 0, c1, pltpu.roll(uc, 1, axis=0))
    prev2 = jnp.where(row == 0, c2, jnp.where(row == 1, c1, pltpu.roll(uc, 2, axis=0)))
    return cb * (cw_ref[0:1, :] * prev2 + cw_ref[1:2, :] * prev1 + cw_ref[2:3, :] * uc)


def _mixer_output(h, o, g, oc, ng_ref, gng_ref, wout_ref):
    heads = []
    for hd in range(GLA_HEADS):
        sl = slice(hd * GLA_DV, (hd + 1) * GLA_DV)
        heads.append(_rms(o[:, sl], gng_ref[...]))
    on = jnp.concatenate(heads, axis=1) * (g * _sigmoid(g))
    cat = jnp.concatenate([on.astype(BF16), oc.astype(BF16)], axis=1)
    yield
    cols = []
    for c in range(0, wout_ref.shape[1], MXU_COLS):
        cols.append(jnp.dot(cat, wout_ref[:, c:c + MXU_COLS], preferred_element_type=F32))
        yield
    return h + _rms(jnp.concatenate(cols, axis=1), ng_ref[3:4, :])


def _chunk_constants():
    t = np.arange(CHUNK)[:, None]
    s = np.arange(CHUNK)[None, :]
    blocks = []
    for j in range(1, N_LEVELS - 1):
        half = 1 << j
        mid = (t // (2 * half)) * (2 * half) + half
        blocks.append((s <= t).astype(np.float32) - (s < mid).astype(np.float32))
    blocks.append((s <= t).astype(np.float32))
    m = np.concatenate(blocks, axis=0)
    mst = np.concatenate([m, m, m], axis=1)
    x = np.bitwise_xor(t, s)
    top = np.floor(np.log2(np.maximum(x, 1))).astype(np.int32) + 1
    lvl = np.where(s > t, -1, np.where(s == t, 0, top)).astype(np.int32)
    return jnp.asarray(mst, BF16), jnp.asarray(np.tile(lvl, (1, GLA_HEADS)), jnp.int32)


def _block_diag_rows(x, lane_head):
    zero = jnp.zeros_like(x)
    return jnp.concatenate([jnp.where(lane_head == hd, x, zero) for hd in range(GLA_HEADS)], axis=0)


def _gla_tile(q, k, v, la, st, mst, lvl, lane_head_k, lane_head_v, lane_head_s):
    c = CHUNK
    chunks = [slice(r, r + c) for r in range(0, q.shape[0], c)]
    half = len(chunks) // 2
    cat = lambda xs: jnp.concatenate(xs, axis=0)

    hi, mid, lo = _split3(la)
    e = []
    for n, r in enumerate(chunks):
        e.append(jnp.dot(mst, cat([hi[r, :], mid[r, :], lo[r, :]]), preferred_element_type=F32))
        if n == half - 1:
            yield
    yield
    bc = cat([en[5 * c:6 * c, :] for en in e])
    b_last = [en[6 * c - 1:6 * c, :] for en in e]
    b_last_rows = cat([jnp.broadcast_to(b, (c, HK)) for b in b_last])

    vb = v.astype(BF16)
    k_dec = (k * jnp.exp(b_last_rows - bc)).astype(BF16)
    upd_all = []
    for n, r in enumerate(chunks):
        upd_all.append(lax.dot_general(vb[r, :], k_dec[r, :], TN_DIMS, preferred_element_type=F32))
        if n == half - 1:
            yield
    yield
    states = []
    for n in range(len(chunks)):
        states.append(st)
        upd = jnp.zeros_like(st)
        for hd in range(GLA_HEADS):
            upd = jnp.where(lane_head_s == hd, upd_all[n][hd * GLA_DV:(hd + 1) * GLA_DV, :], upd)
        st = st * jnp.exp(b_last[n]) + upd

    qb = q.astype(BF16)
    q1 = (q * jnp.exp(la)).astype(BF16)
    kb = k.astype(BF16)
    att = []
    for n, r in enumerate(chunks):
        x = lax.dot_general(cat([qb[r, :], q1[r, :]]), _block_diag_rows(kb[r, :], lane_head_k), NT_DIMS,
                            preferred_element_type=F32)
        att.append(jnp.where(lvl == 1, x[c:, :], jnp.where(lvl == 0, x[:c, :], 0.0)))
        if n == half - 1:
            yield
    yield
    for lv in range(2, N_LEVELS):
        el = cat([en[(lv - 2) * c:(lv - 1) * c, :] for en in e])
        ql = (q * jnp.exp(jnp.minimum(el, 0.0))).astype(BF16)
        kl = (k * jnp.exp(jnp.minimum(-el, 0.0))).astype(BF16)
        for n, r in enumerate(chunks):
            x = lax.dot_general(ql[r, :], _block_diag_rows(kl[r, :], lane_head_k), NT_DIMS,
                                preferred_element_type=F32)
            att[n] = jnp.where(lvl == lv, x, att[n])
        yield

    q_in = (q * jnp.exp(bc)).astype(BF16)
    o = []
    for n, r in enumerate(chunks):
        o.append(lax.dot_general(q_in[r, :], _block_diag_rows(states[n], lane_head_s).astype(BF16), NT_DIMS,
                                 preferred_element_type=F32)
                 + jnp.dot(att[n].astype(BF16), _block_diag_rows(vb[r, :], lane_head_v),
                           preferred_element_type=F32))
        if n == half - 1:
            yield
    return cat(o), st


def _mixer_prompt_body(tm, h_ref, ng_ref, wa_ref, wr_ref, wgu_ref, bg_ref, wc_ref, gng_ref, cw_ref,
                       wout_ref, mst_ref, lvl_ref, o_ref, sg_ref, sc_ref, st_scr, carry_scr):
    j = pl.program_id(1)

    @pl.when(j == 0)
    def _():
        st_scr[...] = jnp.zeros_like(st_scr)
        carry_scr[...] = jnp.zeros_like(carry_scr)

    lane_head_k = lax.broadcasted_iota(jnp.int32, (CHUNK, HK), 1) // GLA_DK
    lane_head_v = lax.broadcasted_iota(jnp.int32, (CHUNK, HV), 1) // GLA_DV
    lane_head_s = lax.broadcasted_iota(jnp.int32, (GLA_DV, HK), 1) // GLA_DK

    def inputs(h):
        return _mixer_inputs(h, ng_ref, wa_ref, wr_ref, wgu_ref, bg_ref, wc_ref)

    def gla(q, k, v, la, st):
        return _gla_tile(q, k, v, la, st, mst_ref[...], lvl_ref[...], lane_head_k, lane_head_v, lane_head_s)

    def output(h, og, g, cb, uc, tail):
        return _mixer_output(h, og, g, _short_conv(cb, uc, tail, cw_ref), ng_ref, gng_ref, wout_ref)

    h_a = h_ref[0:tm, :]
    h_b = h_ref[tm:2 * tm, :]
    q, k, v, g_a, la, cb_a, cc, ch = _finish(inputs(h_a))
    uc_a = cc * ch
    (og_a, st), (q, k, v, g_b, la, cb_b, cc, ch) = _run_together(gla(q, k, v, la, st_scr[...]), inputs(h_b))
    uc_b = cc * ch
    out_a, (og_b, st) = _run_together(output(h_a, og_a, g_a, cb_a, uc_a, carry_scr[...]), gla(q, k, v, la, st))
    o_ref[0:tm, :] = out_a
    o_ref[tm:2 * tm, :] = _finish(output(h_b, og_b, g_b, cb_b, uc_b, uc_a[tm - 8:tm, :]))
    st_scr[...] = st
    carry_scr[...] = uc_b[tm - 8:tm, :]

    @pl.when(j == pl.num_programs(1) - 1)
    def _():
        sg_ref[0] = st.T
        sc_ref[0] = carry_scr[6:8, :]


def _mixer_prompt(h, batch, seq, tm, ng, wa, wr, wgu, bg, wc, gng, cw, wout):
    assert seq % (2 * tm) == 0 and tm % (2 * CHUNK) == 0
    nt = seq // (2 * tm)
    mst, lvl = _chunk_constants()
    consts = [ng, wa, wr, wgu, bg, wc, gng, cw, wout, mst, lvl]
    row_spec = pl.BlockSpec((2 * tm, D_MODEL), lambda b, j: (b * nt + j, 0))
    est = (8 * _nbytes((tm, D_MODEL), F32) + sum(_nbytes(c.shape, c.dtype) for c in consts)
           + 4 * _nbytes((tm, 3 * HV), F32) + 2 * _nbytes((tm, 3 * HK + 2 * HV), F32)
           + 8 * _nbytes((tm, D_MODEL), F32))
    return pl.pallas_call(
        functools.partial(_mixer_prompt_body, tm),
        grid=(batch, nt),
        in_specs=[row_spec] + [_resident(c.shape) for c in consts],
        out_specs=[row_spec,
                   pl.BlockSpec((1, HK, GLA_DV), lambda b, j: (b, 0, 0)),
                   pl.BlockSpec((1, CONV_K - 1, CONV_WIDTH), lambda b, j: (b, 0, 0))],
        out_shape=[jax.ShapeDtypeStruct((batch * seq, D_MODEL), F32),
                   jax.ShapeDtypeStruct((batch, HK, GLA_DV), F32),
                   jax.ShapeDtypeStruct((batch, CONV_K - 1, CONV_WIDTH), F32)],
        scratch_shapes=[pltpu.VMEM((GLA_DV, HK), F32), pltpu.VMEM((8, CONV_WIDTH), F32)],
        compiler_params=pltpu.CompilerParams(
            dimension_semantics=("arbitrary", "arbitrary"), vmem_limit_bytes=_vmem_limit(est)),
        name="mixer_prompt",
    )(h, *consts)


def _mixer_sample_body(nb, bs, h_ref, s_ref, c0_ref, ng_ref, wa_ref, wr_ref, wgu_ref, bg_ref, wc_ref,
                       gng_ref, cw_ref, wout_ref, o_ref, so_ref, co_ref,
                       q_scr, k_scr, la3_scr, v_scr, g_scr, oc_scr, og_scr):
    i = pl.program_id(0)

    @pl.when(i == 0)
    def _():
        q, k, v, g, la, cb, cc, ch = _finish(
            _mixer_inputs(h_ref[...], ng_ref, wa_ref, wr_ref, wgu_ref, bg_ref, wc_ref))
        q_scr[...] = q.astype(BF16)
        k_scr[...] = k.astype(BF16)
        la3_scr[...] = jnp.concatenate(_split3(la), axis=0)
        v_scr[...] = v
        g_scr[...] = g
        uc = cc * ch
        c0 = c0_ref[...]
        old2 = c0[:, 0:CONV_WIDTH]
        old1 = c0[:, CONV_WIDTH:2 * CONV_WIDTH]
        y = cw_ref[0:1, :] * old2 + cw_ref[1:2, :] * old1 + cw_ref[2:3, :] * uc
        oc_scr[...] = cb * y
        co_ref[...] = jnp.concatenate([old1, uc], axis=1)

    seq_row = lax.broadcasted_iota(jnp.int32, (nb, LANES), 0)

    def seq_step(t, carry):
        b = i * bs + t
        pick = jnp.where(seq_row == b, 1.0, 0.0).astype(BF16)
        pick3 = jnp.concatenate([pick, pick, pick], axis=0)
        la_t = lax.dot_general(la3_scr[...], pick3, TN_DIMS, preferred_element_type=F32)
        k_t = lax.dot_general(k_scr[...], pick, TN_DIMS, preferred_element_type=F32)
        q_t = lax.dot_general(q_scr[...], pick, TN_DIMS, preferred_element_type=F32)
        v_row = v_scr[pl.ds(b, 1), :]
        v_b = jnp.concatenate(
            [jnp.broadcast_to(v_row[:, hd * GLA_DV:(hd + 1) * GLA_DV], (GLA_DK, GLA_DV))
             for hd in range(GLA_HEADS)], axis=0)
        s_new = jnp.exp(la_t) * s_ref[t] + k_t * v_b
        so_ref[t] = s_new
        qs = q_t * s_new
        o_row = jnp.concatenate(
            [jnp.sum(qs[hd * GLA_DK:(hd + 1) * GLA_DK, :], axis=0, keepdims=True)
             for hd in range(GLA_HEADS)], axis=1)
        og_scr[pl.ds(b, 1), :] = o_row
        return carry

    lax.fori_loop(0, bs, seq_step, 0)

    @pl.when(i == pl.num_programs(0) - 1)
    def _():
        o_ref[...] = _finish(
            _mixer_output(h_ref[...], og_scr[...], g_scr[...], oc_scr[...], ng_ref, gng_ref, wout_ref))


def _mixer_sample(h, s0, c0, bs, ng, wa, wr, wgu, bg, wc, gng, cw, wout):
    nb = h.shape[0]
    assert nb == LANES and nb % bs == 0
    consts = [ng, wa, wr, wgu, bg, wc, gng, cw, wout]
    state_spec = pl.BlockSpec((bs, HK, GLA_DV), lambda i: (i, 0, 0))
    est = (sum(_nbytes(c.shape, c.dtype) for c in consts) + 4 * _nbytes((bs, HK, GLA_DV), F32)
           + 12 * _nbytes((nb, D_MODEL), F32) + 2 * _nbytes((nb, 3 * HV), F32))
    return pl.pallas_call(
        functools.partial(_mixer_sample_body, nb, bs),
        grid=(nb // bs,),
        in_specs=[_resident(h.shape), state_spec, _resident(c0.shape)] + [_resident(c.shape) for c in consts],
        out_specs=[pl.BlockSpec((nb, D_MODEL), lambda i: (0, 0)), state_spec,
                   pl.BlockSpec(c0.shape, lambda i: (0, 0))],
        out_shape=[jax.ShapeDtypeStruct((nb, D_MODEL), F32),
                   jax.ShapeDtypeStruct(s0.shape, F32),
                   jax.ShapeDtypeStruct(c0.shape, F32)],
        scratch_shapes=[pltpu.VMEM((nb, HK), BF16), pltpu.VMEM((nb, HK), BF16), pltpu.VMEM((3 * nb, HK), BF16),
                        pltpu.VMEM((nb, HV), F32), pltpu.VMEM((nb, HV), F32), pltpu.VMEM((nb, CONV_WIDTH), F32),
                        pltpu.VMEM((nb, HV), F32)],
        compiler_params=pltpu.CompilerParams(
            dimension_semantics=("arbitrary",), vmem_limit_bytes=_vmem_limit(est)),
        name="mixer_sample",
    )(h, s0, c0, *consts)


TM_PROMPT = 512
TS_FFN = 512
NSUB_FFN = 2
BS_SAMPLE = 16


def kernel(x_prompt, x_sample, state_gla, state_conv, p_prompt, p_sample, norm_g, w_in, w_gate_up, b_gate,
           gla_norm_g, conv_w, w_out, ffn1_gate, ffn1_up, ffn1_down, ffn2_gate, ffn2_up, ffn2_down,
           w_ple_proj, w_ple_gate):
    batch, seq, _ = x_prompt.shape
    nb = x_sample.shape[0]
    assert norm_g.shape[0] == 1 and x_sample.shape[1] == 1

    ng = norm_g[0]
    wi = w_in[0]
    r0 = 2 * HK + 2 * HV
    wa = _gain_into_rows(wi[:, :r0], ng[2])
    wr = _gain_into_rows(jnp.pad(wi[:, r0:r0 + GLA_RANK], ((0, 0), (0, LANES - GLA_RANK))), ng[2])
    wc = _gain_into_rows(wi[:, r0 + GLA_RANK:], ng[2])
    wgu = jnp.pad(w_gate_up[0], ((0, LANES - GLA_RANK), (0, 0))).astype(BF16)
    bg = b_gate[0].reshape(1, HK)
    gng = gla_norm_g[0].reshape(1, GLA_DV)
    cw = conv_w[0]
    wout = w_out[0].astype(BF16)
    f1 = (_gain_into_rows(ffn1_gate[0], ng[0]), _gain_into_rows(ffn1_up[0], ng[0]), ffn1_down[0].astype(BF16))
    f2 = (_gain_into_rows(ffn2_gate[0], ng[4]), _gain_into_rows(ffn2_up[0], ng[4]), ffn2_down[0].astype(BF16))
    wpp = w_ple_proj[0].astype(BF16)
    wpg = _gain_into_rows(w_ple_gate[0], ng[6])
    mix_w = (ng, wa, wr, wgu, bg, wc, gng, cw, wout)

    xp = x_prompt.reshape(batch * seq, D_MODEL)
    pp = p_prompt[0].reshape(batch * seq, PLE_DIM)
    hp = _ffn(xp, ng, *f1, 1, TS_FFN, NSUB_FFN)
    hp, sg_p, sc_p = _mixer_prompt(hp, batch, seq, TM_PROMPT, *mix_w)
    hp = _ffn(hp, ng, *f2, 5, TS_FFN, NSUB_FFN, ple_args=(pp, wpp, wpg))

    xs = x_sample.reshape(nb, D_MODEL)
    ps = p_sample[0].reshape(nb, PLE_DIM)
    s0 = state_gla[0].reshape(nb, HK, GLA_DV)
    c0 = state_conv[0].reshape(nb, (CONV_K - 1) * CONV_WIDTH)
    hs = _ffn(xs, ng, *f1, 1, nb, 1)
    hs, sg_s, sc_s = _mixer_sample(hs, s0, c0, BS_SAMPLE, *mix_w)
    hs = _ffn(hs, ng, *f2, 5, nb, 1, ple_args=(ps, wpp, wpg))

    return (hp.reshape(batch, seq, D_MODEL),
            hs.reshape(nb, 1, D_MODEL),
            sg_p.reshape(1, batch, GLA_HEADS, GLA_DK, GLA_DV),
            sc_p.reshape(1, batch, CONV_K - 1, CONV_WIDTH),
            sg_s.reshape(1, nb, GLA_HEADS, GLA_DK, GLA_DV),
            sc_s.reshape(1, nb, CONV_K - 1, CONV_WIDTH))
```

```python
import functools

import numpy as np
import jax
import jax.numpy as jnp
from jax import lax
from jax.experimental import pallas as pl
from jax.experimental.pallas import tpu as pltpu

F32 = jnp.float32
BF16 = jnp.bfloat16

D_MODEL = 1024
PLE_DIM = 256
GLA_HEADS = 4
GLA_DK = 64
GLA_DV = 128
HK = GLA_HEADS * GLA_DK
HV = GLA_HEADS * GLA_DV
GLA_RANK = 16
GLA_GATE_NORM = 16.0
CONV_WIDTH = 512
CONV_K = 3
D_FF = 2816
EPS = 1e-6

LANES = 128
MXU_COLS = 256
CHUNK = 64
N_LEVELS = 7
VMEM_CAP = 56 * 1024 * 1024

NT_DIMS = (((1,), (1,)), ((), ()))
TN_DIMS = (((0,), (0,)), ((), ()))


def _inv_rms(x):
    return lax.rsqrt(jnp.mean(x * x, axis=-1, keepdims=True) + EPS)


def _rms(x, g):
    return x * _inv_rms(x) * g


def _gain_into_rows(w, g):
    return (g[:, None] * w).astype(BF16)


def _sigmoid(x):
    return 1.0 / (1.0 + jnp.exp(-x))


def _log_sigmoid(x):
    return jnp.minimum(x, 0.0) - jnp.log1p(jnp.exp(-jnp.abs(x)))


def _split3(x):
    hi = x.astype(BF16)
    r1 = x - hi.astype(F32)
    mid = r1.astype(BF16)
    lo = (r1 - mid.astype(F32)).astype(BF16)
    return hi, mid, lo


def _resident(shape):
    nd = len(shape)
    return pl.BlockSpec(shape, lambda *_: (0,) * nd, pipeline_mode=pl.Buffered(1))


def _nbytes(shape, dtype):
    return int(np.prod(shape)) * jnp.dtype(dtype).itemsize


def _vmem_limit(est_bytes):
    return int(min(VMEM_CAP, max(est_bytes * 5 // 4, 16 * 1024 * 1024)))


class _PrepJob:
    def __init__(self, w, gain=None, pieces=None):
        self.w = w
        self.gain = gain
        self.pieces = pieces or [(0, w.shape[1], w.shape[1])]


def _prep_rows(job, g_ref, w_ref, out_refs):
    w = w_ref[...]
    if job.gain is not None:
        w = w * g_ref[:, job.gain:job.gain + 1]
    for (c0, keep, width), out_ref in zip(job.pieces, out_refs):
        piece = w[:, c0:c0 + width]
        if keep < width:
            lane = lax.broadcasted_iota(jnp.int32, piece.shape, 1)
            piece = jnp.where(lane < keep, piece, 0.0)
        out_ref[...] = piece.astype(BF16)


def _ffn_body(post, ple, ts, nsub, jobs, *refs):
    refs = list(refs)
    x_ref, ng_ref, wg_ref, wu_ref, wd_ref = refs[:5]
    del refs[:5]
    if ple:
        p_ref, wpp_ref, wpg_ref = refs[:3]
        del refs[:3]
    if jobs:
        g_ref = refs.pop(0)
        job_in = refs[:len(jobs)]
        del refs[:len(jobs)]
    o_ref = refs.pop(0)
    h_scr = refs.pop()
    for job, w_ref in zip(jobs, job_in if jobs else ()):
        outs = refs[:len(job.pieces)]
        del refs[:len(job.pieces)]
        _prep_rows(job, g_ref, w_ref, outs)

    def gate_up(s, beside):
        x = x_ref[s * ts:(s + 1) * ts, :]
        xb = x.astype(BF16)
        inv = _inv_rms(x)
        for c in range(D_FF // MXU_COLS):
            sl = slice(c * MXU_COLS, (c + 1) * MXU_COLS)
            g = jnp.dot(xb, wg_ref[:, sl], preferred_element_type=F32) * inv
            u = jnp.dot(xb, wu_ref[:, sl], preferred_element_type=F32) * inv
            h_scr[s % 2, :, sl] = (g * _sigmoid(g) * u).astype(BF16)
            if c == 0 and beside is not None:
                beside()

    def down(s):
        return jnp.dot(h_scr[s % 2], wd_ref[...], preferred_element_type=F32)

    def epilogue(s, y):
        rows = slice(s * ts, (s + 1) * ts)
        h = x_ref[rows, :] + 0.5 * _rms(y, ng_ref[post:post + 1, :])
        if ple:
            gate = _sigmoid(jnp.dot(h.astype(BF16), wpg_ref[...], preferred_element_type=F32) * _inv_rms(h))
            proj = jnp.dot(p_ref[rows, :].astype(BF16), wpp_ref[...], preferred_element_type=F32)
            h = h + _rms(proj * gate, ng_ref[7:8, :])
        o_ref[rows, :] = h

    pending = None
    for s in range(nsub):
        gate_up(s, pending)
        y = down(s)
        pending = functools.partial(epilogue, s, y)
    pending()


def _ffn(x, ng, wg, wu, wd, post, ts, nsub, ple_args=None, prep=None):
    rows = x.shape[0]
    tm = nsub * ts
    assert rows % tm == 0
    steps = rows // tm
    ple = ple_args is not None
    row_spec = pl.BlockSpec((tm, D_MODEL), lambda i: (i, 0))
    in_specs = [row_spec, _resident(ng.shape), _resident(wg.shape), _resident(wu.shape), _resident(wd.shape)]
    args = [x, ng, wg, wu, wd]
    est = (4 * _nbytes((tm, D_MODEL), F32) + 3 * _nbytes(wg.shape, BF16)
           + 2 * _nbytes((ts, D_FF), BF16) + _nbytes((ts, D_MODEL), BF16)
           + 6 * _nbytes((ts, MXU_COLS), F32) + 6 * _nbytes((ts, D_MODEL), F32))
    if ple:
        p, wpp, wpg = ple_args
        in_specs += [pl.BlockSpec((tm, PLE_DIM), lambda i: (i, 0)), _resident(wpp.shape), _resident(wpg.shape)]
        args += [p, wpp, wpg]
        est += 2 * _nbytes((tm, PLE_DIM), F32) + _nbytes(wpp.shape, BF16) + _nbytes(wpg.shape, BF16)
        est += 3 * _nbytes((ts, D_MODEL), F32)
    jobs = []
    out_specs = [row_spec]
    out_shape = [jax.ShapeDtypeStruct((rows, D_MODEL), F32)]
    if prep is not None:
        gains, jobs = prep
        in_specs.append(pl.BlockSpec((gains.shape[0] // steps, gains.shape[1]), lambda i: (i, 0)))
        args.append(gains)
        for job in jobs:
            r, c = job.w.shape
            rb = r // steps
            assert r % steps == 0 and rb % 16 == 0
            assert job.gain is None or r == gains.shape[0]
            in_specs.append(pl.BlockSpec((rb, c), lambda i: (i, 0)))
            args.append(job.w)
            est += 2 * _nbytes((rb, c), F32)
            for _, _, width in job.pieces:
                out_specs.append(pl.BlockSpec((rb, width), lambda i: (i, 0)))
                out_shape.append(jax.ShapeDtypeStruct((r, width), BF16))
                est += 2 * _nbytes((rb, width), BF16)
    res = pl.pallas_call(
        functools.partial(_ffn_body, post, ple, ts, nsub, jobs),
        grid=(steps,),
        in_specs=in_specs,
        out_specs=out_specs,
        out_shape=out_shape,
        scratch_shapes=[pltpu.VMEM((2, ts, D_FF), BF16)],
        compiler_params=pltpu.CompilerParams(
            dimension_semantics=("arbitrary",), vmem_limit_bytes=_vmem_limit(est)),
        name="ffn_ple" if ple else "ffn",
    )(*args)
    return res if prep is not None else res[0]


def _run_together(*stages):
    results = [None] * len(stages)
    live = list(enumerate(stages))
    while live:
        still = []
        for idx, gen in live:
            try:
                next(gen)
                still.append((idx, gen))
            except StopIteration as stop:
                results[idx] = stop.value
        live = still
    return results


def _finish(stages):
    return _run_together(stages)[0]


def _mixer_inputs(h, ng_ref, wa_ref, wr_ref, wgu_ref, bg_ref, wc_ref):
    hb = h.astype(BF16)
    inv = _inv_rms(h)
    yield
    r = jnp.dot(hb, wr_ref[...], preferred_element_type=F32) * inv
    logit = jnp.dot(r.astype(BF16), wgu_ref[...], preferred_element_type=F32) + bg_ref[...]
    la = _log_sigmoid(logit) * (1.0 / GLA_GATE_NORM)
    yield
    cols = []
    for w_ref in (wa_ref, wc_ref):
        for c in range(0, w_ref.shape[1], MXU_COLS):
            cols.append(jnp.dot(hb, w_ref[:, c:c + MXU_COLS], preferred_element_type=F32) * inv)
            yield
    z = jnp.concatenate(cols, axis=1)
    q = z[:, 0:HK] * (GLA_DK ** -0.5)
    k = z[:, HK:2 * HK]
    v = z[:, 2 * HK:2 * HK + HV]
    g = z[:, 2 * HK + HV:2 * HK + 2 * HV]
    zc = z[:, 2 * HK + 2 * HV:]
    cb = zc[:, 0:CONV_WIDTH]
    cc = zc[:, CONV_WIDTH:2 * CONV_WIDTH]
    ch = zc[:, 2 * CONV_WIDTH:3 * CONV_WIDTH]
    return q, k, v, g, la, cb, cc, ch


def _short_conv(cb, uc, tail, cw_ref):
    row = lax.broadcasted_iota(jnp.int32, uc.shape, 0)
    c1 = tail[7:8, :]
    c2 = tail[6:7, :]
    prev1 = jnp.where(row == 0, c1, pltpu.roll(uc, 1, axis=0))
    prev2 = jnp.where(row == 0, c2, jnp.where(row == 1, c1, pltpu.roll(uc, 2, axis=0)))
    return cb * (cw_ref[0:1, :] * prev2 + cw_ref[1:2, :] * prev1 + cw_ref[2:3, :] * uc)


def _mixer_output(h, o, g, oc, ng_ref, gng_ref, wout_ref):
    heads = []
    for hd in range(GLA_HEADS):
        sl = slice(hd * GLA_DV, (hd + 1) * GLA_DV)
        heads.append(_rms(o[:, sl], gng_ref[...]))
    on = jnp.concatenate(heads, axis=1) * (g * _sigmoid(g))
    cat = jnp.concatenate([on.astype(BF16), oc.astype(BF16)], axis=1)
    yield
    cols = []
    for c in range(0, wout_ref.shape[1], MXU_COLS):
        cols.append(jnp.dot(cat, wout_ref[:, c:c + MXU_COLS], preferred_element_type=F32))
        yield
    return h + _rms(jnp.concatenate(cols, axis=1), ng_ref[3:4, :])


def _chunk_constants():
    t = np.arange(CHUNK)[:, None]
    s = np.arange(CHUNK)[None, :]
    blocks = []
    for j in range(1, N_LEVELS - 1):
        half = 1 << j
        mid = (t // (2 * half)) * (2 * half) + half
        blocks.append((s <= t).astype(np.float32) - (s < mid).astype(np.float32))
    blocks.append((s <= t).astype(np.float32))
    m = np.concatenate(blocks, axis=0)
    mst = np.concatenate([m, m, m], axis=1)
    x = np.bitwise_xor(t, s)
    top = np.floor(np.log2(np.maximum(x, 1))).astype(np.int32) + 1
    lvl = np.where(s > t, -1, np.where(s == t, 0, top)).astype(np.int32)
    return jnp.asarray(mst, BF16), jnp.asarray(np.tile(lvl, (1, GLA_HEADS)), jnp.int32)


def _block_diag_rows(x, lane_head):
    zero = jnp.zeros_like(x)
    return jnp.concatenate([jnp.where(lane_head == hd, x, zero) for hd in range(GLA_HEADS)], axis=0)


def _gla_tile(q, k, v, la, st, mst, lvl, lane_head_k, lane_head_v, lane_head_s):
    c = CHUNK
    chunks = [slice(r, r + c) for r in range(0, q.shape[0], c)]
    half = len(chunks) // 2
    cat = lambda xs: jnp.concatenate(xs, axis=0)

    hi, mid, lo = _split3(la)
    e = []
    for n, r in enumerate(chunks):
        e.append(jnp.dot(mst, cat([hi[r, :], mid[r, :], lo[r, :]]), preferred_element_type=F32))
        if n == half - 1:
            yield
    yield
    bc = cat([en[5 * c:6 * c, :] for en in e])
    b_last = [en[6 * c - 1:6 * c, :] for en in e]
    b_last_rows = cat([jnp.broadcast_to(b, (c, HK)) for b in b_last])

    vb = v.astype(BF16)
    k_dec = (k * jnp.exp(b_last_rows - bc)).astype(BF16)
    upd_all = []
    for n, r in enumerate(chunks):
        upd_all.append(lax.dot_general(vb[r, :], k_dec[r, :], TN_DIMS, preferred_element_type=F32))
        if n == half - 1:
            yield
    yield
    states = []
    for n in range(len(chunks)):
        states.append(st)
        upd = jnp.zeros_like(st)
        for hd in range(GLA_HEADS):
            upd = jnp.where(lane_head_s == hd, upd_all[n][hd * GLA_DV:(hd + 1) * GLA_DV, :], upd)
        st = st * jnp.exp(b_last[n]) + upd

    qb = q.astype(BF16)
    q1 = (q * jnp.exp(la)).astype(BF16)
    kb = k.astype(BF16)
    att = []
    for n, r in enumerate(chunks):
        x = lax.dot_general(cat([qb[r, :], q1[r, :]]), _block_diag_rows(kb[r, :], lane_head_k), NT_DIMS,
                            preferred_element_type=F32)
        att.append(jnp.where(lvl == 1, x[c:, :], jnp.where(lvl == 0, x[:c, :], 0.0)))
        if n == half - 1:
            yield
    yield
    for lv in range(2, N_LEVELS):
        el = cat([en[(lv - 2) * c:(lv - 1) * c, :] for en in e])
        ql = (q * jnp.exp(jnp.minimum(el, 0.0))).astype(BF16)
        kl = (k * jnp.exp(jnp.minimum(-el, 0.0))).astype(BF16)
        for n, r in enumerate(chunks):
            x = lax.dot_general(ql[r, :], _block_diag_rows(kl[r, :], lane_head_k), NT_DIMS,
                                preferred_element_type=F32)
            att[n] = jnp.where(lvl == lv, x, att[n])
        yield

    q_in = (q * jnp.exp(bc)).astype(BF16)
    o = []
    for n, r in enumerate(chunks):
        o.append(lax.dot_general(q_in[r, :], _block_diag_rows(states[n], lane_head_s).astype(BF16), NT_DIMS,
                                 preferred_element_type=F32)
                 + jnp.dot(att[n].astype(BF16), _block_diag_rows(vb[r, :], lane_head_v),
                           preferred_element_type=F32))
        if n == half - 1:
            yield
    return cat(o), st


def _mixer_prompt_body(tm, h_ref, ng_ref, wa_ref, wr_ref, wgu_ref, bg_ref, wc_ref, gng_ref, cw_ref,
                       wout_ref, mst_ref, lvl_ref, o_ref, sg_ref, sc_ref, st_scr, carry_scr):
    j = pl.program_id(1)

    @pl.when(j == 0)
    def _():
        st_scr[...] = jnp.zeros_like(st_scr)
        carry_scr[...] = jnp.zeros_like(carry_scr)

    lane_head_k = lax.broadcasted_iota(jnp.int32, (CHUNK, HK), 1) // GLA_DK
    lane_head_v = lax.broadcasted_iota(jnp.int32, (CHUNK, HV), 1) // GLA_DV
    lane_head_s = lax.broadcasted_iota(jnp.int32, (GLA_DV, HK), 1) // GLA_DK

    def inputs(h):
        return _mixer_inputs(h, ng_ref, wa_ref, wr_ref, wgu_ref, bg_ref, wc_ref)

    def gla(q, k, v, la, st):
        return _gla_tile(q, k, v, la, st, mst_ref[...], lvl_ref[...], lane_head_k, lane_head_v, lane_head_s)

    def output(h, og, g, cb, uc, tail):
        return _mixer_output(h, og, g, _short_conv(cb, uc, tail, cw_ref), ng_ref, gng_ref, wout_ref)

    h_a = h_ref[0:tm, :]
    h_b = h_ref[tm:2 * tm, :]
    q, k, v, g_a, la, cb_a, cc, ch = _finish(inputs(h_a))
    uc_a = cc * ch
    (og_a, st), (q, k, v, g_b, la, cb_b, cc, ch) = _run_together(gla(q, k, v, la, st_scr[...]), inputs(h_b))
    uc_b = cc * ch
    out_a, (og_b, st) = _run_together(output(h_a, og_a, g_a, cb_a, uc_a, carry_scr[...]), gla(q, k, v, la, st))
    o_ref[0:tm, :] = out_a
    o_ref[tm:2 * tm, :] = _finish(output(h_b, og_b, g_b, cb_b, uc_b, uc_a[tm - 8:tm, :]))
    st_scr[...] = st
    carry_scr[...] = uc_b[tm - 8:tm, :]

    @pl.when(j == pl.num_programs(1) - 1)
    def _():
        sg_ref[0] = st.T
        sc_ref[0] = carry_scr[6:8, :]


def _mixer_prompt(h, batch, seq, tm, ng, wa, wr, wgu, bg, wc, gng, cw, wout):
    assert seq % (2 * tm) == 0 and tm % (2 * CHUNK) == 0
    nt = seq // (2 * tm)
    mst, lvl = _chunk_constants()
    consts = [ng, wa, wr, wgu, bg, wc, gng, cw, wout, mst, lvl]
    row_spec = pl.BlockSpec((2 * tm, D_MODEL), lambda b, j: (b * nt + j, 0))
    est = (8 * _nbytes((tm, D_MODEL), F32) + sum(_nbytes(c.shape, c.dtype) for c in consts)
           + 4 * _nbytes((tm, 3 * HV), F32) + 2 * _nbytes((tm, 3 * HK + 2 * HV), F32)
           + 8 * _nbytes((tm, D_MODEL), F32))
    return pl.pallas_call(
        functools.partial(_mixer_prompt_body, tm),
        grid=(batch, nt),
        in_specs=[row_spec] + [_resident(c.shape) for c in consts],
        out_specs=[row_spec,
                   pl.BlockSpec((1, HK, GLA_DV), lambda b, j: (b, 0, 0)),
                   pl.BlockSpec((1, CONV_K - 1, CONV_WIDTH), lambda b, j: (b, 0, 0))],
        out_shape=[jax.ShapeDtypeStruct((batch * seq, D_MODEL), F32),
                   jax.ShapeDtypeStruct((batch, HK, GLA_DV), F32),
                   jax.ShapeDtypeStruct((batch, CONV_K - 1, CONV_WIDTH), F32)],
        scratch_shapes=[pltpu.VMEM((GLA_DV, HK), F32), pltpu.VMEM((8, CONV_WIDTH), F32)],
        compiler_params=pltpu.CompilerParams(
            dimension_semantics=("arbitrary", "arbitrary"), vmem_limit_bytes=_vmem_limit(est)),
        name="mixer_prompt",
    )(h, *consts)


def _mixer_sample_body(nb, bs, h_ref, s_ref, c0_ref, ng_ref, wa_ref, wr_ref, wgu_ref, bg_ref, wc_ref,
                       gng_ref, cw_ref, wout_ref, o_ref, so_ref, co_ref,
                       q_scr, k_scr, la3_scr, v_scr, g_scr, oc_scr, og_scr):
    i = pl.program_id(0)

    @pl.when(i == 0)
    def _():
        q, k, v, g, la, cb, cc, ch = _finish(
            _mixer_inputs(h_ref[...], ng_ref, wa_ref, wr_ref, wgu_ref, bg_ref, wc_ref))
        q_scr[...] = q.astype(BF16)
        k_scr[...] = k.astype(BF16)
        la3_scr[...] = jnp.concatenate(_split3(la), axis=0)
        v_scr[...] = v
        g_scr[...] = g
        uc = cc * ch
        c0 = c0_ref[...]
        old2 = c0[:, 0:CONV_WIDTH]
        old1 = c0[:, CONV_WIDTH:2 * CONV_WIDTH]
        y = cw_ref[0:1, :] * old2 + cw_ref[1:2, :] * old1 + cw_ref[2:3, :] * uc
        oc_scr[...] = cb * y
        co_ref[...] = jnp.concatenate([old1, uc], axis=1)

    first = pl.multiple_of(i * bs, bs)
    seq_row = lax.broadcasted_iota(jnp.int32, (nb, bs * LANES), 0)
    seq_col = lax.broadcasted_iota(jnp.int32, (nb, bs * LANES), 1) // LANES + first
    pick = jnp.where(seq_row == seq_col, 1.0, 0.0).astype(BF16)
    pick3 = jnp.concatenate([pick, pick, pick], axis=0)
    decay_t = jnp.exp(lax.dot_general(la3_scr[...], pick3, TN_DIMS, preferred_element_type=F32))
    k_t = lax.dot_general(k_scr[...], pick, TN_DIMS, preferred_element_type=F32)
    q_t = lax.dot_general(q_scr[...], pick, TN_DIMS, preferred_element_type=F32)
    v_blk = v_scr[pl.ds(first, bs), :]
    o_rows = []
    for t in range(bs):
        cols = slice(t * LANES, (t + 1) * LANES)
        v_b = jnp.concatenate(
            [jnp.broadcast_to(v_blk[t:t + 1, hd * GLA_DV:(hd + 1) * GLA_DV], (GLA_DK, GLA_DV))
             for hd in range(GLA_HEADS)], axis=0)
        s_new = decay_t[:, cols] * s_ref[t] + k_t[:, cols] * v_b
        so_ref[t] = s_new
        qs = q_t[:, cols] * s_new
        o_rows.append(jnp.concatenate(
            [jnp.sum(qs[hd * GLA_DK:(hd + 1) * GLA_DK, :], axis=0, keepdims=True)
             for hd in range(GLA_HEADS)], axis=1))
    og_scr[pl.ds(first, bs), :] = jnp.concatenate(o_rows, axis=0)

    @pl.when(i == pl.num_programs(0) - 1)
    def _():
        o_ref[...] = _finish(
            _mixer_output(h_ref[...], og_scr[...], g_scr[...], oc_scr[...], ng_ref, gng_ref, wout_ref))


def _mixer_sample(h, s0, c0, bs, ng, wa, wr, wgu, bg, wc, gng, cw, wout):
    nb = h.shape[0]
    assert nb == LANES and nb % bs == 0
    consts = [ng, wa, wr, wgu, bg, wc, gng, cw, wout]
    state_spec = pl.BlockSpec((bs, HK, GLA_DV), lambda i: (i, 0, 0))
    est = (sum(_nbytes(c.shape, c.dtype) for c in consts) + 4 * _nbytes((bs, HK, GLA_DV), F32)
           + 12 * _nbytes((nb, D_MODEL), F32) + 2 * _nbytes((nb, 3 * HV), F32))
    return pl.pallas_call(
        functools.partial(_mixer_sample_body, nb, bs),
        grid=(nb // bs,),
        in_specs=[_resident(h.shape), state_spec, _resident(c0.shape)] + [_resident(c.shape) for c in consts],
        out_specs=[pl.BlockSpec((nb, D_MODEL), lambda i: (0, 0)), state_spec,
                   pl.BlockSpec(c0.shape, lambda i: (0, 0))],
        out_shape=[jax.ShapeDtypeStruct((nb, D_MODEL), F32),
                   jax.ShapeDtypeStruct(s0.shape, F32),
                   jax.ShapeDtypeStruct(c0.shape, F32)],
        scratch_shapes=[pltpu.VMEM((nb, HK), BF16), pltpu.VMEM((nb, HK), BF16), pltpu.VMEM((3 * nb, HK), BF16),
                        pltpu.VMEM((nb, HV), F32), pltpu.VMEM((nb, HV), F32), pltpu.VMEM((nb, CONV_WIDTH), F32),
                        pltpu.VMEM((nb, HV), F32)],
        compiler_params=pltpu.CompilerParams(
            dimension_semantics=("arbitrary",), vmem_limit_bytes=_vmem_limit(est)),
        name="mixer_sample",
    )(h, s0, c0, *consts)


TM_PROMPT = 512
TS_FFN = 512
NSUB_FFN = 2
BS_SAMPLE = 16


def kernel(x_prompt, x_sample, state_gla, state_conv, p_prompt, p_sample, norm_g, w_in, w_gate_up, b_gate,
           gla_norm_g, conv_w, w_out, ffn1_gate, ffn1_up, ffn1_down, ffn2_gate, ffn2_up, ffn2_down,
           w_ple_proj, w_ple_gate):
    batch, seq, _ = x_prompt.shape
    nb = x_sample.shape[0]
    assert norm_g.shape[0] == 1 and x_sample.shape[1] == 1

    ng = norm_g[0]
    f1 = (_gain_into_rows(ffn1_gate[0], ng[0]), _gain_into_rows(ffn1_up[0], ng[0]), ffn1_down[0].astype(BF16))
    wgu = jnp.pad(w_gate_up[0], ((0, LANES - GLA_RANK), (0, 0))).astype(BF16)
    bg = b_gate[0].reshape(1, HK)
    gng = gla_norm_g[0].reshape(1, GLA_DV)
    cw = conv_w[0]
    r0 = 2 * HK + 2 * HV
    r1 = r0 + GLA_RANK
    jobs = [
        _PrepJob(ffn2_gate[0], gain=4), _PrepJob(ffn2_up[0], gain=4), _PrepJob(ffn2_down[0]),
        _PrepJob(w_in[0], gain=2, pieces=[(0, r0, r0), (r0, GLA_RANK, LANES), (r1, w_in.shape[2] - r1,
                                                                              w_in.shape[2] - r1)]),
        _PrepJob(w_out[0]), _PrepJob(w_ple_gate[0], gain=6), _PrepJob(w_ple_proj[0]),
    ]

    xp = x_prompt.reshape(batch * seq, D_MODEL)
    pp = p_prompt[0].reshape(batch * seq, PLE_DIM)
    hp, f2g, f2u, f2d, wa, wr, wc, wout, wpg, wpp = _ffn(xp, ng, *f1, 1, TS_FFN, NSUB_FFN, prep=(ng.T, jobs))
    f2 = (f2g, f2u, f2d)
    mix_w = (ng, wa, wr, wgu, bg, wc, gng, cw, wout)
    hp, sg_p, sc_p = _mixer_prompt(hp, batch, seq, TM_PROMPT, *mix_w)
    hp = _ffn(hp, ng, *f2, 5, TS_FFN, NSUB_FFN, ple_args=(pp, wpp, wpg))

    xs = x_sample.reshape(nb, D_MODEL)
    ps = p_sample[0].reshape(nb, PLE_DIM)
    s0 = state_gla[0].reshape(nb, HK, GLA_DV)
    c0 = state_conv[0].reshape(nb, (CONV_K - 1) * CONV_WIDTH)
    hs = _ffn(xs, ng, *f1, 1, nb, 1)
    hs, sg_s, sc_s = _mixer_sample(hs, s0, c0, BS_SAMPLE, *mix_w)
    hs = _ffn(hs, ng, *f2, 5, nb, 1, ple_args=(ps, wpp, wpg))

    return (hp.reshape(batch, seq, D_MODEL),
            hs.reshape(nb, 1, D_MODEL),
            sg_p.reshape(1, batch, GLA_HEADS, GLA_DK, GLA_DV),
            sc_p.reshape(1, batch, CONV_K - 1, CONV_WIDTH),
            sg_s.reshape(1, nb, GLA_HEADS, GLA_DK, GLA_DV),
            sc_s.reshape(1, nb, CONV_K - 1, CONV_WIDTH))
```

```python
import functools

import numpy as np
import jax
import jax.numpy as jnp
from jax import lax
from jax.experimental import pallas as pl
from jax.experimental.pallas import tpu as pltpu

F32 = jnp.float32
BF16 = jnp.bfloat16

D_MODEL = 1024
PLE_DIM = 256
GLA_HEADS = 4
GLA_DK = 64
GLA_DV = 128
HK = GLA_HEADS * GLA_DK
HV = GLA_HEADS * GLA_DV
GLA_RANK = 16
GLA_GATE_NORM = 16.0
CONV_WIDTH = 512
CONV_K = 3
D_FF = 2816
EPS = 1e-6

LANES = 128
MXU_COLS = 256
CHUNK = 128
N_LEVELS = 8
VMEM_CAP = 56 * 1024 * 1024

NT_DIMS = (((1,), (1,)), ((), ()))
TN_DIMS = (((0,), (0,)), ((), ()))


def _inv_rms(x):
    return lax.rsqrt(jnp.mean(x * x, axis=-1, keepdims=True) + EPS)


def _rms(x, g):
    return x * _inv_rms(x) * g


def _gain_into_rows(w, g):
    return (g[:, None] * w).astype(BF16)


def _sigmoid(x):
    return 1.0 / (1.0 + jnp.exp(-x))


def _log_sigmoid(x):
    return jnp.minimum(x, 0.0) - jnp.log1p(jnp.exp(-jnp.abs(x)))


def _split3(x):
    hi = x.astype(BF16)
    r1 = x - hi.astype(F32)
    mid = r1.astype(BF16)
    lo = (r1 - mid.astype(F32)).astype(BF16)
    return hi, mid, lo


def _resident(shape):
    nd = len(shape)
    return pl.BlockSpec(shape, lambda *_: (0,) * nd, pipeline_mode=pl.Buffered(1))


def _nbytes(shape, dtype):
    return int(np.prod(shape)) * jnp.dtype(dtype).itemsize


def _vmem_limit(est_bytes):
    return int(min(VMEM_CAP, max(est_bytes * 5 // 4, 16 * 1024 * 1024)))


class _PrepJob:
    def __init__(self, w, gain=None, pieces=None):
        self.w = w
        self.gain = gain
        self.pieces = pieces or [(0, w.shape[1], w.shape[1])]


def _prep_rows(job, g_ref, w_ref, out_refs):
    w = w_ref[...]
    if job.gain is not None:
        w = w * g_ref[:, job.gain:job.gain + 1]
    for (c0, keep, width), out_ref in zip(job.pieces, out_refs):
        piece = w[:, c0:c0 + width]
        if keep < width:
            lane = lax.broadcasted_iota(jnp.int32, piece.shape, 1)
            piece = jnp.where(lane < keep, piece, 0.0)
        out_ref[...] = piece.astype(BF16)


def _ffn_body(post, ple, ts, nsub, jobs, *refs):
    refs = list(refs)
    x_ref, ng_ref, wg_ref, wu_ref, wd_ref = refs[:5]
    del refs[:5]
    if ple:
        p_ref, wpp_ref, wpg_ref = refs[:3]
        del refs[:3]
    if jobs:
        g_ref = refs.pop(0)
        job_in = refs[:len(jobs)]
        del refs[:len(jobs)]
    o_ref = refs.pop(0)
    h_scr = refs.pop()
    for job, w_ref in zip(jobs, job_in if jobs else ()):
        outs = refs[:len(job.pieces)]
        del refs[:len(job.pieces)]
        _prep_rows(job, g_ref, w_ref, outs)

    def gate_up(s, beside):
        x = x_ref[s * ts:(s + 1) * ts, :]
        xb = x.astype(BF16)
        inv = _inv_rms(x)
        for c in range(D_FF // MXU_COLS):
            sl = slice(c * MXU_COLS, (c + 1) * MXU_COLS)
            g = jnp.dot(xb, wg_ref[:, sl], preferred_element_type=F32) * inv
            u = jnp.dot(xb, wu_ref[:, sl], preferred_element_type=F32) * inv
            h_scr[s % 2, :, sl] = (g * _sigmoid(g) * u).astype(BF16)
            if c == 0 and beside is not None:
                beside()

    def down(s):
        return jnp.dot(h_scr[s % 2], wd_ref[...], preferred_element_type=F32)

    def epilogue(s, y):
        rows = slice(s * ts, (s + 1) * ts)
        h = x_ref[rows, :] + 0.5 * _rms(y, ng_ref[post:post + 1, :])
        if ple:
            gate = _sigmoid(jnp.dot(h.astype(BF16), wpg_ref[...], preferred_element_type=F32) * _inv_rms(h))
            proj = jnp.dot(p_ref[rows, :].astype(BF16), wpp_ref[...], preferred_element_type=F32)
            h = h + _rms(proj * gate, ng_ref[7:8, :])
        o_ref[rows, :] = h

    pending = None
    for s in range(nsub):
        gate_up(s, pending)
        y = down(s)
        pending = functools.partial(epilogue, s, y)
    pending()


def _ffn(x, ng, wg, wu, wd, post, ts, nsub, ple_args=None, prep=None):
    rows = x.shape[0]
    tm = nsub * ts
    assert rows % tm == 0
    steps = rows // tm
    ple = ple_args is not None
    row_spec = pl.BlockSpec((tm, D_MODEL), lambda i: (i, 0))
    in_specs = [row_spec, _resident(ng.shape), _resident(wg.shape), _resident(wu.shape), _resident(wd.shape)]
    args = [x, ng, wg, wu, wd]
    est = (4 * _nbytes((tm, D_MODEL), F32) + 3 * _nbytes(wg.shape, BF16)
           + 2 * _nbytes((ts, D_FF), BF16) + _nbytes((ts, D_MODEL), BF16)
           + 6 * _nbytes((ts, MXU_COLS), F32) + 6 * _nbytes((ts, D_MODEL), F32))
    if ple:
        p, wpp, wpg = ple_args
        in_specs += [pl.BlockSpec((tm, PLE_DIM), lambda i: (i, 0)), _resident(wpp.shape), _resident(wpg.shape)]
        args += [p, wpp, wpg]
        est += 2 * _nbytes((tm, PLE_DIM), F32) + _nbytes(wpp.shape, BF16) + _nbytes(wpg.shape, BF16)
        est += 3 * _nbytes((ts, D_MODEL), F32)
    jobs = []
    out_specs = [row_spec]
    out_shape = [jax.ShapeDtypeStruct((rows, D_MODEL), F32)]
    if prep is not None:
        gains, jobs = prep
        in_specs.append(pl.BlockSpec((gains.shape[0] // steps, gains.shape[1]), lambda i: (i, 0)))
        args.append(gains)
        for job in jobs:
            r, c = job.w.shape
            rb = r // steps
            assert r % steps == 0 and rb % 16 == 0
            assert job.gain is None or r == gains.shape[0]
            in_specs.append(pl.BlockSpec((rb, c), lambda i: (i, 0)))
            args.append(job.w)
            est += 2 * _nbytes((rb, c), F32)
            for _, _, width in job.pieces:
                out_specs.append(pl.BlockSpec((rb, width), lambda i: (i, 0)))
                out_shape.append(jax.ShapeDtypeStruct((r, width), BF16))
                est += 2 * _nbytes((rb, width), BF16)
    res = pl.pallas_call(
        functools.partial(_ffn_body, post, ple, ts, nsub, jobs),
        grid=(steps,),
        in_specs=in_specs,
        out_specs=out_specs,
        out_shape=out_shape,
        scratch_shapes=[pltpu.VMEM((2, ts, D_FF), BF16)],
        compiler_params=pltpu.CompilerParams(
            dimension_semantics=("arbitrary",), vmem_limit_bytes=_vmem_limit(est)),
        name="ffn_ple" if ple else "ffn",
    )(*args)
    return res if prep is not None else res[0]


def _run_together(*stages):
    results = [None] * len(stages)
    live = list(enumerate(stages))
    while live:
        still = []
        for idx, gen in live:
            try:
                next(gen)
                still.append((idx, gen))
            except StopIteration as stop:
                results[idx] = stop.value
        live = still
    return results


def _finish(stages):
    return _run_together(stages)[0]


def _mixer_inputs(h, ng_ref, wa_ref, wr_ref, wgu_ref, bg_ref, wc_ref):
    hb = h.astype(BF16)
    inv = _inv_rms(h)
    yield
    r = jnp.dot(hb, wr_ref[...], preferred_element_type=F32) * inv
    logit = jnp.dot(r.astype(BF16), wgu_ref[...], preferred_element_type=F32) + bg_ref[...]
    la = _log_sigmoid(logit) * (1.0 / GLA_GATE_NORM)
    yield
    cols = []
    for w_ref in (wa_ref, wc_ref):
        for c in range(0, w_ref.shape[1], MXU_COLS):
            cols.append(jnp.dot(hb, w_ref[:, c:c + MXU_COLS], preferred_element_type=F32) * inv)
            yield
    z = jnp.concatenate(cols, axis=1)
    q = z[:, 0:HK] * (GLA_DK ** -0.5)
    k = z[:, HK:2 * HK]
    v = z[:, 2 * HK:2 * HK + HV]
    g = z[:, 2 * HK + HV:2 * HK + 2 * HV]
    zc = z[:, 2 * HK + 2 * HV:]
    cb = zc[:, 0:CONV_WIDTH]
    cc = zc[:, CONV_WIDTH:2 * CONV_WIDTH]
    ch = zc[:, 2 * CONV_WIDTH:3 * CONV_WIDTH]
    return q, k, v, g, la, cb, cc, ch


def _short_conv(cb, uc, tail, cw_ref):
    row = lax.broadcasted_iota(jnp.int32, uc.shape, 0)
    c1 = tail[7:8, :]
    c2 = tail[6:7, :]
    prev1 = jnp.where(row == 0, c1, pltpu.roll(uc, 1, axis=0))
    prev2 = jnp.where(row == 0, c2, jnp.where(row == 1, c1, pltpu.roll(uc, 2, axis=0)))
    return cb * (cw_ref[0:1, :] * prev2 + cw_ref[1:2, :] * prev1 + cw_ref[2:3, :] * uc)


def _mixer_output(h, o, g, oc, ng_ref, gng_ref, wout_ref):
    heads = []
    for hd in range(GLA_HEADS):
        sl = slice(hd * GLA_DV, (hd + 1) * GLA_DV)
        heads.append(_rms(o[:, sl], gng_ref[...]))
    on = jnp.concatenate(heads, axis=1) * (g * _sigmoid(g))
    cat = jnp.concatenate([on.astype(BF16), oc.astype(BF16)], axis=1)
    yield
    cols = []
    for c in range(0, wout_ref.shape[1], MXU_COLS):
        cols.append(jnp.dot(cat, wout_ref[:, c:c + MXU_COLS], preferred_element_type=F32))
        yield
    return h + _rms(jnp.concatenate(cols, axis=1), ng_ref[3:4, :])


def _chunk_constants():
    t = np.arange(CHUNK)[:, None]
    s = np.arange(CHUNK)[None, :]
    blocks = []
    for j in range(1, N_LEVELS - 1):
        half = 1 << j
        mid = (t // (2 * half)) * (2 * half) + half
        blocks.append((s <= t).astype(np.float32) - (s < mid).astype(np.float32))
    blocks.append((s <= t).astype(np.float32))
    m = np.concatenate(blocks, axis=0)
    mst = np.concatenate([m, m, m], axis=1)
    x = np.bitwise_xor(t, s)
    top = np.floor(np.log2(np.maximum(x, 1))).astype(np.int32) + 1
    lvl = np.where(s > t, -1, np.where(s == t, 0, top)).astype(np.int32)
    return jnp.asarray(mst, BF16), jnp.asarray(np.tile(lvl, (1, 2)), jnp.int32)


def _pair_diag(a, b):
    z = jnp.zeros_like(a)
    return jnp.concatenate([jnp.concatenate([a, z], axis=1), jnp.concatenate([z, b], axis=1)], axis=0)


def _gla_tile(q, k, v, la, st, mst, lvl):
    c = CHUNK
    chunks = [slice(r, r + c) for r in range(0, q.shape[0], c)]
    pairs = [slice(p * 2 * GLA_DK, (p + 1) * 2 * GLA_DK) for p in range(GLA_HEADS // 2)]
    half = len(chunks) // 2
    cat = lambda xs: jnp.concatenate(xs, axis=0)

    def key_weights(kt, r, p):
        lo = 2 * p * GLA_DK
        return _pair_diag(kt[lo:lo + GLA_DK, r], kt[lo + GLA_DK:lo + 2 * GLA_DK, r])

    hi, mid, lo = _split3(la)
    e = []
    for n, r in enumerate(chunks):
        e.append(jnp.dot(mst, cat([hi[r, :], mid[r, :], lo[r, :]]), preferred_element_type=F32))
        if n == half - 1:
            yield
    yield
    bc = cat([en[(N_LEVELS - 2) * c:(N_LEVELS - 1) * c, :] for en in e])
    k_t = k.T
    bc_t = bc.T
    b_last = [bc_t[:, r.stop - 1:r.stop] for r in chunks]
    b_last_cols = jnp.concatenate([jnp.broadcast_to(b, (HK, c)) for b in b_last], axis=1)

    vb = v.astype(BF16)
    k_dec = (k_t * jnp.exp(b_last_cols - bc_t)).astype(BF16)
    upd_all = []
    for n, r in enumerate(chunks):
        upd_all.append(jnp.dot(k_dec[:, r], vb[r, :], preferred_element_type=F32))
        if n == half - 1:
            yield
    yield
    states = []
    for n in range(len(chunks)):
        states.append(st)
        upd = cat([upd_all[n][hd * GLA_DK:(hd + 1) * GLA_DK, hd * GLA_DV:(hd + 1) * GLA_DV]
                   for hd in range(GLA_HEADS)])
        st = st * jnp.exp(b_last[n]) + upd

    qb = q.astype(BF16)
    q1 = (q * jnp.exp(la)).astype(BF16)
    kb = k_t.astype(BF16)
    att = []
    for n, r in enumerate(chunks):
        row = []
        for p, cols in enumerate(pairs):
            x = jnp.dot(cat([qb[r, cols], q1[r, cols]]), key_weights(kb, r, p), preferred_element_type=F32)
            row.append(jnp.where(lvl == 1, x[c:, :], jnp.where(lvl == 0, x[:c, :], 0.0)))
        att.append(row)
        if n == half - 1:
            yield
    yield
    for lv in range(2, N_LEVELS):
        el = cat([en[(lv - 2) * c:(lv - 1) * c, :] for en in e])
        ql = (q * jnp.exp(jnp.minimum(el, 0.0))).astype(BF16)
        kl = (k_t * jnp.exp(jnp.minimum(-el.T, 0.0))).astype(BF16)
        for n, r in enumerate(chunks):
            for p, cols in enumerate(pairs):
                x = jnp.dot(ql[r, cols], key_weights(kl, r, p), preferred_element_type=F32)
                att[n][p] = jnp.where(lvl == lv, x, att[n][p])
        yield

    q_in = (q * jnp.exp(bc)).astype(BF16)
    o = []
    for n, r in enumerate(chunks):
        sb = states[n].astype(BF16)
        row = []
        for p, cols in enumerate(pairs):
            h0 = 2 * p
            s_w = _pair_diag(sb[h0 * GLA_DK:(h0 + 1) * GLA_DK, :], sb[(h0 + 1) * GLA_DK:(h0 + 2) * GLA_DK, :])
            v_w = _pair_diag(vb[r, h0 * GLA_DV:(h0 + 1) * GLA_DV], vb[r, (h0 + 1) * GLA_DV:(h0 + 2) * GLA_DV])
            row.append(jnp.dot(q_in[r, cols], s_w, preferred_element_type=F32)
                       + jnp.dot(att[n][p].astype(BF16), v_w, preferred_element_type=F32))
        o.append(jnp.concatenate(row, axis=1))
        if n == half - 1:
            yield
    return cat(o), st


def _mixer_prompt_body(tm, h_ref, ng_ref, wa_ref, wr_ref, wgu_ref, bg_ref, wc_ref, gng_ref, cw_ref,
                       wout_ref, mst_ref, lvl_ref, o_ref, sg_ref, sc_ref, st_scr, carry_scr):
    j = pl.program_id(1)

    @pl.when(j == 0)
    def _():
        st_scr[...] = jnp.zeros_like(st_scr)
        carry_scr[...] = jnp.zeros_like(carry_scr)

    def inputs(h):
        return _mixer_inputs(h, ng_ref, wa_ref, wr_ref, wgu_ref, bg_ref, wc_ref)

    def gla(q, k, v, la, st):
        return _gla_tile(q, k, v, la, st, mst_ref[...], lvl_ref[...])

    def output(h, og, g, cb, uc, tail):
        return _mixer_output(h, og, g, _short_conv(cb, uc, tail, cw_ref), ng_ref, gng_ref, wout_ref)

    h_a = h_ref[0:tm, :]
    h_b = h_ref[tm:2 * tm, :]
    q, k, v, g_a, la, cb_a, cc, ch = _finish(inputs(h_a))
    uc_a = cc * ch
    (og_a, st), (q, k, v, g_b, la, cb_b, cc, ch) = _run_together(gla(q, k, v, la, st_scr[...]), inputs(h_b))
    uc_b = cc * ch
    out_a, (og_b, st) = _run_together(output(h_a, og_a, g_a, cb_a, uc_a, carry_scr[...]), gla(q, k, v, la, st))
    o_ref[0:tm, :] = out_a
    o_ref[tm:2 * tm, :] = _finish(output(h_b, og_b, g_b, cb_b, uc_b, uc_a[tm - 8:tm, :]))
    st_scr[...] = st
    carry_scr[...] = uc_b[tm - 8:tm, :]

    @pl.when(j == pl.num_programs(1) - 1)
    def _():
        sg_ref[0] = st
        sc_ref[0] = carry_scr[6:8, :]


def _mixer_prompt(h, batch, seq, tm, ng, wa, wr, wgu, bg, wc, gng, cw, wout):
    assert seq % (2 * tm) == 0 and tm % (2 * CHUNK) == 0
    nt = seq // (2 * tm)
    mst, lvl = _chunk_constants()
    consts = [ng, wa, wr, wgu, bg, wc, gng, cw, wout, mst, lvl]
    row_spec = pl.BlockSpec((2 * tm, D_MODEL), lambda b, j: (b * nt + j, 0))
    est = (8 * _nbytes((tm, D_MODEL), F32) + sum(_nbytes(c.shape, c.dtype) for c in consts)
           + 4 * _nbytes((tm, 3 * HV), F32) + 2 * _nbytes((tm, 3 * HK + 2 * HV), F32)
           + 8 * _nbytes((tm, D_MODEL), F32))
    return pl.pallas_call(
        functools.partial(_mixer_prompt_body, tm),
        grid=(batch, nt),
        in_specs=[row_spec] + [_resident(c.shape) for c in consts],
        out_specs=[row_spec,
                   pl.BlockSpec((1, HK, GLA_DV), lambda b, j: (b, 0, 0)),
                   pl.BlockSpec((1, CONV_K - 1, CONV_WIDTH), lambda b, j: (b, 0, 0))],
        out_shape=[jax.ShapeDtypeStruct((batch * seq, D_MODEL), F32),
                   jax.ShapeDtypeStruct((batch, HK, GLA_DV), F32),
                   jax.ShapeDtypeStruct((batch, CONV_K - 1, CONV_WIDTH), F32)],
        scratch_shapes=[pltpu.VMEM((HK, GLA_DV), F32), pltpu.VMEM((8, CONV_WIDTH), F32)],
        compiler_params=pltpu.CompilerParams(
            dimension_semantics=("arbitrary", "arbitrary"), vmem_limit_bytes=_vmem_limit(est)),
        name="mixer_prompt",
    )(h, *consts)


def _mixer_sample_body(nb, bs, h_ref, s_ref, c0_ref, ng_ref, wa_ref, wr_ref, wgu_ref, bg_ref, wc_ref,
                       gng_ref, cw_ref, wout_ref, o_ref, so_ref, co_ref,
                       q_scr, k_scr, la3_scr, v_scr, g_scr, oc_scr, og_scr):
    i = pl.program_id(0)

    @pl.when(i == 0)
    def _():
        q, k, v, g, la, cb, cc, ch = _finish(
            _mixer_inputs(h_ref[...], ng_ref, wa_ref, wr_ref, wgu_ref, bg_ref, wc_ref))
        q_scr[...] = q.astype(BF16)
        k_scr[...] = k.astype(BF16)
        la3_scr[...] = jnp.concatenate(_split3(la), axis=0)
        v_scr[...] = v
        g_scr[...] = g
        uc = cc * ch
        c0 = c0_ref[...]
        old2 = c0[:, 0:CONV_WIDTH]
        old1 = c0[:, CONV_WIDTH:2 * CONV_WIDTH]
        y = cw_ref[0:1, :] * old2 + cw_ref[1:2, :] * old1 + cw_ref[2:3, :] * uc
        oc_scr[...] = cb * y
        co_ref[...] = jnp.concatenate([old1, uc], axis=1)

    first = pl.multiple_of(i * bs, bs)
    seq_row = lax.broadcasted_iota(jnp.int32, (nb, bs * LANES), 0)
    seq_col = lax.broadcasted_iota(jnp.int32, (nb, bs * LANES), 1) // LANES + first
    pick = jnp.where(seq_row == seq_col, 1.0, 0.0).astype(BF16)
    pick3 = jnp.concatenate([pick, pick, pick], axis=0)
    decay_t = jnp.exp(lax.dot_general(la3_scr[...], pick3, TN_DIMS, preferred_element_type=F32))
    k_t = lax.dot_general(k_scr[...], pick, TN_DIMS, preferred_element_type=F32)
    q_t = lax.dot_general(q_scr[...], pick, TN_DIMS, preferred_element_type=F32)
    v_blk = v_scr[pl.ds(first, bs), :]
    o_rows = []
    for t in range(bs):
        cols = slice(t * LANES, (t + 1) * LANES)
        v_b = jnp.concatenate(
            [jnp.broadcast_to(v_blk[t:t + 1, hd * GLA_DV:(hd + 1) * GLA_DV], (GLA_DK, GLA_DV))
             for hd in range(GLA_HEADS)], axis=0)
        s_new = decay_t[:, cols] * s_ref[t] + k_t[:, cols] * v_b
        so_ref[t] = s_new
        qs = q_t[:, cols] * s_new
        o_rows.append(jnp.concatenate(
            [jnp.sum(qs[hd * GLA_DK:(hd + 1) * GLA_DK, :], axis=0, keepdims=True)
             for hd in range(GLA_HEADS)], axis=1))
    og_scr[pl.ds(first, bs), :] = jnp.concatenate(o_rows, axis=0)

    @pl.when(i == pl.num_programs(0) - 1)
    def _():
        o_ref[...] = _finish(
            _mixer_output(h_ref[...], og_scr[...], g_scr[...], oc_scr[...], ng_ref, gng_ref, wout_ref))


def _mixer_sample(h, s0, c0, bs, ng, wa, wr, wgu, bg, wc, gng, cw, wout):
    nb = h.shape[0]
    assert nb == LANES and nb % bs == 0
    consts = [ng, wa, wr, wgu, bg, wc, gng, cw, wout]
    state_spec = pl.BlockSpec((bs, HK, GLA_DV), lambda i: (i, 0, 0))
    est = (sum(_nbytes(c.shape, c.dtype) for c in consts) + 4 * _nbytes((bs, HK, GLA_DV), F32)
           + 12 * _nbytes((nb, D_MODEL), F32) + 2 * _nbytes((nb, 3 * HV), F32))
    return pl.pallas_call(
        functools.partial(_mixer_sample_body, nb, bs),
        grid=(nb // bs,),
        in_specs=[_resident(h.shape), state_spec, _resident(c0.shape)] + [_resident(c.shape) for c in consts],
        out_specs=[pl.BlockSpec((nb, D_MODEL), lambda i: (0, 0)), state_spec,
                   pl.BlockSpec(c0.shape, lambda i: (0, 0))],
        out_shape=[jax.ShapeDtypeStruct((nb, D_MODEL), F32),
                   jax.ShapeDtypeStruct(s0.shape, F32),
                   jax.ShapeDtypeStruct(c0.shape, F32)],
        scratch_shapes=[pltpu.VMEM((nb, HK), BF16), pltpu.VMEM((nb, HK), BF16), pltpu.VMEM((3 * nb, HK), BF16),
                        pltpu.VMEM((nb, HV), F32), pltpu.VMEM((nb, HV), F32), pltpu.VMEM((nb, CONV_WIDTH), F32),
                        pltpu.VMEM((nb, HV), F32)],
        compiler_params=pltpu.CompilerParams(
            dimension_semantics=("arbitrary",), vmem_limit_bytes=_vmem_limit(est)),
        name="mixer_sample",
    )(h, s0, c0, *consts)


TM_PROMPT = 512
TS_FFN = 512
NSUB_FFN = 2
BS_SAMPLE = 16


def kernel(x_prompt, x_sample, state_gla, state_conv, p_prompt, p_sample, norm_g, w_in, w_gate_up, b_gate,
           gla_norm_g, conv_w, w_out, ffn1_gate, ffn1_up, ffn1_down, ffn2_gate, ffn2_up, ffn2_down,
           w_ple_proj, w_ple_gate):
    batch, seq, _ = x_prompt.shape
    nb = x_sample.shape[0]
    assert norm_g.shape[0] == 1 and x_sample.shape[1] == 1

    ng = norm_g[0]
    f1 = (_gain_into_rows(ffn1_gate[0], ng[0]), _gain_into_rows(ffn1_up[0], ng[0]), ffn1_down[0].astype(BF16))
    wgu = jnp.pad(w_gate_up[0], ((0, LANES - GLA_RANK), (0, 0))).astype(BF16)
    bg = b_gate[0].reshape(1, HK)
    gng = gla_norm_g[0].reshape(1, GLA_DV)
    cw = conv_w[0]
    r0 = 2 * HK + 2 * HV
    r1 = r0 + GLA_RANK
    jobs = [
        _PrepJob(ffn2_gate[0], gain=4), _PrepJob(ffn2_up[0], gain=4), _PrepJob(ffn2_down[0]),
        _PrepJob(w_in[0], gain=2, pieces=[(0, r0, r0), (r0, GLA_RANK, LANES), (r1, w_in.shape[2] - r1,
                                                                              w_in.shape[2] - r1)]),
        _PrepJob(w_out[0]), _PrepJob(w_ple_gate[0], gain=6), _PrepJob(w_ple_proj[0]),
    ]

    xp = x_prompt.reshape(batch * seq, D_MODEL)
    pp = p_prompt[0].reshape(batch * seq, PLE_DIM)
    hp, f2g, f2u, f2d, wa, wr, wc, wout, wpg, wpp = _ffn(xp, ng, *f1, 1, TS_FFN, NSUB_FFN, prep=(ng.T, jobs))
    f2 = (f2g, f2u, f2d)
    mix_w = (ng, wa, wr, wgu, bg, wc, gng, cw, wout)
    hp, sg_p, sc_p = _mixer_prompt(hp, batch, seq, TM_PROMPT, *mix_w)
    hp = _ffn(hp, ng, *f2, 5, TS_FFN, NSUB_FFN, ple_args=(pp, wpp, wpg))

    xs = x_sample.reshape(nb, D_MODEL)
    ps = p_sample[0].reshape(nb, PLE_DIM)
    s0 = state_gla[0].reshape(nb, HK, GLA_DV)
    c0 = state_conv[0].reshape(nb, (CONV_K - 1) * CONV_WIDTH)
    hs = _ffn(xs, ng, *f1, 1, nb, 1)
    hs, sg_s, sc_s = _mixer_sample(hs, s0, c0, BS_SAMPLE, *mix_w)
    hs = _ffn(hs, ng, *f2, 5, nb, 1, ple_args=(ps, wpp, wpg))

    return (hp.reshape(batch, seq, D_MODEL),
            hs.reshape(nb, 1, D_MODEL),
            sg_p.reshape(1, batch, GLA_HEADS, GLA_DK, GLA_DV),
            sc_p.reshape(1, batch, CONV_K - 1, CONV_WIDTH),
            sg_s.reshape(1, nb, GLA_HEADS, GLA_DK, GLA_DV),
            sc_s.reshape(1, nb, CONV_K - 1, CONV_WIDTH))
```

```python
import functools

import numpy as np
import jax
import jax.numpy as jnp
from jax import lax
from jax.experimental import pallas as pl
from jax.experimental.pallas import tpu as pltpu

F32 = jnp.float32
BF16 = jnp.bfloat16

D_MODEL = 1024
PLE_DIM = 256
GLA_HEADS = 4
GLA_DK = 64
GLA_DV = 128
HK = GLA_HEADS * GLA_DK
HV = GLA_HEADS * GLA_DV
GLA_RANK = 16
GLA_GATE_NORM = 16.0
CONV_WIDTH = 512
CONV_K = 3
D_FF = 2816
EPS = 1e-6

LANES = 128
MXU_COLS = 256
CHUNK = 128
N_LEVELS = 8
VMEM_CAP = 56 * 1024 * 1024

NT_DIMS = (((1,), (1,)), ((), ()))
TN_DIMS = (((0,), (0,)), ((), ()))


def _inv_rms(x):
    return lax.rsqrt(jnp.mean(x * x, axis=-1, keepdims=True) + EPS)


def _rms(x, g):
    return x * _inv_rms(x) * g


def _gain_into_rows(w, g):
    return (g[:, None] * w).astype(BF16)


def _sigmoid(x):
    return 1.0 / (1.0 + jnp.exp(-x))


def _log_sigmoid(x):
    return jnp.minimum(x, 0.0) - jnp.log1p(jnp.exp(-jnp.abs(x)))


def _zero_from(x):
    bits = x.astype(jnp.int32)
    sixteen = jnp.full(x.shape, 16, jnp.int32)
    return lax.shift_right_logical(lax.shift_right_logical(bits, sixteen), sixteen).astype(F32)


def _split2(x):
    hi = x.astype(BF16)
    return hi, (x - hi.astype(F32)).astype(BF16)


def _split3(x):
    hi = x.astype(BF16)
    r1 = x - hi.astype(F32)
    mid = r1.astype(BF16)
    lo = (r1 - mid.astype(F32)).astype(BF16)
    return hi, mid, lo


def _resident(shape):
    nd = len(shape)
    return pl.BlockSpec(shape, lambda *_: (0,) * nd, pipeline_mode=pl.Buffered(1))


def _nbytes(shape, dtype):
    return int(np.prod(shape)) * jnp.dtype(dtype).itemsize


def _vmem_limit(est_bytes):
    return int(min(VMEM_CAP, max(est_bytes * 5 // 4, 16 * 1024 * 1024)))


class _PrepJob:
    def __init__(self, w, gain=None, pieces=None):
        self.w = w
        self.gain = gain
        self.pieces = pieces or [(0, w.shape[1], w.shape[1])]


def _prep_rows(job, g_ref, w_ref, out_refs):
    w = w_ref[...]
    if job.gain is not None:
        w = w * g_ref[:, job.gain:job.gain + 1]
    for (c0, keep, width), out_ref in zip(job.pieces, out_refs):
        piece = w[:, c0:c0 + width]
        if keep < width:
            lane = lax.broadcasted_iota(jnp.int32, piece.shape, 1)
            piece = jnp.where(lane < keep, piece, 0.0)
        out_ref[...] = piece.astype(BF16)


def _ffn_body(post, ple, sizes, jobs, *refs):
    refs = list(refs)
    x_ref, ng_ref, wg_ref, wu_ref, wd_ref = refs[:5]
    del refs[:5]
    if ple:
        p_ref, wpp_ref, wpg_ref = refs[:3]
        del refs[:3]
    if jobs:
        g_ref = refs.pop(0)
        job_in = refs[:len(jobs)]
        del refs[:len(jobs)]
    o_ref = refs.pop(0)
    h_scr = refs.pop()
    for job, w_ref in zip(jobs, job_in if jobs else ()):
        outs = refs[:len(job.pieces)]
        del refs[:len(job.pieces)]
        _prep_rows(job, g_ref, w_ref, outs)

    starts = [sum(sizes[:s]) for s in range(len(sizes))]

    n_chunks = D_FF // MXU_COLS

    def gate_up(s, beside):
        x = x_ref[starts[s]:starts[s] + sizes[s], :]
        xb = x.astype(BF16)
        inv = _inv_rms(x)
        for c in range(n_chunks):
            sl = slice(c * MXU_COLS, (c + 1) * MXU_COLS)
            if c == n_chunks // 2 and beside is not None:
                inv = inv + _zero_from(beside()[0:sizes[s], 0:1])
            g = jnp.dot(xb, wg_ref[:, sl], preferred_element_type=F32) * inv
            u = jnp.dot(xb, wu_ref[:, sl], preferred_element_type=F32) * inv
            h_scr[s % 2, 0:sizes[s], sl] = (g * _sigmoid(g) * u).astype(BF16)

    def down(s):
        return jnp.dot(h_scr[s % 2, 0:sizes[s], :], wd_ref[...], preferred_element_type=F32)

    def epilogue(s, y):
        rows = slice(starts[s], starts[s] + sizes[s])
        h = x_ref[rows, :] + 0.5 * _rms(y, ng_ref[post:post + 1, :])
        if ple:
            gate = _sigmoid(jnp.dot(h.astype(BF16), wpg_ref[...], preferred_element_type=F32) * _inv_rms(h))
            proj = jnp.dot(p_ref[rows, :].astype(BF16), wpp_ref[...], preferred_element_type=F32)
            h = h + _rms(proj * gate, ng_ref[7:8, :])
        o_ref[rows, :] = h
        return h

    pending = None
    for s in range(len(sizes)):
        gate_up(s, pending)
        y = down(s)
        pending = functools.partial(epilogue, s, y)
    pending()


def _ffn(x, ng, wg, wu, wd, post, sizes, ple_args=None, prep=None):
    rows = x.shape[0]
    tm = sum(sizes)
    ts = max(sizes)
    assert rows % tm == 0 and all(sz % 16 == 0 for sz in sizes)
    assert all(a >= b for a, b in zip(sizes, sizes[1:]))
    steps = rows // tm
    ple = ple_args is not None
    row_spec = pl.BlockSpec((tm, D_MODEL), lambda i: (i, 0))
    in_specs = [row_spec, _resident(ng.shape), _resident(wg.shape), _resident(wu.shape), _resident(wd.shape)]
    args = [x, ng, wg, wu, wd]
    est = (4 * _nbytes((tm, D_MODEL), F32) + 3 * _nbytes(wg.shape, BF16)
           + 2 * _nbytes((ts, D_FF), BF16) + _nbytes((ts, D_MODEL), BF16)
           + 6 * _nbytes((ts, MXU_COLS), F32) + 6 * _nbytes((ts, D_MODEL), F32))
    if ple:
        p, wpp, wpg = ple_args
        in_specs += [pl.BlockSpec((tm, PLE_DIM), lambda i: (i, 0)), _resident(wpp.shape), _resident(wpg.shape)]
        args += [p, wpp, wpg]
        est += 2 * _nbytes((tm, PLE_DIM), F32) + _nbytes(wpp.shape, BF16) + _nbytes(wpg.shape, BF16)
        est += 3 * _nbytes((ts, D_MODEL), F32)
    jobs = []
    out_specs = [row_spec]
    out_shape = [jax.ShapeDtypeStruct((rows, D_MODEL), F32)]
    if prep is not None:
        gains, jobs = prep
        in_specs.append(pl.BlockSpec((gains.shape[0] // steps, gains.shape[1]), lambda i: (i, 0)))
        args.append(gains)
        for job in jobs:
            r, c = job.w.shape
            rb = r // steps
            assert r % steps == 0 and rb % 16 == 0
            assert job.gain is None or r == gains.shape[0]
            in_specs.append(pl.BlockSpec((rb, c), lambda i: (i, 0)))
            args.append(job.w)
            est += 2 * _nbytes((rb, c), F32)
            for _, _, width in job.pieces:
                out_specs.append(pl.BlockSpec((rb, width), lambda i: (i, 0)))
                out_shape.append(jax.ShapeDtypeStruct((r, width), BF16))
                est += 2 * _nbytes((rb, width), BF16)
    res = pl.pallas_call(
        functools.partial(_ffn_body, post, ple, tuple(sizes), jobs),
        grid=(steps,),
        in_specs=in_specs,
        out_specs=out_specs,
        out_shape=out_shape,
        scratch_shapes=[pltpu.VMEM((2, ts, D_FF), BF16)],
        compiler_params=pltpu.CompilerParams(
            dimension_semantics=("arbitrary",), vmem_limit_bytes=_vmem_limit(est)),
        name="ffn_ple" if ple else "ffn",
    )(*args)
    return res if prep is not None else res[0]


def _run_together(*stages):
    results = [None] * len(stages)
    live = list(enumerate(stages))
    while live:
        still = []
        for idx, gen in live:
            try:
                next(gen)
                still.append((idx, gen))
            except StopIteration as stop:
                results[idx] = stop.value
        live = still
    return results


def _finish(stages):
    return _run_together(stages)[0]


def _mixer_inputs(h, ng_ref, wa_ref, wr_ref, wgu_ref, bg_ref, wc_ref):
    hb = h.astype(BF16)
    inv = _inv_rms(h)
    yield
    r = jnp.dot(hb, wr_ref[...], preferred_element_type=F32) * inv
    logit = jnp.dot(r.astype(BF16), wgu_ref[...], preferred_element_type=F32) + bg_ref[...]
    la = _log_sigmoid(logit) * (1.0 / GLA_GATE_NORM)
    yield
    cols = []
    for w_ref in (wa_ref, wc_ref):
        for c in range(0, w_ref.shape[1], MXU_COLS):
            cols.append(jnp.dot(hb, w_ref[:, c:c + MXU_COLS], preferred_element_type=F32) * inv)
            yield
    z = jnp.concatenate(cols, axis=1)
    q = z[:, 0:HK] * (GLA_DK ** -0.5)
    k = z[:, HK:2 * HK]
    v = z[:, 2 * HK:2 * HK + HV]
    g = z[:, 2 * HK + HV:2 * HK + 2 * HV]
    zc = z[:, 2 * HK + 2 * HV:]
    cb = zc[:, 0:CONV_WIDTH]
    cc = zc[:, CONV_WIDTH:2 * CONV_WIDTH]
    ch = zc[:, 2 * CONV_WIDTH:3 * CONV_WIDTH]
    return q, k, v, g, la, cb, cc, ch


def _short_conv(cb, uc, tail, cw_ref):
    row = lax.broadcasted_iota(jnp.int32, uc.shape, 0)
    c1 = tail[7:8, :]
    c2 = tail[6:7, :]
    prev1 = jnp.where(row == 0, c1, pltpu.roll(uc, 1, axis=0))
    prev2 = jnp.where(row == 0, c2, jnp.where(row == 1, c1, pltpu.roll(uc, 2, axis=0)))
    return cb * (cw_ref[0:1, :] * prev2 + cw_ref[1:2, :] * prev1 + cw_ref[2:3, :] * uc)


def _mixer_output(h, o, g, oc, ng_ref, gng_ref, wout_ref):
    heads = []
    for hd in range(GLA_HEADS):
        sl = slice(hd * GLA_DV, (hd + 1) * GLA_DV)
        heads.append(_rms(o[:, sl], gng_ref[...]))
    on = jnp.concatenate(heads, axis=1) * (g * _sigmoid(g))
    cat = jnp.concatenate([on.astype(BF16), oc.astype(BF16)], axis=1)
    yield
    cols = []
    for c in range(0, wout_ref.shape[1], MXU_COLS):
        cols.append(jnp.dot(cat, wout_ref[:, c:c + MXU_COLS], preferred_element_type=F32))
        yield
    return h + _rms(jnp.concatenate(cols, axis=1), ng_ref[3:4, :])


def _chunk_constants():
    t = np.arange(CHUNK)[:, None]
    s = np.arange(CHUNK)[None, :]
    blocks = []
    for j in range(1, N_LEVELS - 1):
        half = 1 << j
        mid = (t // (2 * half)) * (2 * half) + half
        blocks.append((s <= t).astype(np.float32) - (s < mid).astype(np.float32))
    blocks.append((s <= t).astype(np.float32))
    m = np.concatenate(blocks, axis=0)
    mst = np.concatenate([m, m], axis=1)
    x = np.bitwise_xor(t, s)
    top = np.floor(np.log2(np.maximum(x, 1))).astype(np.int32) + 1
    lvl = np.where(s > t, -1, np.where(s == t, 0, top)).astype(np.int32)
    return jnp.asarray(mst, BF16), jnp.asarray(np.tile(lvl, (1, 2)), jnp.int32)


def _pair_diag(a, b):
    z = jnp.zeros_like(a)
    return jnp.concatenate([jnp.concatenate([a, z], axis=1), jnp.concatenate([z, b], axis=1)], axis=0)


def _gla_tile(q, k, v, la, st, mst, lvl):
    c = CHUNK
    chunks = [slice(r, r + c) for r in range(0, q.shape[0], c)]
    pairs = [slice(p * 2 * GLA_DK, (p + 1) * 2 * GLA_DK) for p in range(GLA_HEADS // 2)]
    half = len(chunks) // 2
    cat = lambda xs: jnp.concatenate(xs, axis=0)

    def key_weights(kt, r, p):
        lo = 2 * p * GLA_DK
        return _pair_diag(kt[lo:lo + GLA_DK, r], kt[lo + GLA_DK:lo + 2 * GLA_DK, r])

    hi, lo = _split2(la)
    e = []
    for n, r in enumerate(chunks):
        e.append(jnp.dot(mst, cat([hi[r, :], lo[r, :]]), preferred_element_type=F32))
        if n == half - 1:
            yield
    yield
    bc = cat([en[(N_LEVELS - 2) * c:(N_LEVELS - 1) * c, :] for en in e])
    k_t = k.T
    bc_t = bc.T
    b_last = [bc_t[:, r.stop - 1:r.stop] for r in chunks]
    b_last_cols = jnp.concatenate([jnp.broadcast_to(b, (HK, c)) for b in b_last], axis=1)

    vb = v.astype(BF16)
    k_dec = (k_t * jnp.exp(b_last_cols - bc_t)).astype(BF16)
    upd_all = []
    for n, r in enumerate(chunks):
        upd_all.append(jnp.dot(k_dec[:, r], vb[r, :], preferred_element_type=F32))
        if n == half - 1:
            yield
    yield
    states = []
    for n in range(len(chunks)):
        states.append(st)
        upd = cat([upd_all[n][hd * GLA_DK:(hd + 1) * GLA_DK, hd * GLA_DV:(hd + 1) * GLA_DV]
                   for hd in range(GLA_HEADS)])
        st = st * jnp.exp(b_last[n]) + upd

    qb = q.astype(BF16)
    q1 = (q * jnp.exp(la)).astype(BF16)
    kb = k_t.astype(BF16)
    att = []
    for n, r in enumerate(chunks):
        row = []
        for p, cols in enumerate(pairs):
            x = jnp.dot(cat([qb[r, cols], q1[r, cols]]), key_weights(kb, r, p), preferred_element_type=F32)
            row.append(jnp.where(lvl == 1, x[c:, :], jnp.where(lvl == 0, x[:c, :], 0.0)))
        att.append(row)
        if n == half - 1:
            yield
    yield
    for lv in range(2, N_LEVELS):
        el = cat([en[(lv - 2) * c:(lv - 1) * c, :] for en in e])
        ql = (q * jnp.exp(jnp.minimum(el, 0.0))).astype(BF16)
        kl = (k_t * jnp.exp(jnp.minimum(-el.T, 0.0))).astype(BF16)
        for n, r in enumerate(chunks):
            for p, cols in enumerate(pairs):
                x = jnp.dot(ql[r, cols], key_weights(kl, r, p), preferred_element_type=F32)
                att[n][p] = jnp.where(lvl == lv, x, att[n][p])
        yield

    q_in = (q * jnp.exp(bc)).astype(BF16)
    o = []
    for n, r in enumerate(chunks):
        sb = states[n].astype(BF16)
        row = []
        for p, cols in enumerate(pairs):
            h0 = 2 * p
            s_w = _pair_diag(sb[h0 * GLA_DK:(h0 + 1) * GLA_DK, :], sb[(h0 + 1) * GLA_DK:(h0 + 2) * GLA_DK, :])
            v_w = _pair_diag(vb[r, h0 * GLA_DV:(h0 + 1) * GLA_DV], vb[r, (h0 + 1) * GLA_DV:(h0 + 2) * GLA_DV])
            row.append(jnp.dot(q_in[r, cols], s_w, preferred_element_type=F32)
                       + jnp.dot(att[n][p].astype(BF16), v_w, preferred_element_type=F32))
        o.append(jnp.concatenate(row, axis=1))
        if n == half - 1:
            yield
    return cat(o), st


def _mixer_prompt_body(tm, h_ref, ng_ref, wa_ref, wr_ref, wgu_ref, bg_ref, wc_ref, gng_ref, cw_ref,
                       wout_ref, mst_ref, lvl_ref, o_ref, sg_ref, sc_ref, st_scr, carry_scr):
    j = pl.program_id(1)

    @pl.when(j == 0)
    def _():
        st_scr[...] = jnp.zeros_like(st_scr)
        carry_scr[...] = jnp.zeros_like(carry_scr)

    def inputs(h):
        return _mixer_inputs(h, ng_ref, wa_ref, wr_ref, wgu_ref, bg_ref, wc_ref)

    def gla(q, k, v, la, st):
        return _gla_tile(q, k, v, la, st, mst_ref[...], lvl_ref[...])

    def output(h, og, g, cb, uc, tail):
        return _mixer_output(h, og, g, _short_conv(cb, uc, tail, cw_ref), ng_ref, gng_ref, wout_ref)

    h_a = h_ref[0:tm, :]
    h_b = h_ref[tm:2 * tm, :]
    q, k, v, g_a, la, cb_a, cc, ch = _finish(inputs(h_a))
    uc_a = cc * ch
    (og_a, st), (q, k, v, g_b, la, cb_b, cc, ch) = _run_together(gla(q, k, v, la, st_scr[...]), inputs(h_b))
    uc_b = cc * ch
    out_a, (og_b, st) = _run_together(output(h_a, og_a, g_a, cb_a, uc_a, carry_scr[...]), gla(q, k, v, la, st))
    o_ref[0:tm, :] = out_a
    o_ref[tm:2 * tm, :] = _finish(output(h_b, og_b, g_b, cb_b, uc_b, uc_a[tm - 8:tm, :]))
    st_scr[...] = st
    carry_scr[...] = uc_b[tm - 8:tm, :]

    @pl.when(j == pl.num_programs(1) - 1)
    def _():
        sg_ref[0] = st
        sc_ref[0] = carry_scr[6:8, :]


def _mixer_prompt(h, batch, seq, tm, ng, wa, wr, wgu, bg, wc, gng, cw, wout):
    assert seq % (2 * tm) == 0 and tm % (2 * CHUNK) == 0
    nt = seq // (2 * tm)
    mst, lvl = _chunk_constants()
    consts = [ng, wa, wr, wgu, bg, wc, gng, cw, wout, mst, lvl]
    row_spec = pl.BlockSpec((2 * tm, D_MODEL), lambda b, j: (b * nt + j, 0))
    est = (8 * _nbytes((tm, D_MODEL), F32) + sum(_nbytes(c.shape, c.dtype) for c in consts)
           + 4 * _nbytes((tm, 3 * HV), F32) + 2 * _nbytes((tm, 3 * HK + 2 * HV), F32)
           + 8 * _nbytes((tm, D_MODEL), F32))
    return pl.pallas_call(
        functools.partial(_mixer_prompt_body, tm),
        grid=(batch, nt),
        in_specs=[row_spec] + [_resident(c.shape) for c in consts],
        out_specs=[row_spec,
                   pl.BlockSpec((1, HK, GLA_DV), lambda b, j: (b, 0, 0)),
                   pl.BlockSpec((1, CONV_K - 1, CONV_WIDTH), lambda b, j: (b, 0, 0))],
        out_shape=[jax.ShapeDtypeStruct((batch * seq, D_MODEL), F32),
                   jax.ShapeDtypeStruct((batch, HK, GLA_DV), F32),
                   jax.ShapeDtypeStruct((batch, CONV_K - 1, CONV_WIDTH), F32)],
        scratch_shapes=[pltpu.VMEM((HK, GLA_DV), F32), pltpu.VMEM((8, CONV_WIDTH), F32)],
        compiler_params=pltpu.CompilerParams(
            dimension_semantics=("arbitrary", "arbitrary"), vmem_limit_bytes=_vmem_limit(est)),
        name="mixer_prompt",
    )(h, *consts)


def _mixer_sample_body(nb, bs, h_ref, s_ref, c0_ref, ng_ref, wa_ref, wr_ref, wgu_ref, bg_ref, wc_ref,
                       gng_ref, cw_ref, wout_ref, o_ref, so_ref, co_ref,
                       q_scr, k_scr, la3_scr, v_scr, g_scr, oc_scr, og_scr):
    i = pl.program_id(0)

    @pl.when(i == 0)
    def _():
        q, k, v, g, la, cb, cc, ch = _finish(
            _mixer_inputs(h_ref[...], ng_ref, wa_ref, wr_ref, wgu_ref, bg_ref, wc_ref))
        q_scr[...] = q.astype(BF16)
        k_scr[...] = k.astype(BF16)
        la3_scr[...] = jnp.concatenate(_split3(la), axis=0)
        v_scr[...] = v
        g_scr[...] = g
        uc = cc * ch
        c0 = c0_ref[...]
        old2 = c0[:, 0:CONV_WIDTH]
        old1 = c0[:, CONV_WIDTH:2 * CONV_WIDTH]
        y = cw_ref[0:1, :] * old2 + cw_ref[1:2, :] * old1 + cw_ref[2:3, :] * uc
        oc_scr[...] = cb * y
        co_ref[...] = jnp.concatenate([old1, uc], axis=1)

    first = pl.multiple_of(i * bs, bs)
    seq_row = lax.broadcasted_iota(jnp.int32, (nb, bs * LANES), 0)
    seq_col = lax.broadcasted_iota(jnp.int32, (nb, bs * LANES), 1) // LANES + first
    pick = jnp.where(seq_row == seq_col, 1.0, 0.0).astype(BF16)
    pick3 = jnp.concatenate([pick, pick, pick], axis=0)
    decay_t = jnp.exp(lax.dot_general(la3_scr[...], pick3, TN_DIMS, preferred_element_type=F32))
    k_t = lax.dot_general(k_scr[...], pick, TN_DIMS, preferred_element_type=F32)
    q_t = lax.dot_general(q_scr[...], pick, TN_DIMS, preferred_element_type=F32)
    v_blk = v_scr[pl.ds(first, bs), :]
    o_rows = []
    for t in range(bs):
        cols = slice(t * LANES, (t + 1) * LANES)
        v_b = jnp.concatenate(
            [jnp.broadcast_to(v_blk[t:t + 1, hd * GLA_DV:(hd + 1) * GLA_DV], (GLA_DK, GLA_DV))
             for hd in range(GLA_HEADS)], axis=0)
        s_new = decay_t[:, cols] * s_ref[t] + k_t[:, cols] * v_b
        so_ref[t] = s_new
        qs = q_t[:, cols] * s_new
        o_rows.append(jnp.concatenate(
            [jnp.sum(qs[hd * GLA_DK:(hd + 1) * GLA_DK, :], axis=0, keepdims=True)
             for hd in range(GLA_HEADS)], axis=1))
    og_scr[pl.ds(first, bs), :] = jnp.concatenate(o_rows, axis=0)

    @pl.when(i == pl.num_programs(0) - 1)
    def _():
        o_ref[...] = _finish(
            _mixer_output(h_ref[...], og_scr[...], g_scr[...], oc_scr[...], ng_ref, gng_ref, wout_ref))


def _mixer_sample(h, s0, c0, bs, ng, wa, wr, wgu, bg, wc, gng, cw, wout):
    nb = h.shape[0]
    assert nb == LANES and nb % bs == 0
    consts = [ng, wa, wr, wgu, bg, wc, gng, cw, wout]
    state_spec = pl.BlockSpec((bs, HK, GLA_DV), lambda i: (i, 0, 0))
    est = (sum(_nbytes(c.shape, c.dtype) for c in consts) + 4 * _nbytes((bs, HK, GLA_DV), F32)
           + 12 * _nbytes((nb, D_MODEL), F32) + 2 * _nbytes((nb, 3 * HV), F32))
    return pl.pallas_call(
        functools.partial(_mixer_sample_body, nb, bs),
        grid=(nb // bs,),
        in_specs=[_resident(h.shape), state_spec, _resident(c0.shape)] + [_resident(c.shape) for c in consts],
        out_specs=[pl.BlockSpec((nb, D_MODEL), lambda i: (0, 0)), state_spec,
                   pl.BlockSpec(c0.shape, lambda i: (0, 0))],
        out_shape=[jax.ShapeDtypeStruct((nb, D_MODEL), F32),
                   jax.ShapeDtypeStruct(s0.shape, F32),
                   jax.ShapeDtypeStruct(c0.shape, F32)],
        scratch_shapes=[pltpu.VMEM((nb, HK), BF16), pltpu.VMEM((nb, HK), BF16), pltpu.VMEM((3 * nb, HK), BF16),
                        pltpu.VMEM((nb, HV), F32), pltpu.VMEM((nb, HV), F32), pltpu.VMEM((nb, CONV_WIDTH), F32),
                        pltpu.VMEM((nb, HV), F32)],
        compiler_params=pltpu.CompilerParams(
            dimension_semantics=("arbitrary",), vmem_limit_bytes=_vmem_limit(est)),
        name="mixer_sample",
    )(h, s0, c0, *consts)


TM_PROMPT = 512
FFN_SIZES = (768, 256)
FFN_PLE_SIZES = (256, 256, 256, 256)
BS_SAMPLE = 16


def kernel(x_prompt, x_sample, state_gla, state_conv, p_prompt, p_sample, norm_g, w_in, w_gate_up, b_gate,
           gla_norm_g, conv_w, w_out, ffn1_gate, ffn1_up, ffn1_down, ffn2_gate, ffn2_up, ffn2_down,
           w_ple_proj, w_ple_gate):
    batch, seq, _ = x_prompt.shape
    nb = x_sample.shape[0]
    assert norm_g.shape[0] == 1 and x_sample.shape[1] == 1

    ng = norm_g[0]
    f1 = (_gain_into_rows(ffn1_gate[0], ng[0]), _gain_into_rows(ffn1_up[0], ng[0]), ffn1_down[0].astype(BF16))
    wgu = jnp.pad(w_gate_up[0], ((0, LANES - GLA_RANK), (0, 0))).astype(BF16)
    bg = b_gate[0].reshape(1, HK)
    gng = gla_norm_g[0].reshape(1, GLA_DV)
    cw = conv_w[0]
    r0 = 2 * HK + 2 * HV
    r1 = r0 + GLA_RANK
    jobs = [
        _PrepJob(ffn2_gate[0], gain=4), _PrepJob(ffn2_up[0], gain=4), _PrepJob(ffn2_down[0]),
        _PrepJob(w_in[0], gain=2, pieces=[(0, r0, r0), (r0, GLA_RANK, LANES), (r1, w_in.shape[2] - r1,
                                                                              w_in.shape[2] - r1)]),
        _PrepJob(w_out[0]), _PrepJob(w_ple_gate[0], gain=6), _PrepJob(w_ple_proj[0]),
    ]

    xp = x_prompt.reshape(batch * seq, D_MODEL)
    pp = p_prompt[0].reshape(batch * seq, PLE_DIM)
    hp, f2g, f2u, f2d, wa, wr, wc, wout, wpg, wpp = _ffn(xp, ng, *f1, 1, FFN_SIZES, prep=(ng.T, jobs))
    f2 = (f2g, f2u, f2d)
    mix_w = (ng, wa, wr, wgu, bg, wc, gng, cw, wout)
    hp, sg_p, sc_p = _mixer_prompt(hp, batch, seq, TM_PROMPT, *mix_w)
    hp = _ffn(hp, ng, *f2, 5, FFN_PLE_SIZES, ple_args=(pp, wpp, wpg))

    xs = x_sample.reshape(nb, D_MODEL)
    ps = p_sample[0].reshape(nb, PLE_DIM)
    s0 = state_gla[0].reshape(nb, HK, GLA_DV)
    c0 = state_conv[0].reshape(nb, (CONV_K - 1) * CONV_WIDTH)
    hs = _ffn(xs, ng, *f1, 1, (nb,))
    hs, sg_s, sc_s = _mixer_sample(hs, s0, c0, BS_SAMPLE, *mix_w)
    hs = _ffn(hs, ng, *f2, 5, (nb,), ple_args=(ps, wpp, wpg))

    return (hp.reshape(batch, seq, D_MODEL),
            hs.reshape(nb, 1, D_MODEL),
            sg_p.reshape(1, batch, GLA_HEADS, GLA_DK, GLA_DV),
            sc_p.reshape(1, batch, CONV_K - 1, CONV_WIDTH),
            sg_s.reshape(1, nb, GLA_HEADS, GLA_DK, GLA_DV),
            sc_s.reshape(1, nb, CONV_K - 1, CONV_WIDTH))
```

```python
import functools

import numpy as np
import jax
import jax.numpy as jnp
from jax import lax
from jax.experimental import pallas as pl
from jax.experimental.pallas import tpu as pltpu

F32 = jnp.float32
BF16 = jnp.bfloat16

D_MODEL = 1024
PLE_DIM = 256
GLA_HEADS = 4
GLA_DK = 64
GLA_DV = 128
HK = GLA_HEADS * GLA_DK
HV = GLA_HEADS * GLA_DV
GLA_RANK = 16
GLA_GATE_NORM = 16.0
CONV_WIDTH = 512
CONV_K = 3
D_FF = 2816
EPS = 1e-6

LANES = 128
MXU_COLS = 256
CHUNK = 128
N_LEVELS = 8
VMEM_CAP = 56 * 1024 * 1024

NT_DIMS = (((1,), (1,)), ((), ()))
TN_DIMS = (((0,), (0,)), ((), ()))


def _inv_rms(x):
    return lax.rsqrt(jnp.mean(x * x, axis=-1, keepdims=True) + EPS)


def _rms(x, g):
    return x * _inv_rms(x) * g


def _gain_into_rows(w, g):
    return (g[:, None] * w).astype(BF16)


def _sigmoid(x):
    return 1.0 / (1.0 + jnp.exp(-x))


def _log_sigmoid(x):
    return jnp.minimum(x, 0.0) - jnp.log1p(jnp.exp(-jnp.abs(x)))


def _zero_from(x):
    bits = x.astype(jnp.int32)
    sixteen = jnp.full(x.shape, 16, jnp.int32)
    return lax.shift_right_logical(lax.shift_right_logical(bits, sixteen), sixteen).astype(F32)


def _split2(x):
    hi = x.astype(BF16)
    return hi, (x - hi.astype(F32)).astype(BF16)


def _split3(x):
    hi = x.astype(BF16)
    r1 = x - hi.astype(F32)
    mid = r1.astype(BF16)
    lo = (r1 - mid.astype(F32)).astype(BF16)
    return hi, mid, lo


def _resident(shape):
    nd = len(shape)
    return pl.BlockSpec(shape, lambda *_: (0,) * nd, pipeline_mode=pl.Buffered(1))


def _nbytes(shape, dtype):
    return int(np.prod(shape)) * jnp.dtype(dtype).itemsize


def _vmem_limit(est_bytes):
    return int(min(VMEM_CAP, max(est_bytes * 5 // 4, 16 * 1024 * 1024)))


class _PrepJob:
    def __init__(self, w, gain=None, pieces=None, transposed=False):
        self.w = w
        self.gain = gain
        self.transposed = transposed
        self.rows, self.cols = (w.shape[1], w.shape[0]) if transposed else w.shape
        self.pieces = pieces or [(0, self.cols, self.cols)]


def _rows_of_transposed(w_t, rb):
    cols = w_t.shape[0]
    whole = cols // LANES * LANES
    parts = [w_t[j:j + LANES, :].T for j in range(0, whole, LANES)]
    if cols > whole:
        parts.append(w_t[cols - LANES:cols, :].T[:, LANES - (cols - whole):])
    t = jnp.concatenate(parts, axis=1)
    which = pl.program_id(0) % (LANES // rb)
    w = t[0:rb, :]
    for j in range(1, LANES // rb):
        w = jnp.where(which == j, t[j * rb:(j + 1) * rb, :], w)
    return w


def _prep_rows(job, g_ref, w_ref, out_refs):
    w = w_ref[...]
    if job.transposed:
        w = _rows_of_transposed(w, job.rb)
    if job.gain is not None:
        w = w * g_ref[:, job.gain:job.gain + 1]
    for (c0, keep, width), out_ref in zip(job.pieces, out_refs):
        piece = w[:, c0:c0 + width]
        if keep < width:
            lane = lax.broadcasted_iota(jnp.int32, piece.shape, 1)
            piece = jnp.where(lane < keep, piece, 0.0)
        out_ref[...] = piece.astype(BF16)


def _ffn_body(post, ple, sizes, jobs, *refs):
    refs = list(refs)
    x_ref, ng_ref, wg_ref, wu_ref, wd_ref = refs[:5]
    del refs[:5]
    if ple:
        p_ref, wpp_ref, wpg_ref = refs[:3]
        del refs[:3]
    if jobs:
        g_ref = refs.pop(0)
        job_in = refs[:len(jobs)]
        del refs[:len(jobs)]
    o_ref = refs.pop(0)
    h_scr = refs.pop()
    for job, w_ref in zip(jobs, job_in if jobs else ()):
        outs = refs[:len(job.pieces)]
        del refs[:len(job.pieces)]
        _prep_rows(job, g_ref, w_ref, outs)

    starts = [sum(sizes[:s]) for s in range(len(sizes))]

    n_chunks = D_FF // MXU_COLS

    def gate_up(s, beside):
        x = x_ref[starts[s]:starts[s] + sizes[s], :]
        xb = x.astype(BF16)
        inv = _inv_rms(x)
        for c in range(n_chunks):
            sl = slice(c * MXU_COLS, (c + 1) * MXU_COLS)
            if c == n_chunks // 2 and beside is not None:
                inv = inv + _zero_from(beside()[0:sizes[s], 0:1])
            g = jnp.dot(xb, wg_ref[:, sl], preferred_element_type=F32) * inv
            u = jnp.dot(xb, wu_ref[:, sl], preferred_element_type=F32) * inv
            h_scr[s % 2, 0:sizes[s], sl] = (g * _sigmoid(g) * u).astype(BF16)

    def down(s):
        return jnp.dot(h_scr[s % 2, 0:sizes[s], :], wd_ref[...], preferred_element_type=F32)

    def epilogue(s, y):
        rows = slice(starts[s], starts[s] + sizes[s])
        h = x_ref[rows, :] + 0.5 * _rms(y, ng_ref[post:post + 1, :])
        if ple:
            gate = _sigmoid(jnp.dot(h.astype(BF16), wpg_ref[...], preferred_element_type=F32) * _inv_rms(h))
            proj = jnp.dot(p_ref[rows, :].astype(BF16), wpp_ref[...], preferred_element_type=F32)
            h = h + _rms(proj * gate, ng_ref[7:8, :])
        o_ref[rows, :] = h
        return h

    pending = None
    for s in range(len(sizes)):
        gate_up(s, pending)
        y = down(s)
        pending = functools.partial(epilogue, s, y)
    pending()


def _ffn(x, ng, wg, wu, wd, post, sizes, ple_args=None, prep=None):
    rows = x.shape[0]
    tm = sum(sizes)
    ts = max(sizes)
    assert rows % tm == 0 and all(sz % 16 == 0 for sz in sizes)
    assert all(a >= b for a, b in zip(sizes, sizes[1:]))
    steps = rows // tm
    ple = ple_args is not None
    row_spec = pl.BlockSpec((tm, D_MODEL), lambda i: (i, 0))
    in_specs = [row_spec, _resident(ng.shape), _resident(wg.shape), _resident(wu.shape), _resident(wd.shape)]
    args = [x, ng, wg, wu, wd]
    est = (4 * _nbytes((tm, D_MODEL), F32) + 3 * _nbytes(wg.shape, BF16)
           + 2 * _nbytes((ts, D_FF), BF16) + _nbytes((ts, D_MODEL), BF16)
           + 6 * _nbytes((ts, MXU_COLS), F32) + 6 * _nbytes((ts, D_MODEL), F32))
    if ple:
        p, wpp, wpg = ple_args
        in_specs += [pl.BlockSpec((tm, PLE_DIM), lambda i: (i, 0)), _resident(wpp.shape), _resident(wpg.shape)]
        args += [p, wpp, wpg]
        est += 2 * _nbytes((tm, PLE_DIM), F32) + _nbytes(wpp.shape, BF16) + _nbytes(wpg.shape, BF16)
        est += 3 * _nbytes((ts, D_MODEL), F32)
    jobs = []
    out_specs = [row_spec]
    out_shape = [jax.ShapeDtypeStruct((rows, D_MODEL), F32)]
    if prep is not None:
        gains, jobs = prep
        in_specs.append(pl.BlockSpec((gains.shape[0] // steps, gains.shape[1]), lambda i: (i, 0)))
        args.append(gains)
        for job in jobs:
            r, c = job.rows, job.cols
            rb = job.rb = r // steps
            assert r % steps == 0 and rb % 16 == 0
            assert job.gain is None or r == gains.shape[0]
            if job.transposed:
                assert LANES % rb == 0
                in_specs.append(pl.BlockSpec((c, LANES), lambda i, k=LANES // rb: (0, i // k)))
            else:
                in_specs.append(pl.BlockSpec((rb, c), lambda i: (i, 0)))
            args.append(job.w)
            est += 2 * _nbytes((rb, c), F32)
            for _, _, width in job.pieces:
                out_specs.append(pl.BlockSpec((rb, width), lambda i: (i, 0)))
                out_shape.append(jax.ShapeDtypeStruct((r, width), BF16))
                est += 2 * _nbytes((rb, width), BF16)
    res = pl.pallas_call(
        functools.partial(_ffn_body, post, ple, tuple(sizes), jobs),
        grid=(steps,),
        in_specs=in_specs,
        out_specs=out_specs,
        out_shape=out_shape,
        scratch_shapes=[pltpu.VMEM((2, ts, D_FF), BF16)],
        compiler_params=pltpu.CompilerParams(
            dimension_semantics=("arbitrary",), vmem_limit_bytes=_vmem_limit(est)),
        name="ffn_ple" if ple else "ffn",
    )(*args)
    return res if prep is not None else res[0]


def _run_together(*stages):
    results = [None] * len(stages)
    live = list(enumerate(stages))
    while live:
        still = []
        for idx, gen in live:
            try:
                next(gen)
                still.append((idx, gen))
            except StopIteration as stop:
                results[idx] = stop.value
        live = still
    return results


def _finish(stages):
    return _run_together(stages)[0]


def _mixer_inputs(h, ng_ref, wa_ref, wr_ref, wgu_ref, bg_ref, wc_ref):
    hb = h.astype(BF16)
    inv = _inv_rms(h)
    yield
    r = jnp.dot(hb, wr_ref[...], preferred_element_type=F32) * inv
    logit = jnp.dot(r.astype(BF16), wgu_ref[...], preferred_element_type=F32) + bg_ref[...]
    la = _log_sigmoid(logit) * (1.0 / GLA_GATE_NORM)
    yield
    cols = []
    for w_ref in (wa_ref, wc_ref):
        for c in range(0, w_ref.shape[1], MXU_COLS):
            cols.append(jnp.dot(hb, w_ref[:, c:c + MXU_COLS], preferred_element_type=F32) * inv)
            yield
    z = jnp.concatenate(cols, axis=1)
    q = z[:, 0:HK] * (GLA_DK ** -0.5)
    k = z[:, HK:2 * HK]
    v = z[:, 2 * HK:2 * HK + HV]
    g = z[:, 2 * HK + HV:2 * HK + 2 * HV]
    zc = z[:, 2 * HK + 2 * HV:]
    cb = zc[:, 0:CONV_WIDTH]
    cc = zc[:, CONV_WIDTH:2 * CONV_WIDTH]
    ch = zc[:, 2 * CONV_WIDTH:3 * CONV_WIDTH]
    return q, k, v, g, la, cb, cc, ch


def _short_conv(cb, uc, tail, cw_ref):
    row = lax.broadcasted_iota(jnp.int32, uc.shape, 0)
    c1 = tail[7:8, :]
    c2 = tail[6:7, :]
    prev1 = jnp.where(row == 0, c1, pltpu.roll(uc, 1, axis=0))
    prev2 = jnp.where(row == 0, c2, jnp.where(row == 1, c1, pltpu.roll(uc, 2, axis=0)))
    return cb * (cw_ref[0:1, :] * prev2 + cw_ref[1:2, :] * prev1 + cw_ref[2:3, :] * uc)


def _mixer_output(h, o, g, oc, ng_ref, gng_ref, wout_ref):
    heads = []
    for hd in range(GLA_HEADS):
        sl = slice(hd * GLA_DV, (hd + 1) * GLA_DV)
        heads.append(_rms(o[:, sl], gng_ref[...]))
    on = jnp.concatenate(heads, axis=1) * (g * _sigmoid(g))
    cat = jnp.concatenate([on.astype(BF16), oc.astype(BF16)], axis=1)
    yield
    cols = []
    for c in range(0, wout_ref.shape[1], MXU_COLS):
        cols.append(jnp.dot(cat, wout_ref[:, c:c + MXU_COLS], preferred_element_type=F32))
        yield
    return h + _rms(jnp.concatenate(cols, axis=1), ng_ref[3:4, :])


def _chunk_constants():
    t = np.arange(CHUNK)[:, None]
    s = np.arange(CHUNK)[None, :]
    blocks = []
    for j in range(1, N_LEVELS - 1):
        half = 1 << j
        mid = (t // (2 * half)) * (2 * half) + half
        blocks.append((s <= t).astype(np.float32) - (s < mid).astype(np.float32))
    blocks.append((s <= t).astype(np.float32))
    m = np.concatenate(blocks, axis=0)
    mst = np.concatenate([m, m], axis=1)
    x = np.bitwise_xor(t, s)
    top = np.floor(np.log2(np.maximum(x, 1))).astype(np.int32) + 1
    lvl = np.where(s > t, -1, np.where(s == t, 0, top)).astype(np.int32)
    return jnp.asarray(mst, BF16), jnp.asarray(np.tile(lvl, (1, 2)), jnp.int32)


def _pair_diag(a, b):
    z = jnp.zeros_like(a)
    return jnp.concatenate([jnp.concatenate([a, z], axis=1), jnp.concatenate([z, b], axis=1)], axis=0)


def _gla_tile(q, k, v, la, st, mst, lvl):
    c = CHUNK
    chunks = [slice(r, r + c) for r in range(0, q.shape[0], c)]
    pairs = [slice(p * 2 * GLA_DK, (p + 1) * 2 * GLA_DK) for p in range(GLA_HEADS // 2)]
    half = len(chunks) // 2
    cat = lambda xs: jnp.concatenate(xs, axis=0)

    def key_weights(kt, r, p):
        lo = 2 * p * GLA_DK
        return _pair_diag(kt[lo:lo + GLA_DK, r], kt[lo + GLA_DK:lo + 2 * GLA_DK, r])

    hi, lo = _split2(la)
    e = []
    for n, r in enumerate(chunks):
        e.append(jnp.dot(mst, cat([hi[r, :], lo[r, :]]), preferred_element_type=F32))
        if n == half - 1:
            yield
    yield
    bc = cat([en[(N_LEVELS - 2) * c:(N_LEVELS - 1) * c, :] for en in e])
    k_t = k.T
    bc_t = bc.T
    b_last = [bc_t[:, r.stop - 1:r.stop] for r in chunks]
    b_last_cols = jnp.concatenate([jnp.broadcast_to(b, (HK, c)) for b in b_last], axis=1)

    vb = v.astype(BF16)
    k_dec = (k_t * jnp.exp(b_last_cols - bc_t)).astype(BF16)
    upd_all = []
    for n, r in enumerate(chunks):
        upd_all.append(jnp.dot(k_dec[:, r], vb[r, :], preferred_element_type=F32))
        if n == half - 1:
            yield
    yield
    states = []
    for n in range(len(chunks)):
        states.append(st)
        upd = cat([upd_all[n][hd * GLA_DK:(hd + 1) * GLA_DK, hd * GLA_DV:(hd + 1) * GLA_DV]
                   for hd in range(GLA_HEADS)])
        st = st * jnp.exp(b_last[n]) + upd

    qb = q.astype(BF16)
    q1 = (q * jnp.exp(la)).astype(BF16)
    kb = k_t.astype(BF16)
    att = []
    for n, r in enumerate(chunks):
        row = []
        for p, cols in enumerate(pairs):
            x = jnp.dot(cat([qb[r, cols], q1[r, cols]]), key_weights(kb, r, p), preferred_element_type=F32)
            row.append(jnp.where(lvl == 1, x[c:, :], jnp.where(lvl == 0, x[:c, :], 0.0)))
        att.append(row)
        if n == half - 1:
            yield
    yield
    for lv in range(2, N_LEVELS):
        el = cat([en[(lv - 2) * c:(lv - 1) * c, :] for en in e])
        ql = (q * jnp.exp(jnp.minimum(el, 0.0))).astype(BF16)
        kl = (k_t * jnp.exp(jnp.minimum(-el.T, 0.0))).astype(BF16)
        for n, r in enumerate(chunks):
            for p, cols in enumerate(pairs):
                x = jnp.dot(ql[r, cols], key_weights(kl, r, p), preferred_element_type=F32)
                att[n][p] = jnp.where(lvl == lv, x, att[n][p])
        yield

    q_in = (q * jnp.exp(bc)).astype(BF16)
    o = []
    for n, r in enumerate(chunks):
        sb = states[n].astype(BF16)
        row = []
        for p, cols in enumerate(pairs):
            h0 = 2 * p
            s_w = _pair_diag(sb[h0 * GLA_DK:(h0 + 1) * GLA_DK, :], sb[(h0 + 1) * GLA_DK:(h0 + 2) * GLA_DK, :])
            v_w = _pair_diag(vb[r, h0 * GLA_DV:(h0 + 1) * GLA_DV], vb[r, (h0 + 1) * GLA_DV:(h0 + 2) * GLA_DV])
            row.append(jnp.dot(q_in[r, cols], s_w, preferred_element_type=F32)
                       + jnp.dot(att[n][p].astype(BF16), v_w, preferred_element_type=F32))
        o.append(jnp.concatenate(row, axis=1))
        if n == half - 1:
            yield
    return cat(o), st


def _mixer_prompt_body(tm, h_ref, ng_ref, wa_ref, wr_ref, wgu_ref, bg_ref, wc_ref, gng_ref, cw_ref,
                       wout_ref, mst_ref, lvl_ref, o_ref, sg_ref, sc_ref, st_scr, carry_scr):
    j = pl.program_id(1)

    @pl.when(j == 0)
    def _():
        st_scr[...] = jnp.zeros_like(st_scr)
        carry_scr[...] = jnp.zeros_like(carry_scr)

    def inputs(h):
        return _mixer_inputs(h, ng_ref, wa_ref, wr_ref, wgu_ref, bg_ref, wc_ref)

    def gla(q, k, v, la, st):
        return _gla_tile(q, k, v, la, st, mst_ref[...], lvl_ref[...])

    def output(h, og, g, cb, uc, tail):
        return _mixer_output(h, og, g, _short_conv(cb, uc, tail, cw_ref), ng_ref, gng_ref, wout_ref)

    h_a = h_ref[0:tm, :]
    h_b = h_ref[tm:2 * tm, :]
    q, k, v, g_a, la, cb_a, cc, ch = _finish(inputs(h_a))
    uc_a = cc * ch
    (og_a, st), (q, k, v, g_b, la, cb_b, cc, ch) = _run_together(gla(q, k, v, la, st_scr[...]), inputs(h_b))
    uc_b = cc * ch
    out_a, (og_b, st) = _run_together(output(h_a, og_a, g_a, cb_a, uc_a, carry_scr[...]), gla(q, k, v, la, st))
    o_ref[0:tm, :] = out_a
    o_ref[tm:2 * tm, :] = _finish(output(h_b, og_b, g_b, cb_b, uc_b, uc_a[tm - 8:tm, :]))
    st_scr[...] = st
    carry_scr[...] = uc_b[tm - 8:tm, :]

    @pl.when(j == pl.num_programs(1) - 1)
    def _():
        sg_ref[0] = st
        sc_ref[0] = carry_scr[6:8, :]


def _mixer_prompt(h, batch, seq, tm, ng, wa, wr, wgu, bg, wc, gng, cw, wout):
    assert seq % (2 * tm) == 0 and tm % (2 * CHUNK) == 0
    nt = seq // (2 * tm)
    mst, lvl = _chunk_constants()
    consts = [ng, wa, wr, wgu, bg, wc, gng, cw, wout, mst, lvl]
    row_spec = pl.BlockSpec((2 * tm, D_MODEL), lambda b, j: (b * nt + j, 0))
    est = (8 * _nbytes((tm, D_MODEL), F32) + sum(_nbytes(c.shape, c.dtype) for c in consts)
           + 4 * _nbytes((tm, 3 * HV), F32) + 2 * _nbytes((tm, 3 * HK + 2 * HV), F32)
           + 8 * _nbytes((tm, D_MODEL), F32))
    return pl.pallas_call(
        functools.partial(_mixer_prompt_body, tm),
        grid=(batch, nt),
        in_specs=[row_spec] + [_resident(c.shape) for c in consts],
        out_specs=[row_spec,
                   pl.BlockSpec((1, HK, GLA_DV), lambda b, j: (b, 0, 0)),
                   pl.BlockSpec((1, CONV_K - 1, CONV_WIDTH), lambda b, j: (b, 0, 0))],
        out_shape=[jax.ShapeDtypeStruct((batch * seq, D_MODEL), F32),
                   jax.ShapeDtypeStruct((batch, HK, GLA_DV), F32),
                   jax.ShapeDtypeStruct((batch, CONV_K - 1, CONV_WIDTH), F32)],
        scratch_shapes=[pltpu.VMEM((HK, GLA_DV), F32), pltpu.VMEM((8, CONV_WIDTH), F32)],
        compiler_params=pltpu.CompilerParams(
            dimension_semantics=("arbitrary", "arbitrary"), vmem_limit_bytes=_vmem_limit(est)),
        name="mixer_prompt",
    )(h, *consts)


def _mixer_sample_body(nb, bs, h_ref, s_ref, c0_ref, ng_ref, wa_ref, wr_ref, wgu_ref, bg_ref, wc_ref,
                       gng_ref, cw_ref, wout_ref, o_ref, so_ref, co_ref,
                       q_scr, k_scr, la3_scr, v_scr, g_scr, oc_scr, og_scr):
    i = pl.program_id(0)

    @pl.when(i == 0)
    def _():
        q, k, v, g, la, cb, cc, ch = _finish(
            _mixer_inputs(h_ref[...], ng_ref, wa_ref, wr_ref, wgu_ref, bg_ref, wc_ref))
        q_scr[...] = q.astype(BF16)
        k_scr[...] = k.astype(BF16)
        la3_scr[...] = jnp.concatenate(_split3(la), axis=0)
        v_scr[...] = v
        g_scr[...] = g
        uc = cc * ch
        c0 = c0_ref[...]
        old2 = c0[:, 0:CONV_WIDTH]
        old1 = c0[:, CONV_WIDTH:2 * CONV_WIDTH]
        y = cw_ref[0:1, :] * old2 + cw_ref[1:2, :] * old1 + cw_ref[2:3, :] * uc
        oc_scr[...] = cb * y
        co_ref[...] = jnp.concatenate([old1, uc], axis=1)

    first = pl.multiple_of(i * bs, bs)
    seq_row = lax.broadcasted_iota(jnp.int32, (nb, bs * LANES), 0)
    seq_col = lax.broadcasted_iota(jnp.int32, (nb, bs * LANES), 1) // LANES + first
    pick = jnp.where(seq_row == seq_col, 1.0, 0.0).astype(BF16)
    pick3 = jnp.concatenate([pick, pick, pick], axis=0)
    decay_t = jnp.exp(lax.dot_general(la3_scr[...], pick3, TN_DIMS, preferred_element_type=F32))
    k_t = lax.dot_general(k_scr[...], pick, TN_DIMS, preferred_element_type=F32)
    q_t = lax.dot_general(q_scr[...], pick, TN_DIMS, preferred_element_type=F32)
    v_blk = v_scr[pl.ds(first, bs), :]
    o_rows = []
    for t in range(bs):
        cols = slice(t * LANES, (t + 1) * LANES)
        v_b = jnp.concatenate(
            [jnp.broadcast_to(v_blk[t:t + 1, hd * GLA_DV:(hd + 1) * GLA_DV], (GLA_DK, GLA_DV))
             for hd in range(GLA_HEADS)], axis=0)
        s_new = decay_t[:, cols] * s_ref[t] + k_t[:, cols] * v_b
        so_ref[t] = s_new
        qs = q_t[:, cols] * s_new
        o_rows.append(jnp.concatenate(
            [jnp.sum(qs[hd * GLA_DK:(hd + 1) * GLA_DK, :], axis=0, keepdims=True)
             for hd in range(GLA_HEADS)], axis=1))
    og_scr[pl.ds(first, bs), :] = jnp.concatenate(o_rows, axis=0)

    @pl.when(i == pl.num_programs(0) - 1)
    def _():
        o_ref[...] = _finish(
            _mixer_output(h_ref[...], og_scr[...], g_scr[...], oc_scr[...], ng_ref, gng_ref, wout_ref))


def _mixer_sample(h, s0, c0, bs, ng, wa, wr, wgu, bg, wc, gng, cw, wout):
    nb = h.shape[0]
    assert nb == LANES and nb % bs == 0
    consts = [ng, wa, wr, wgu, bg, wc, gng, cw, wout]
    state_spec = pl.BlockSpec((bs, HK, GLA_DV), lambda i: (i, 0, 0))
    est = (sum(_nbytes(c.shape, c.dtype) for c in consts) + 4 * _nbytes((bs, HK, GLA_DV), F32)
           + 12 * _nbytes((nb, D_MODEL), F32) + 2 * _nbytes((nb, 3 * HV), F32))
    return pl.pallas_call(
        functools.partial(_mixer_sample_body, nb, bs),
        grid=(nb // bs,),
        in_specs=[_resident(h.shape), state_spec, _resident(c0.shape)] + [_resident(c.shape) for c in consts],
        out_specs=[pl.BlockSpec((nb, D_MODEL), lambda i: (0, 0)), state_spec,
                   pl.BlockSpec(c0.shape, lambda i: (0, 0))],
        out_shape=[jax.ShapeDtypeStruct((nb, D_MODEL), F32),
                   jax.ShapeDtypeStruct(s0.shape, F32),
                   jax.ShapeDtypeStruct(c0.shape, F32)],
        scratch_shapes=[pltpu.VMEM((nb, HK), BF16), pltpu.VMEM((nb, HK), BF16), pltpu.VMEM((3 * nb, HK), BF16),
                        pltpu.VMEM((nb, HV), F32), pltpu.VMEM((nb, HV), F32), pltpu.VMEM((nb, CONV_WIDTH), F32),
                        pltpu.VMEM((nb, HV), F32)],
        compiler_params=pltpu.CompilerParams(
            dimension_semantics=("arbitrary",), vmem_limit_bytes=_vmem_limit(est)),
        name="mixer_sample",
    )(h, s0, c0, *consts)


TM_PROMPT = 512
FFN_SIZES = (768, 256)
FFN_PLE_SIZES = (256, 256, 256, 256)
BS_SAMPLE = 16


def kernel(x_prompt, x_sample, state_gla, state_conv, p_prompt, p_sample, norm_g, w_in, w_gate_up, b_gate,
           gla_norm_g, conv_w, w_out, ffn1_gate, ffn1_up, ffn1_down, ffn2_gate, ffn2_up, ffn2_down,
           w_ple_proj, w_ple_gate):
    batch, seq, _ = x_prompt.shape
    nb = x_sample.shape[0]
    assert norm_g.shape[0] == 1 and x_sample.shape[1] == 1

    ng = norm_g[0]
    f1 = (_gain_into_rows(ffn1_gate[0], ng[0]), _gain_into_rows(ffn1_up[0], ng[0]), ffn1_down[0].astype(BF16))
    wgu = jnp.pad(w_gate_up[0], ((0, LANES - GLA_RANK), (0, 0))).astype(BF16)
    bg = b_gate[0].reshape(1, HK)
    gng = gla_norm_g[0].reshape(1, GLA_DV)
    cw = conv_w[0]
    r0 = 2 * HK + 2 * HV
    r1 = r0 + GLA_RANK
    jobs = [
        _PrepJob(ffn2_gate[0], gain=4), _PrepJob(ffn2_up[0], gain=4), _PrepJob(ffn2_down[0]),
        _PrepJob(jnp.transpose(w_in[0]), gain=2, transposed=True,
                 pieces=[(0, r0, r0), (r0, GLA_RANK, LANES), (r1, w_in.shape[2] - r1, w_in.shape[2] - r1)]),
        _PrepJob(w_out[0]), _PrepJob(w_ple_gate[0], gain=6), _PrepJob(w_ple_proj[0]),
    ]

    xp = x_prompt.reshape(batch * seq, D_MODEL)
    pp = p_prompt[0].reshape(batch * seq, PLE_DIM)
    hp, f2g, f2u, f2d, wa, wr, wc, wout, wpg, wpp = _ffn(xp, ng, *f1, 1, FFN_SIZES, prep=(ng.T, jobs))
    f2 = (f2g, f2u, f2d)
    mix_w = (ng, wa, wr, wgu, bg, wc, gng, cw, wout)
    hp, sg_p, sc_p = _mixer_prompt(hp, batch, seq, TM_PROMPT, *mix_w)
    hp = _ffn(hp, ng, *f2, 5, FFN_PLE_SIZES, ple_args=(pp, wpp, wpg))

    xs = x_sample.reshape(nb, D_MODEL)
    ps = p_sample[0].reshape(nb, PLE_DIM)
    s0 = state_gla[0].reshape(nb, HK, GLA_DV)
    c0 = state_conv[0].reshape(nb, (CONV_K - 1) * CONV_WIDTH)
    hs = _ffn(xs, ng, *f1, 1, (nb,))
    hs, sg_s, sc_s = _mixer_sample(hs, s0, c0, BS_SAMPLE, *mix_w)
    hs = _ffn(hs, ng, *f2, 5, (nb,), ple_args=(ps, wpp, wpg))

    return (hp.reshape(batch, seq, D_MODEL),
            hs.reshape(nb, 1, D_MODEL),
            sg_p.reshape(1, batch, GLA_HEADS, GLA_DK, GLA_DV),
            sc_p.reshape(1, batch, CONV_K - 1, CONV_WIDTH),
            sg_s.reshape(1, nb, GLA_HEADS, GLA_DK, GLA_DV),
            sc_s.reshape(1, nb, CONV_K - 1, CONV_WIDTH))
```

```python
import functools

import numpy as np
import jax
import jax.numpy as jnp
from jax import lax
from jax.experimental import pallas as pl
from jax.experimental.pallas import tpu as pltpu

F32 = jnp.float32
BF16 = jnp.bfloat16

D_MODEL = 1024
PLE_DIM = 256
GLA_HEADS = 4
GLA_DK = 64
GLA_DV = 128
HK = GLA_HEADS * GLA_DK
HV = GLA_HEADS * GLA_DV
GLA_RANK = 16
GLA_GATE_NORM = 16.0
CONV_WIDTH = 512
CONV_K = 3
D_FF = 2816
EPS = 1e-6

LANES = 128
MXU_COLS = 256
CHUNK = 128
N_LEVELS = 8
VMEM_CAP = 56 * 1024 * 1024

NT_DIMS = (((1,), (1,)), ((), ()))
TN_DIMS = (((0,), (0,)), ((), ()))


def _inv_rms(x):
    return lax.rsqrt(jnp.mean(x * x, axis=-1, keepdims=True) + EPS)


def _rms(x, g):
    return x * _inv_rms(x) * g


def _gain_into_rows(w, g):
    return (g[:, None] * w).astype(BF16)


def _sigmoid(x):
    return 1.0 / (1.0 + jnp.exp(-x))


def _log_sigmoid(x):
    return jnp.minimum(x, 0.0) - jnp.log1p(jnp.exp(-jnp.abs(x)))


def _zero_from(x):
    bits = x.astype(jnp.int32)
    sixteen = jnp.full(x.shape, 16, jnp.int32)
    return lax.shift_right_logical(lax.shift_right_logical(bits, sixteen), sixteen).astype(F32)


def _split2(x):
    hi = x.astype(BF16)
    return hi, (x - hi.astype(F32)).astype(BF16)


def _split3(x):
    hi = x.astype(BF16)
    r1 = x - hi.astype(F32)
    mid = r1.astype(BF16)
    lo = (r1 - mid.astype(F32)).astype(BF16)
    return hi, mid, lo


def _resident(shape):
    nd = len(shape)
    return pl.BlockSpec(shape, lambda *_: (0,) * nd, pipeline_mode=pl.Buffered(1))


def _nbytes(shape, dtype):
    return int(np.prod(shape)) * jnp.dtype(dtype).itemsize


def _vmem_limit(est_bytes):
    return int(min(VMEM_CAP, max(est_bytes * 5 // 4, 16 * 1024 * 1024)))


class _PrepJob:
    def __init__(self, w, gain=None, pieces=None, transposed=False):
        self.w = w
        self.gain = gain
        self.transposed = transposed
        self.rows, self.cols = (w.shape[1], w.shape[0]) if transposed else w.shape
        self.pieces = pieces or [(0, self.cols, self.cols)]


def _rows_of_transposed(w_t, rb):
    cols = w_t.shape[0]
    whole = cols // LANES * LANES
    parts = [w_t[j:j + LANES, :].T for j in range(0, whole, LANES)]
    if cols > whole:
        parts.append(w_t[cols - LANES:cols, :].T[:, LANES - (cols - whole):])
    t = jnp.concatenate(parts, axis=1)
    which = pl.program_id(0) % (LANES // rb)
    w = t[0:rb, :]
    for j in range(1, LANES // rb):
        w = jnp.where(which == j, t[j * rb:(j + 1) * rb, :], w)
    return w


def _prep_rows(job, g_ref, w_ref, out_refs):
    w = w_ref[...]
    if job.transposed:
        w = _rows_of_transposed(w, job.rb)
    if job.gain is not None:
        w = w * g_ref[:, job.gain:job.gain + 1]
    for (c0, keep, width), out_ref in zip(job.pieces, out_refs):
        piece = w[:, c0:c0 + width]
        if keep < width:
            lane = lax.broadcasted_iota(jnp.int32, piece.shape, 1)
            piece = jnp.where(lane < keep, piece, 0.0)
        out_ref[...] = piece.astype(BF16)


def _ffn_body(post, ple, sizes, jobs, *refs):
    refs = list(refs)
    x_ref, ng_ref, wg_ref, wu_ref, wd_ref = refs[:5]
    del refs[:5]
    if ple:
        p_ref, wpp_ref, wpg_ref = refs[:3]
        del refs[:3]
    if jobs:
        g_ref = refs.pop(0)
        job_in = refs[:len(jobs)]
        del refs[:len(jobs)]
    o_ref = refs.pop(0)
    h_scr = refs.pop()
    for job, w_ref in zip(jobs, job_in if jobs else ()):
        outs = refs[:len(job.pieces)]
        del refs[:len(job.pieces)]
        _prep_rows(job, g_ref, w_ref, outs)

    starts = [sum(sizes[:s]) for s in range(len(sizes))]

    n_chunks = D_FF // MXU_COLS

    def gate_up(s, beside):
        x = x_ref[starts[s]:starts[s] + sizes[s], :]
        xb = x.astype(BF16)
        inv = _inv_rms(x)
        for c in range(n_chunks):
            sl = slice(c * MXU_COLS, (c + 1) * MXU_COLS)
            if c == n_chunks // 2 and beside is not None:
                inv = inv + _zero_from(beside()[0:sizes[s], 0:1])
            g = jnp.dot(xb, wg_ref[:, sl], preferred_element_type=F32) * inv
            u = jnp.dot(xb, wu_ref[:, sl], preferred_element_type=F32) * inv
            h_scr[s % 2, 0:sizes[s], sl] = (g * _sigmoid(g) * u).astype(BF16)

    def down(s):
        return jnp.dot(h_scr[s % 2, 0:sizes[s], :], wd_ref[...], preferred_element_type=F32)

    def epilogue(s, y):
        rows = slice(starts[s], starts[s] + sizes[s])
        h = x_ref[rows, :] + 0.5 * _rms(y, ng_ref[post:post + 1, :])
        if ple:
            gate = _sigmoid(jnp.dot(h.astype(BF16), wpg_ref[...], preferred_element_type=F32) * _inv_rms(h))
            proj = jnp.dot(p_ref[rows, :].astype(BF16), wpp_ref[...], preferred_element_type=F32)
            h = h + _rms(proj * gate, ng_ref[7:8, :])
        o_ref[rows, :] = h
        return h

    pending = None
    for s in range(len(sizes)):
        gate_up(s, pending)
        y = down(s)
        pending = functools.partial(epilogue, s, y)
    pending()


def _ffn(x, ng, wg, wu, wd, post, sizes, ple_args=None, prep=None):
    rows = x.shape[0]
    tm = sum(sizes)
    ts = max(sizes)
    assert rows % tm == 0 and all(sz % 16 == 0 for sz in sizes)
    assert all(a >= b for a, b in zip(sizes, sizes[1:]))
    steps = rows // tm
    ple = ple_args is not None
    row_spec = pl.BlockSpec((tm, D_MODEL), lambda i: (i, 0))
    in_specs = [row_spec, _resident(ng.shape), _resident(wg.shape), _resident(wu.shape), _resident(wd.shape)]
    args = [x, ng, wg, wu, wd]
    est = (4 * _nbytes((tm, D_MODEL), F32) + 3 * _nbytes(wg.shape, BF16)
           + 2 * _nbytes((ts, D_FF), BF16) + _nbytes((ts, D_MODEL), BF16)
           + 6 * _nbytes((ts, MXU_COLS), F32) + 6 * _nbytes((ts, D_MODEL), F32))
    if ple:
        p, wpp, wpg = ple_args
        in_specs += [pl.BlockSpec((tm, PLE_DIM), lambda i: (i, 0)), _resident(wpp.shape), _resident(wpg.shape)]
        args += [p, wpp, wpg]
        est += 2 * _nbytes((tm, PLE_DIM), F32) + _nbytes(wpp.shape, BF16) + _nbytes(wpg.shape, BF16)
        est += 3 * _nbytes((ts, D_MODEL), F32)
    jobs = []
    out_specs = [row_spec]
    out_shape = [jax.ShapeDtypeStruct((rows, D_MODEL), F32)]
    if prep is not None:
        gains, jobs = prep
        in_specs.append(pl.BlockSpec((gains.shape[0] // steps, gains.shape[1]), lambda i: (i, 0)))
        args.append(gains)
        for job in jobs:
            r, c = job.rows, job.cols
            rb = job.rb = r // steps
            assert r % steps == 0 and rb % 16 == 0
            assert job.gain is None or r == gains.shape[0]
            if job.transposed:
                assert LANES % rb == 0
                in_specs.append(pl.BlockSpec((c, LANES), lambda i, k=LANES // rb: (0, i // k)))
            else:
                in_specs.append(pl.BlockSpec((rb, c), lambda i: (i, 0)))
            args.append(job.w)
            est += 2 * _nbytes((rb, c), F32)
            for _, _, width in job.pieces:
                out_specs.append(pl.BlockSpec((rb, width), lambda i: (i, 0)))
                out_shape.append(jax.ShapeDtypeStruct((r, width), BF16))
                est += 2 * _nbytes((rb, width), BF16)
    res = pl.pallas_call(
        functools.partial(_ffn_body, post, ple, tuple(sizes), jobs),
        grid=(steps,),
        in_specs=in_specs,
        out_specs=out_specs,
        out_shape=out_shape,
        scratch_shapes=[pltpu.VMEM((2, ts, D_FF), BF16)],
        compiler_params=pltpu.CompilerParams(
            dimension_semantics=("arbitrary",), vmem_limit_bytes=_vmem_limit(est)),
        name="ffn_ple" if ple else "ffn",
    )(*args)
    return res if prep is not None else res[0]


def _run_together(*stages):
    results = [None] * len(stages)
    live = list(enumerate(stages))
    while live:
        still = []
        for idx, gen in live:
            try:
                next(gen)
                still.append((idx, gen))
            except StopIteration as stop:
                results[idx] = stop.value
        live = still
    return results


def _finish(stages):
    return _run_together(stages)[0]


def _mixer_inputs(h, ng_ref, wa_ref, wr_ref, wgu_ref, bg_ref, wc_ref):
    hb = h.astype(BF16)
    inv = _inv_rms(h)
    yield
    r = jnp.dot(hb, wr_ref[...], preferred_element_type=F32) * inv
    logit = jnp.dot(r.astype(BF16), wgu_ref[...], preferred_element_type=F32) + bg_ref[...]
    la = _log_sigmoid(logit) * (1.0 / GLA_GATE_NORM)
    yield
    cols = []
    for w_ref in (wa_ref, wc_ref):
        for c in range(0, w_ref.shape[1], MXU_COLS):
            cols.append(jnp.dot(hb, w_ref[:, c:c + MXU_COLS], preferred_element_type=F32) * inv)
            yield
    z = jnp.concatenate(cols, axis=1)
    q = z[:, 0:HK] * (GLA_DK ** -0.5)
    k = z[:, HK:2 * HK]
    v = z[:, 2 * HK:2 * HK + HV]
    g = z[:, 2 * HK + HV:2 * HK + 2 * HV]
    zc = z[:, 2 * HK + 2 * HV:]
    cb = zc[:, 0:CONV_WIDTH]
    cc = zc[:, CONV_WIDTH:2 * CONV_WIDTH]
    ch = zc[:, 2 * CONV_WIDTH:3 * CONV_WIDTH]
    return q, k, v, g, la, cb, cc, ch


def _short_conv(cb, uc, tail, cw_ref):
    row = lax.broadcasted_iota(jnp.int32, uc.shape, 0)
    c1 = tail[7:8, :]
    c2 = tail[6:7, :]
    prev1 = jnp.where(row == 0, c1, pltpu.roll(uc, 1, axis=0))
    prev2 = jnp.where(row == 0, c2, jnp.where(row == 1, c1, pltpu.roll(uc, 2, axis=0)))
    return cb * (cw_ref[0:1, :] * prev2 + cw_ref[1:2, :] * prev1 + cw_ref[2:3, :] * uc)


def _mixer_output(h, o, g, oc, ng_ref, gng_ref, wout_ref):
    heads = []
    for hd in range(GLA_HEADS):
        sl = slice(hd * GLA_DV, (hd + 1) * GLA_DV)
        heads.append(_rms(o[:, sl], gng_ref[...]))
    on = jnp.concatenate(heads, axis=1) * (g * _sigmoid(g))
    cat = jnp.concatenate([on.astype(BF16), oc.astype(BF16)], axis=1)
    yield
    cols = []
    for c in range(0, wout_ref.shape[1], MXU_COLS):
        cols.append(jnp.dot(cat, wout_ref[:, c:c + MXU_COLS], preferred_element_type=F32))
        yield
    return h + _rms(jnp.concatenate(cols, axis=1), ng_ref[3:4, :])


def _chunk_constants():
    t = np.arange(CHUNK)[:, None]
    s = np.arange(CHUNK)[None, :]
    blocks = []
    for j in range(1, N_LEVELS - 1):
        half = 1 << j
        mid = (t // (2 * half)) * (2 * half) + half
        blocks.append((s <= t).astype(np.float32) - (s < mid).astype(np.float32))
    blocks.append((s <= t).astype(np.float32))
    m = np.concatenate(blocks, axis=0)
    mst = np.concatenate([m, m], axis=1)
    x = np.bitwise_xor(t, s)
    top = np.floor(np.log2(np.maximum(x, 1))).astype(np.int32) + 1
    lvl = np.where(s > t, -1, np.where(s == t, 0, top)).astype(np.int32)
    return jnp.asarray(mst, BF16), jnp.asarray(np.tile(lvl, (1, 2)), jnp.int32)


def _pair_diag(a, b):
    z = jnp.zeros_like(a)
    return jnp.concatenate([jnp.concatenate([a, z], axis=1), jnp.concatenate([z, b], axis=1)], axis=0)


def _gla_tile(q, k, v, la, st, mst, lvl):
    c = CHUNK
    chunks = [slice(r, r + c) for r in range(0, q.shape[0], c)]
    pairs = [slice(p * 2 * GLA_DK, (p + 1) * 2 * GLA_DK) for p in range(GLA_HEADS // 2)]
    half = len(chunks) // 2
    cat = lambda xs: jnp.concatenate(xs, axis=0)

    def key_weights(kt, r, p):
        lo = 2 * p * GLA_DK
        return _pair_diag(kt[lo:lo + GLA_DK, r], kt[lo + GLA_DK:lo + 2 * GLA_DK, r])

    hi, lo = _split2(la)
    e = []
    for n, r in enumerate(chunks):
        e.append(jnp.dot(mst, cat([hi[r, :], lo[r, :]]), preferred_element_type=F32))
        if n == half - 1:
            yield
    yield
    bc = cat([en[(N_LEVELS - 2) * c:(N_LEVELS - 1) * c, :] for en in e])
    k_t = k.T
    bc_t = bc.T
    b_last = [bc_t[:, r.stop - 1:r.stop] for r in chunks]
    b_last_cols = jnp.concatenate([jnp.broadcast_to(b, (HK, c)) for b in b_last], axis=1)

    vb = v.astype(BF16)
    k_dec = (k_t * jnp.exp(b_last_cols - bc_t)).astype(BF16)
    upd_all = []
    for n, r in enumerate(chunks):
        upd_all.append(jnp.dot(k_dec[:, r], vb[r, :], preferred_element_type=F32))
        if n == half - 1:
            yield
    yield
    states = []
    for n in range(len(chunks)):
        states.append(st)
        upd = cat([upd_all[n][hd * GLA_DK:(hd + 1) * GLA_DK, hd * GLA_DV:(hd + 1) * GLA_DV]
                   for hd in range(GLA_HEADS)])
        st = st * jnp.exp(b_last[n]) + upd

    qb = q.astype(BF16)
    q1 = (q * jnp.exp(la)).astype(BF16)
    kb = k_t.astype(BF16)
    att = []
    for n, r in enumerate(chunks):
        row = []
        for p, cols in enumerate(pairs):
            x = jnp.dot(cat([qb[r, cols], q1[r, cols]]), key_weights(kb, r, p), preferred_element_type=F32)
            row.append(jnp.where(lvl == 1, x[c:, :], jnp.where(lvl == 0, x[:c, :], 0.0)))
        att.append(row)
        if n == half - 1:
            yield
    yield
    for lv in range(2, N_LEVELS):
        el = cat([en[(lv - 2) * c:(lv - 1) * c, :] for en in e])
        ql = (q * jnp.exp(jnp.minimum(el, 0.0))).astype(BF16)
        kl = (k_t * jnp.exp(jnp.minimum(-el.T, 0.0))).astype(BF16)
        for n, r in enumerate(chunks):
            for p, cols in enumerate(pairs):
                x = jnp.dot(ql[r, cols], key_weights(kl, r, p), preferred_element_type=F32)
                att[n][p] = jnp.where(lvl == lv, x, att[n][p])
        yield

    q_in = (q * jnp.exp(bc)).astype(BF16)
    o = []
    for n, r in enumerate(chunks):
        sb = states[n].astype(BF16)
        row = []
        for p, cols in enumerate(pairs):
            h0 = 2 * p
            s_w = _pair_diag(sb[h0 * GLA_DK:(h0 + 1) * GLA_DK, :], sb[(h0 + 1) * GLA_DK:(h0 + 2) * GLA_DK, :])
            v_w = _pair_diag(vb[r, h0 * GLA_DV:(h0 + 1) * GLA_DV], vb[r, (h0 + 1) * GLA_DV:(h0 + 2) * GLA_DV])
            row.append(jnp.dot(q_in[r, cols], s_w, preferred_element_type=F32)
                       + jnp.dot(att[n][p].astype(BF16), v_w, preferred_element_type=F32))
        o.append(jnp.concatenate(row, axis=1))
        if n == half - 1:
            yield
    return cat(o), st


def _mixer_prompt_body(tm, h_ref, ng_ref, wa_ref, wr_ref, wgu_ref, bg_ref, wc_ref, gng_ref, cw_ref,
                       wout_ref, mst_ref, lvl_ref, o_ref, sg_ref, sc_ref, st_scr, carry_scr):
    j = pl.program_id(1)

    @pl.when(j == 0)
    def _():
        st_scr[...] = jnp.zeros_like(st_scr)
        carry_scr[...] = jnp.zeros_like(carry_scr)

    def inputs(h):
        return _mixer_inputs(h, ng_ref, wa_ref, wr_ref, wgu_ref, bg_ref, wc_ref)

    def gla(q, k, v, la, st):
        return _gla_tile(q, k, v, la, st, mst_ref[...], lvl_ref[...])

    def output(h, og, g, cb, uc, tail):
        return _mixer_output(h, og, g, _short_conv(cb, uc, tail, cw_ref), ng_ref, gng_ref, wout_ref)

    h_a = h_ref[0:tm, :]
    h_b = h_ref[tm:2 * tm, :]
    q, k, v, g_a, la, cb_a, cc, ch = _finish(inputs(h_a))
    uc_a = cc * ch
    (og_a, st), (q, k, v, g_b, la, cb_b, cc, ch) = _run_together(gla(q, k, v, la, st_scr[...]), inputs(h_b))
    uc_b = cc * ch
    out_a, (og_b, st) = _run_together(output(h_a, og_a, g_a, cb_a, uc_a, carry_scr[...]), gla(q, k, v, la, st))
    o_ref[0:tm, :] = out_a
    o_ref[tm:2 * tm, :] = _finish(output(h_b, og_b, g_b, cb_b, uc_b, uc_a[tm - 8:tm, :]))
    st_scr[...] = st
    carry_scr[...] = uc_b[tm - 8:tm, :]

    @pl.when(j == pl.num_programs(1) - 1)
    def _():
        sg_ref[0] = st
        sc_ref[0] = carry_scr[6:8, :]


def _mixer_prompt(h, batch, seq, tm, ng, wa, wr, wgu, bg, wc, gng, cw, wout):
    assert seq % (2 * tm) == 0 and tm % (2 * CHUNK) == 0
    nt = seq // (2 * tm)
    mst, lvl = _chunk_constants()
    consts = [ng, wa, wr, wgu, bg, wc, gng, cw, wout, mst, lvl]
    row_spec = pl.BlockSpec((2 * tm, D_MODEL), lambda b, j: (b * nt + j, 0))
    est = (8 * _nbytes((tm, D_MODEL), F32) + sum(_nbytes(c.shape, c.dtype) for c in consts)
           + 4 * _nbytes((tm, 3 * HV), F32) + 2 * _nbytes((tm, 3 * HK + 2 * HV), F32)
           + 8 * _nbytes((tm, D_MODEL), F32))
    return pl.pallas_call(
        functools.partial(_mixer_prompt_body, tm),
        grid=(batch, nt),
        in_specs=[row_spec] + [_resident(c.shape) for c in consts],
        out_specs=[row_spec,
                   pl.BlockSpec((1, HK, GLA_DV), lambda b, j: (b, 0, 0)),
                   pl.BlockSpec((1, CONV_K - 1, CONV_WIDTH), lambda b, j: (b, 0, 0))],
        out_shape=[jax.ShapeDtypeStruct((batch * seq, D_MODEL), F32),
                   jax.ShapeDtypeStruct((batch, HK, GLA_DV), F32),
                   jax.ShapeDtypeStruct((batch, CONV_K - 1, CONV_WIDTH), F32)],
        scratch_shapes=[pltpu.VMEM((HK, GLA_DV), F32), pltpu.VMEM((8, CONV_WIDTH), F32)],
        compiler_params=pltpu.CompilerParams(
            dimension_semantics=("arbitrary", "arbitrary"), vmem_limit_bytes=_vmem_limit(est)),
        name="mixer_prompt",
    )(h, *consts)


def _mixer_sample_body(nb, bs, h_ref, s_ref, c0_ref, ng_ref, wa_ref, wr_ref, wgu_ref, bg_ref, wc_ref,
                       gng_ref, cw_ref, wout_ref, o_ref, so_ref, co_ref,
                       q_scr, k_scr, la3_scr, v_scr, g_scr, oc_scr, og_scr):
    i = pl.program_id(0)

    @pl.when(i == 0)
    def _():
        q, k, v, g, la, cb, cc, ch = _finish(
            _mixer_inputs(h_ref[...], ng_ref, wa_ref, wr_ref, wgu_ref, bg_ref, wc_ref))
        q_scr[...] = q.astype(BF16)
        k_scr[...] = k.astype(BF16)
        la3_scr[...] = jnp.concatenate(_split3(la), axis=0)
        v_scr[...] = v
        g_scr[...] = g
        uc = cc * ch
        c0 = c0_ref[...]
        old2 = c0[:, 0:CONV_WIDTH]
        old1 = c0[:, CONV_WIDTH:2 * CONV_WIDTH]
        y = cw_ref[0:1, :] * old2 + cw_ref[1:2, :] * old1 + cw_ref[2:3, :] * uc
        oc_scr[...] = cb * y
        co_ref[...] = jnp.concatenate([old1, uc], axis=1)

    first = pl.multiple_of(i * bs, bs)
    seq_row = lax.broadcasted_iota(jnp.int32, (nb, bs * LANES), 0)
    seq_col = lax.broadcasted_iota(jnp.int32, (nb, bs * LANES), 1) // LANES + first
    pick = jnp.where(seq_row == seq_col, 1.0, 0.0).astype(BF16)
    pick3 = jnp.concatenate([pick, pick, pick], axis=0)
    decay_t = jnp.exp(lax.dot_general(la3_scr[...], pick3, TN_DIMS, preferred_element_type=F32))
    k_t = lax.dot_general(k_scr[...], pick, TN_DIMS, preferred_element_type=F32)
    q_t = lax.dot_general(q_scr[...], pick, TN_DIMS, preferred_element_type=F32)
    v_blk = v_scr[pl.ds(first, bs), :]
    o_rows = []
    for t in range(bs):
        cols = slice(t * LANES, (t + 1) * LANES)
        v_b = jnp.concatenate(
            [jnp.broadcast_to(v_blk[t:t + 1, hd * GLA_DV:(hd + 1) * GLA_DV], (GLA_DK, GLA_DV))
             for hd in range(GLA_HEADS)], axis=0)
        s_new = decay_t[:, cols] * s_ref[t] + k_t[:, cols] * v_b
        so_ref[t] = s_new
        qs = q_t[:, cols] * s_new
        o_rows.append(jnp.concatenate(
            [jnp.sum(qs[hd * GLA_DK:(hd + 1) * GLA_DK, :], axis=0, keepdims=True)
             for hd in range(GLA_HEADS)], axis=1))
    og_scr[pl.ds(first, bs), :] = jnp.concatenate(o_rows, axis=0)

    @pl.when(i == pl.num_programs(0) - 1)
    def _():
        o_ref[...] = _finish(
            _mixer_output(h_ref[...], og_scr[...], g_scr[...], oc_scr[...], ng_ref, gng_ref, wout_ref))


def _mixer_sample(h, s0, c0, bs, ng, wa, wr, wgu, bg, wc, gng, cw, wout):
    nb = h.shape[0]
    assert nb == LANES and nb % bs == 0
    consts = [ng, wa, wr, wgu, bg, wc, gng, cw, wout]
    state_spec = pl.BlockSpec((bs, HK, GLA_DV), lambda i: (i, 0, 0))
    est = (sum(_nbytes(c.shape, c.dtype) for c in consts) + 4 * _nbytes((bs, HK, GLA_DV), F32)
           + 12 * _nbytes((nb, D_MODEL), F32) + 2 * _nbytes((nb, 3 * HV), F32))
    return pl.pallas_call(
        functools.partial(_mixer_sample_body, nb, bs),
        grid=(nb // bs,),
        in_specs=[_resident(h.shape), state_spec, _resident(c0.shape)] + [_resident(c.shape) for c in consts],
        out_specs=[pl.BlockSpec((nb, D_MODEL), lambda i: (0, 0)), state_spec,
                   pl.BlockSpec(c0.shape, lambda i: (0, 0))],
        out_shape=[jax.ShapeDtypeStruct((nb, D_MODEL), F32),
                   jax.ShapeDtypeStruct(s0.shape, F32),
                   jax.ShapeDtypeStruct(c0.shape, F32)],
        scratch_shapes=[pltpu.VMEM((nb, HK), BF16), pltpu.VMEM((nb, HK), BF16), pltpu.VMEM((3 * nb, HK), BF16),
                        pltpu.VMEM((nb, HV), F32), pltpu.VMEM((nb, HV), F32), pltpu.VMEM((nb, CONV_WIDTH), F32),
                        pltpu.VMEM((nb, HV), F32)],
        compiler_params=pltpu.CompilerParams(
            dimension_semantics=("arbitrary",), vmem_limit_bytes=_vmem_limit(est)),
        name="mixer_sample",
    )(h, s0, c0, *consts)


TM_PROMPT = 256
FFN_SIZES = (768, 256)
FFN_PLE_SIZES = (256, 256, 256, 256)
BS_SAMPLE = 16


def kernel(x_prompt, x_sample, state_gla, state_conv, p_prompt, p_sample, norm_g, w_in, w_gate_up, b_gate,
           gla_norm_g, conv_w, w_out, ffn1_gate, ffn1_up, ffn1_down, ffn2_gate, ffn2_up, ffn2_down,
           w_ple_proj, w_ple_gate):
    batch, seq, _ = x_prompt.shape
    nb = x_sample.shape[0]
    assert norm_g.shape[0] == 1 and x_sample.shape[1] == 1

    ng = norm_g[0]
    f1 = (_gain_into_rows(ffn1_gate[0], ng[0]), _gain_into_rows(ffn1_up[0], ng[0]), ffn1_down[0].astype(BF16))
    wgu = jnp.pad(w_gate_up[0], ((0, LANES - GLA_RANK), (0, 0))).astype(BF16)
    bg = b_gate[0].reshape(1, HK)
    gng = gla_norm_g[0].reshape(1, GLA_DV)
    cw = conv_w[0]
    r0 = 2 * HK + 2 * HV
    r1 = r0 + GLA_RANK
    jobs = [
        _PrepJob(ffn2_gate[0], gain=4), _PrepJob(ffn2_up[0], gain=4), _PrepJob(ffn2_down[0]),
        _PrepJob(jnp.transpose(w_in[0]), gain=2, transposed=True,
                 pieces=[(0, r0, r0), (r0, GLA_RANK, LANES), (r1, w_in.shape[2] - r1, w_in.shape[2] - r1)]),
        _PrepJob(w_out[0]), _PrepJob(w_ple_gate[0], gain=6), _PrepJob(w_ple_proj[0]),
    ]

    xp = x_prompt.reshape(batch * seq, D_MODEL)
    pp = p_prompt[0].reshape(batch * seq, PLE_DIM)
    hp, f2g, f2u, f2d, wa, wr, wc, wout, wpg, wpp = _ffn(xp, ng, *f1, 1, FFN_SIZES, prep=(ng.T, jobs))
    f2 = (f2g, f2u, f2d)
    mix_w = (ng, wa, wr, wgu, bg, wc, gng, cw, wout)
    hp, sg_p, sc_p = _mixer_prompt(hp, batch, seq, TM_PROMPT, *mix_w)
    hp = _ffn(hp, ng, *f2, 5, FFN_PLE_SIZES, ple_args=(pp, wpp, wpg))

    xs = x_sample.reshape(nb, D_MODEL)
    ps = p_sample[0].reshape(nb, PLE_DIM)
    s0 = state_gla[0].reshape(nb, HK, GLA_DV)
    c0 = state_conv[0].reshape(nb, (CONV_K - 1) * CONV_WIDTH)
    hs = _ffn(xs, ng, *f1, 1, (nb,))
    hs, sg_s, sc_s = _mixer_sample(hs, s0, c0, BS_SAMPLE, *mix_w)
    hs = _ffn(hs, ng, *f2, 5, (nb,), ple_args=(ps, wpp, wpg))

    return (hp.reshape(batch, seq, D_MODEL),
            hs.reshape(nb, 1, D_MODEL),
            sg_p.reshape(1, batch, GLA_HEADS, GLA_DK, GLA_DV),
            sc_p.reshape(1, batch, CONV_K - 1, CONV_WIDTH),
            sg_s.reshape(1, nb, GLA_HEADS, GLA_DK, GLA_DV),
            sc_s.reshape(1, nb, CONV_K - 1, CONV_WIDTH))
```

```python
import functools

import numpy as np
import jax
import jax.numpy as jnp
from jax import lax
from jax.experimental import pallas as pl
from jax.experimental.pallas import tpu as pltpu

F32 = jnp.float32
BF16 = jnp.bfloat16

D_MODEL = 1024
PLE_DIM = 256
GLA_HEADS = 4
GLA_DK = 64
GLA_DV = 128
HK = GLA_HEADS * GLA_DK
HV = GLA_HEADS * GLA_DV
GLA_RANK = 16
GLA_GATE_NORM = 16.0
CONV_WIDTH = 512
CONV_K = 3
D_FF = 2816
EPS = 1e-6

LANES = 128
MXU_COLS = 256
CHUNK = 128
N_LEVELS = 8
VMEM_CAP = 56 * 1024 * 1024

NT_DIMS = (((1,), (1,)), ((), ()))
TN_DIMS = (((0,), (0,)), ((), ()))


def _inv_rms(x):
    return lax.rsqrt(jnp.mean(x * x, axis=-1, keepdims=True) + EPS)


def _rms(x, g):
    return x * _inv_rms(x) * g


def _gain_into_rows(w, g):
    return (g[:, None] * w).astype(BF16)


def _sigmoid(x):
    return 1.0 / (1.0 + jnp.exp(-x))


def _log_sigmoid(x):
    return jnp.minimum(x, 0.0) - jnp.log(1.0 + jnp.exp(-jnp.abs(x)))


def _zero_from(x):
    bits = x.astype(jnp.int32)
    sixteen = jnp.full(x.shape, 16, jnp.int32)
    return lax.shift_right_logical(lax.shift_right_logical(bits, sixteen), sixteen).astype(F32)


def _split2(x):
    hi = x.astype(BF16)
    return hi, (x - hi.astype(F32)).astype(BF16)


def _split3(x):
    hi = x.astype(BF16)
    r1 = x - hi.astype(F32)
    mid = r1.astype(BF16)
    lo = (r1 - mid.astype(F32)).astype(BF16)
    return hi, mid, lo


def _resident(shape):
    nd = len(shape)
    return pl.BlockSpec(shape, lambda *_: (0,) * nd, pipeline_mode=pl.Buffered(1))


def _nbytes(shape, dtype):
    return int(np.prod(shape)) * jnp.dtype(dtype).itemsize


def _vmem_limit(est_bytes):
    return int(min(VMEM_CAP, max(est_bytes * 5 // 4, 16 * 1024 * 1024)))


class _PrepJob:
    def __init__(self, w, gain=None, pieces=None, transposed=False):
        self.w = w
        self.gain = gain
        self.transposed = transposed
        self.rows, self.cols = (w.shape[1], w.shape[0]) if transposed else w.shape
        self.pieces = pieces or [(0, self.cols, self.cols)]


def _rows_of_transposed(w_t, rb):
    cols = w_t.shape[0]
    whole = cols // LANES * LANES
    parts = [w_t[j:j + LANES, :].T for j in range(0, whole, LANES)]
    if cols > whole:
        parts.append(w_t[cols - LANES:cols, :].T[:, LANES - (cols - whole):])
    t = jnp.concatenate(parts, axis=1)
    which = pl.program_id(0) % (LANES // rb)
    w = t[0:rb, :]
    for j in range(1, LANES // rb):
        w = jnp.where(which == j, t[j * rb:(j + 1) * rb, :], w)
    return w


def _prep_rows(job, g_ref, w_ref, out_refs):
    w = w_ref[...]
    if job.transposed:
        w = _rows_of_transposed(w, job.rb)
    if job.gain is not None:
        w = w * g_ref[:, job.gain:job.gain + 1]
    for (c0, keep, width), out_ref in zip(job.pieces, out_refs):
        piece = w[:, c0:c0 + width]
        if keep < width:
            lane = lax.broadcasted_iota(jnp.int32, piece.shape, 1)
            piece = jnp.where(lane < keep, piece, 0.0)
        out_ref[...] = piece.astype(BF16)


def _ffn_body(post, ple, sizes, jobs, *refs):
    refs = list(refs)
    x_ref, ng_ref, wg_ref, wu_ref, wd_ref = refs[:5]
    del refs[:5]
    if ple:
        p_ref, wpp_ref, wpg_ref = refs[:3]
        del refs[:3]
    if jobs:
        g_ref = refs.pop(0)
        job_in = refs[:len(jobs)]
        del refs[:len(jobs)]
    o_ref = refs.pop(0)
    h_scr = refs.pop()
    for job, w_ref in zip(jobs, job_in if jobs else ()):
        outs = refs[:len(job.pieces)]
        del refs[:len(job.pieces)]
        _prep_rows(job, g_ref, w_ref, outs)

    starts = [sum(sizes[:s]) for s in range(len(sizes))]

    n_chunks = D_FF // MXU_COLS

    def gate_up(s, beside):
        x = x_ref[starts[s]:starts[s] + sizes[s], :]
        xb = x.astype(BF16)
        inv = _inv_rms(x)
        for c in range(n_chunks):
            sl = slice(c * MXU_COLS, (c + 1) * MXU_COLS)
            if c == n_chunks // 2 and beside is not None:
                inv = inv + _zero_from(beside()[0:sizes[s], 0:1])
            g = jnp.dot(xb, wg_ref[:, sl], preferred_element_type=F32) * inv
            u = jnp.dot(xb, wu_ref[:, sl], preferred_element_type=F32) * inv
            h_scr[s % 2, 0:sizes[s], sl] = (g * _sigmoid(g) * u).astype(BF16)

    def down(s):
        return jnp.dot(h_scr[s % 2, 0:sizes[s], :], wd_ref[...], preferred_element_type=F32)

    def epilogue(s, y):
        rows = slice(starts[s], starts[s] + sizes[s])
        h = x_ref[rows, :] + 0.5 * _rms(y, ng_ref[post:post + 1, :])
        if ple:
            gate = _sigmoid(jnp.dot(h.astype(BF16), wpg_ref[...], preferred_element_type=F32) * _inv_rms(h))
            proj = jnp.dot(p_ref[rows, :].astype(BF16), wpp_ref[...], preferred_element_type=F32)
            h = h + _rms(proj * gate, ng_ref[7:8, :])
        o_ref[rows, :] = h
        return h

    pending = None
    for s in range(len(sizes)):
        gate_up(s, pending)
        y = down(s)
        pending = functools.partial(epilogue, s, y)
    pending()


def _ffn(x, ng, wg, wu, wd, post, sizes, ple_args=None, prep=None):
    rows = x.shape[0]
    tm = sum(sizes)
    ts = max(sizes)
    assert rows % tm == 0 and all(sz % 16 == 0 for sz in sizes)
    assert all(a >= b for a, b in zip(sizes, sizes[1:]))
    steps = rows // tm
    ple = ple_args is not None
    row_spec = pl.BlockSpec((tm, D_MODEL), lambda i: (i, 0))
    in_specs = [row_spec, _resident(ng.shape), _resident(wg.shape), _resident(wu.shape), _resident(wd.shape)]
    args = [x, ng, wg, wu, wd]
    est = (4 * _nbytes((tm, D_MODEL), F32) + 3 * _nbytes(wg.shape, BF16)
           + 2 * _nbytes((ts, D_FF), BF16) + _nbytes((ts, D_MODEL), BF16)
           + 6 * _nbytes((ts, MXU_COLS), F32) + 6 * _nbytes((ts, D_MODEL), F32))
    if ple:
        p, wpp, wpg = ple_args
        in_specs += [pl.BlockSpec((tm, PLE_DIM), lambda i: (i, 0)), _resident(wpp.shape), _resident(wpg.shape)]
        args += [p, wpp, wpg]
        est += 2 * _nbytes((tm, PLE_DIM), F32) + _nbytes(wpp.shape, BF16) + _nbytes(wpg.shape, BF16)
        est += 3 * _nbytes((ts, D_MODEL), F32)
    jobs = []
    out_specs = [row_spec]
    out_shape = [jax.ShapeDtypeStruct((rows, D_MODEL), F32)]
    if prep is not None:
        gains, jobs = prep
        in_specs.append(pl.BlockSpec((gains.shape[0] // steps, gains.shape[1]), lambda i: (i, 0)))
        args.append(gains)
        for job in jobs:
            r, c = job.rows, job.cols
            rb = job.rb = r // steps
            assert r % steps == 0 and rb % 16 == 0
            assert job.gain is None or r == gains.shape[0]
            if job.transposed:
                assert LANES % rb == 0
                in_specs.append(pl.BlockSpec((c, LANES), lambda i, k=LANES // rb: (0, i // k)))
            else:
                in_specs.append(pl.BlockSpec((rb, c), lambda i: (i, 0)))
            args.append(job.w)
            est += 2 * _nbytes((rb, c), F32)
            for _, _, width in job.pieces:
                out_specs.append(pl.BlockSpec((rb, width), lambda i: (i, 0)))
                out_shape.append(jax.ShapeDtypeStruct((r, width), BF16))
                est += 2 * _nbytes((rb, width), BF16)
    res = pl.pallas_call(
        functools.partial(_ffn_body, post, ple, tuple(sizes), jobs),
        grid=(steps,),
        in_specs=in_specs,
        out_specs=out_specs,
        out_shape=out_shape,
        scratch_shapes=[pltpu.VMEM((2, ts, D_FF), BF16)],
        compiler_params=pltpu.CompilerParams(
            dimension_semantics=("arbitrary",), vmem_limit_bytes=_vmem_limit(est)),
        name="ffn_ple" if ple else "ffn",
    )(*args)
    return res if prep is not None else res[0]


def _run_together(*stages):
    results = [None] * len(stages)
    live = list(enumerate(stages))
    while live:
        still = []
        for idx, gen in live:
            try:
                next(gen)
                still.append((idx, gen))
            except StopIteration as stop:
                results[idx] = stop.value
        live = still
    return results


def _finish(stages):
    return _run_together(stages)[0]


def _mixer_inputs(h, ng_ref, wa_ref, wr_ref, wgu_ref, bg_ref, wc_ref):
    hb = h.astype(BF16)
    inv = _inv_rms(h)
    yield
    r = jnp.dot(hb, wr_ref[...], preferred_element_type=F32) * inv
    logit = jnp.dot(r.astype(BF16), wgu_ref[...], preferred_element_type=F32) + bg_ref[...]
    la = _log_sigmoid(logit) * (1.0 / GLA_GATE_NORM)
    yield
    cols = []
    for w_ref in (wa_ref, wc_ref):
        for c in range(0, w_ref.shape[1], MXU_COLS):
            cols.append(jnp.dot(hb, w_ref[:, c:c + MXU_COLS], preferred_element_type=F32) * inv)
            yield
    z = jnp.concatenate(cols, axis=1)
    q = z[:, 0:HK] * (GLA_DK ** -0.5)
    k = z[:, HK:2 * HK]
    v = z[:, 2 * HK:2 * HK + HV]
    g = z[:, 2 * HK + HV:2 * HK + 2 * HV]
    zc = z[:, 2 * HK + 2 * HV:]
    cb = zc[:, 0:CONV_WIDTH]
    cc = zc[:, CONV_WIDTH:2 * CONV_WIDTH]
    ch = zc[:, 2 * CONV_WIDTH:3 * CONV_WIDTH]
    return q, k, v, g, la, cb, cc, ch


def _short_conv(cb, uc, tail, cw_ref):
    row = lax.broadcasted_iota(jnp.int32, uc.shape, 0)
    c1 = tail[7:8, :]
    c2 = tail[6:7, :]
    prev1 = jnp.where(row == 0, c1, pltpu.roll(uc, 1, axis=0))
    prev2 = jnp.where(row == 0, c2, jnp.where(row == 1, c1, pltpu.roll(uc, 2, axis=0)))
    return cb * (cw_ref[0:1, :] * prev2 + cw_ref[1:2, :] * prev1 + cw_ref[2:3, :] * uc)


def _mixer_output(h, o, g, oc, ng_ref, gng_ref, wout_ref):
    heads = []
    for hd in range(GLA_HEADS):
        sl = slice(hd * GLA_DV, (hd + 1) * GLA_DV)
        heads.append(_rms(o[:, sl], gng_ref[...]))
    on = jnp.concatenate(heads, axis=1) * (g * _sigmoid(g))
    cat = jnp.concatenate([on.astype(BF16), oc.astype(BF16)], axis=1)
    yield
    cols = []
    for c in range(0, wout_ref.shape[1], MXU_COLS):
        cols.append(jnp.dot(cat, wout_ref[:, c:c + MXU_COLS], preferred_element_type=F32))
        yield
    return h + _rms(jnp.concatenate(cols, axis=1), ng_ref[3:4, :])


def _chunk_constants():
    t = np.arange(CHUNK)[:, None]
    s = np.arange(CHUNK)[None, :]
    blocks = []
    for j in range(1, N_LEVELS - 1):
        half = 1 << j
        mid = (t // (2 * half)) * (2 * half) + half
        blocks.append((s <= t).astype(np.float32) - (s < mid).astype(np.float32))
    blocks.append((s <= t).astype(np.float32))
    m = np.concatenate(blocks, axis=0)
    mst = np.concatenate([m, m], axis=1)
    x = np.bitwise_xor(t, s)
    top = np.floor(np.log2(np.maximum(x, 1))).astype(np.int32) + 1
    lvl = np.where(s > t, -1, np.where(s == t, 0, top)).astype(np.int32)
    return jnp.asarray(mst, BF16), jnp.asarray(np.tile(lvl, (1, 2)), jnp.int32)


def _pair_diag(a, b):
    z = jnp.zeros_like(a)
    return jnp.concatenate([jnp.concatenate([a, z], axis=1), jnp.concatenate([z, b], axis=1)], axis=0)


def _gla_tile(q, k, v, la, st, mst, lvl):
    c = CHUNK
    chunks = [slice(r, r + c) for r in range(0, q.shape[0], c)]
    pairs = [slice(p * 2 * GLA_DK, (p + 1) * 2 * GLA_DK) for p in range(GLA_HEADS // 2)]
    half = len(chunks) // 2
    cat = lambda xs: jnp.concatenate(xs, axis=0)

    def key_weights(kt, r, p):
        lo = 2 * p * GLA_DK
        return _pair_diag(kt[lo:lo + GLA_DK, r], kt[lo + GLA_DK:lo + 2 * GLA_DK, r])

    hi, lo = _split2(la)
    e = []
    for n, r in enumerate(chunks):
        e.append(jnp.dot(mst, cat([hi[r, :], lo[r, :]]), preferred_element_type=F32))
        if n == half - 1:
            yield
    yield
    bc = cat([en[(N_LEVELS - 2) * c:(N_LEVELS - 1) * c, :] for en in e])
    k_t = k.T
    bc_t = bc.T
    b_last = [bc_t[:, r.stop - 1:r.stop] for r in chunks]
    b_last_cols = jnp.concatenate([jnp.broadcast_to(b, (HK, c)) for b in b_last], axis=1)

    vb = v.astype(BF16)
    k_dec = (k_t * jnp.exp(b_last_cols - bc_t)).astype(BF16)
    upd_all = []
    for n, r in enumerate(chunks):
        upd_all.append(jnp.dot(k_dec[:, r], vb[r, :], preferred_element_type=F32))
        if n == half - 1:
            yield
    yield
    states = []
    for n in range(len(chunks)):
        states.append(st)
        upd = cat([upd_all[n][hd * GLA_DK:(hd + 1) * GLA_DK, hd * GLA_DV:(hd + 1) * GLA_DV]
                   for hd in range(GLA_HEADS)])
        st = st * jnp.exp(b_last[n]) + upd

    qb = q.astype(BF16)
    q1 = (q * jnp.exp(la)).astype(BF16)
    kb = k_t.astype(BF16)
    att = []
    for n, r in enumerate(chunks):
        row = []
        for p, cols in enumerate(pairs):
            x = jnp.dot(cat([qb[r, cols], q1[r, cols]]), key_weights(kb, r, p), preferred_element_type=F32)
            row.append(jnp.where(lvl == 1, x[c:, :], jnp.where(lvl == 0, x[:c, :], 0.0)))
        att.append(row)
        if n == half - 1:
            yield
    yield
    for lv in range(2, N_LEVELS):
        el = cat([en[(lv - 2) * c:(lv - 1) * c, :] for en in e])
        ql = (q * jnp.exp(jnp.minimum(el, 0.0))).astype(BF16)
        kl = (k_t * jnp.exp(jnp.minimum(-el.T, 0.0))).astype(BF16)
        for n, r in enumerate(chunks):
            for p, cols in enumerate(pairs):
                x = jnp.dot(ql[r, cols], key_weights(kl, r, p), preferred_element_type=F32)
                att[n][p] = jnp.where(lvl == lv, x, att[n][p])
        yield

    q_in = (q * jnp.exp(bc)).astype(BF16)
    o = []
    for n, r in enumerate(chunks):
        sb = states[n].astype(BF16)
        row = []
        for p, cols in enumerate(pairs):
            h0 = 2 * p
            s_w = _pair_diag(sb[h0 * GLA_DK:(h0 + 1) * GLA_DK, :], sb[(h0 + 1) * GLA_DK:(h0 + 2) * GLA_DK, :])
            v_w = _pair_diag(vb[r, h0 * GLA_DV:(h0 + 1) * GLA_DV], vb[r, (h0 + 1) * GLA_DV:(h0 + 2) * GLA_DV])
            row.append(jnp.dot(q_in[r, cols], s_w, preferred_element_type=F32)
                       + jnp.dot(att[n][p].astype(BF16), v_w, preferred_element_type=F32))
        o.append(jnp.concatenate(row, axis=1))
        if n == half - 1:
            yield
    return cat(o), st


def _mixer_prompt_body(tm, h_ref, ng_ref, wa_ref, wr_ref, wgu_ref, bg_ref, wc_ref, gng_ref, cw_ref,
                       wout_ref, mst_ref, lvl_ref, o_ref, sg_ref, sc_ref, st_scr, carry_scr):
    j = pl.program_id(1)

    @pl.when(j == 0)
    def _():
        st_scr[...] = jnp.zeros_like(st_scr)
        carry_scr[...] = jnp.zeros_like(carry_scr)

    def inputs(h):
        return _mixer_inputs(h, ng_ref, wa_ref, wr_ref, wgu_ref, bg_ref, wc_ref)

    def gla(q, k, v, la, st):
        return _gla_tile(q, k, v, la, st, mst_ref[...], lvl_ref[...])

    def output(h, og, g, cb, uc, tail):
        return _mixer_output(h, og, g, _short_conv(cb, uc, tail, cw_ref), ng_ref, gng_ref, wout_ref)

    h_a = h_ref[0:tm, :]
    h_b = h_ref[tm:2 * tm, :]
    q, k, v, g_a, la, cb_a, cc, ch = _finish(inputs(h_a))
    uc_a = cc * ch
    (og_a, st), (q, k, v, g_b, la, cb_b, cc, ch) = _run_together(gla(q, k, v, la, st_scr[...]), inputs(h_b))
    uc_b = cc * ch
    out_a, (og_b, st) = _run_together(output(h_a, og_a, g_a, cb_a, uc_a, carry_scr[...]), gla(q, k, v, la, st))
    o_ref[0:tm, :] = out_a
    o_ref[tm:2 * tm, :] = _finish(output(h_b, og_b, g_b, cb_b, uc_b, uc_a[tm - 8:tm, :]))
    st_scr[...] = st
    carry_scr[...] = uc_b[tm - 8:tm, :]

    @pl.when(j == pl.num_programs(1) - 1)
    def _():
        sg_ref[0] = st
        sc_ref[0] = carry_scr[6:8, :]


def _mixer_prompt(h, batch, seq, tm, ng, wa, wr, wgu, bg, wc, gng, cw, wout):
    assert seq % (2 * tm) == 0 and tm % (2 * CHUNK) == 0
    nt = seq // (2 * tm)
    mst, lvl = _chunk_constants()
    consts = [ng, wa, wr, wgu, bg, wc, gng, cw, wout, mst, lvl]
    row_spec = pl.BlockSpec((2 * tm, D_MODEL), lambda b, j: (b * nt + j, 0))
    est = (8 * _nbytes((tm, D_MODEL), F32) + sum(_nbytes(c.shape, c.dtype) for c in consts)
           + 4 * _nbytes((tm, 3 * HV), F32) + 2 * _nbytes((tm, 3 * HK + 2 * HV), F32)
           + 8 * _nbytes((tm, D_MODEL), F32))
    return pl.pallas_call(
        functools.partial(_mixer_prompt_body, tm),
        grid=(batch, nt),
        in_specs=[row_spec] + [_resident(c.shape) for c in consts],
        out_specs=[row_spec,
                   pl.BlockSpec((1, HK, GLA_DV), lambda b, j: (b, 0, 0)),
                   pl.BlockSpec((1, CONV_K - 1, CONV_WIDTH), lambda b, j: (b, 0, 0))],
        out_shape=[jax.ShapeDtypeStruct((batch * seq, D_MODEL), F32),
                   jax.ShapeDtypeStruct((batch, HK, GLA_DV), F32),
                   jax.ShapeDtypeStruct((batch, CONV_K - 1, CONV_WIDTH), F32)],
        scratch_shapes=[pltpu.VMEM((HK, GLA_DV), F32), pltpu.VMEM((8, CONV_WIDTH), F32)],
        compiler_params=pltpu.CompilerParams(
            dimension_semantics=("arbitrary", "arbitrary"), vmem_limit_bytes=_vmem_limit(est)),
        name="mixer_prompt",
    )(h, *consts)


def _mixer_sample_body(nb, bs, h_ref, s_ref, c0_ref, ng_ref, wa_ref, wr_ref, wgu_ref, bg_ref, wc_ref,
                       gng_ref, cw_ref, wout_ref, o_ref, so_ref, co_ref,
                       q_scr, k_scr, la3_scr, v_scr, g_scr, oc_scr, og_scr):
    i = pl.program_id(0)

    @pl.when(i == 0)
    def _():
        q, k, v, g, la, cb, cc, ch = _finish(
            _mixer_inputs(h_ref[...], ng_ref, wa_ref, wr_ref, wgu_ref, bg_ref, wc_ref))
        q_scr[...] = q.astype(BF16)
        k_scr[...] = k.astype(BF16)
        la3_scr[...] = jnp.concatenate(_split3(la), axis=0)
        v_scr[...] = v
        g_scr[...] = g
        uc = cc * ch
        c0 = c0_ref[...]
        old2 = c0[:, 0:CONV_WIDTH]
        old1 = c0[:, CONV_WIDTH:2 * CONV_WIDTH]
        y = cw_ref[0:1, :] * old2 + cw_ref[1:2, :] * old1 + cw_ref[2:3, :] * uc
        oc_scr[...] = cb * y
        co_ref[...] = jnp.concatenate([old1, uc], axis=1)

    first = pl.multiple_of(i * bs, bs)
    seq_row = lax.broadcasted_iota(jnp.int32, (nb, bs * LANES), 0)
    seq_col = lax.broadcasted_iota(jnp.int32, (nb, bs * LANES), 1) // LANES + first
    pick = jnp.where(seq_row == seq_col, 1.0, 0.0).astype(BF16)
    pick3 = jnp.concatenate([pick, pick, pick], axis=0)
    decay_t = jnp.exp(lax.dot_general(la3_scr[...], pick3, TN_DIMS, preferred_element_type=F32))
    k_t = lax.dot_general(k_scr[...], pick, TN_DIMS, preferred_element_type=F32)
    q_t = lax.dot_general(q_scr[...], pick, TN_DIMS, preferred_element_type=F32)
    v_blk = v_scr[pl.ds(first, bs), :]
    o_rows = []
    for t in range(bs):
        cols = slice(t * LANES, (t + 1) * LANES)
        v_b = jnp.concatenate(
            [jnp.broadcast_to(v_blk[t:t + 1, hd * GLA_DV:(hd + 1) * GLA_DV], (GLA_DK, GLA_DV))
             for hd in range(GLA_HEADS)], axis=0)
        s_new = decay_t[:, cols] * s_ref[t] + k_t[:, cols] * v_b
        so_ref[t] = s_new
        qs = q_t[:, cols] * s_new
        o_rows.append(jnp.concatenate(
            [jnp.sum(qs[hd * GLA_DK:(hd + 1) * GLA_DK, :], axis=0, keepdims=True)
             for hd in range(GLA_HEADS)], axis=1))
    og_scr[pl.ds(first, bs), :] = jnp.concatenate(o_rows, axis=0)

    @pl.when(i == pl.num_programs(0) - 1)
    def _():
        o_ref[...] = _finish(
            _mixer_output(h_ref[...], og_scr[...], g_scr[...], oc_scr[...], ng_ref, gng_ref, wout_ref))


def _mixer_sample(h, s0, c0, bs, ng, wa, wr, wgu, bg, wc, gng, cw, wout):
    nb = h.shape[0]
    assert nb == LANES and nb % bs == 0
    consts = [ng, wa, wr, wgu, bg, wc, gng, cw, wout]
    state_spec = pl.BlockSpec((bs, HK, GLA_DV), lambda i: (i, 0, 0))
    est = (sum(_nbytes(c.shape, c.dtype) for c in consts) + 4 * _nbytes((bs, HK, GLA_DV), F32)
           + 12 * _nbytes((nb, D_MODEL), F32) + 2 * _nbytes((nb, 3 * HV), F32))
    return pl.pallas_call(
        functools.partial(_mixer_sample_body, nb, bs),
        grid=(nb // bs,),
        in_specs=[_resident(h.shape), state_spec, _resident(c0.shape)] + [_resident(c.shape) for c in consts],
        out_specs=[pl.BlockSpec((nb, D_MODEL), lambda i: (0, 0)), state_spec,
                   pl.BlockSpec(c0.shape, lambda i: (0, 0))],
        out_shape=[jax.ShapeDtypeStruct((nb, D_MODEL), F32),
                   jax.ShapeDtypeStruct(s0.shape, F32),
                   jax.ShapeDtypeStruct(c0.shape, F32)],
        scratch_shapes=[pltpu.VMEM((nb, HK), BF16), pltpu.VMEM((nb, HK), BF16), pltpu.VMEM((3 * nb, HK), BF16),
                        pltpu.VMEM((nb, HV), F32), pltpu.VMEM((nb, HV), F32), pltpu.VMEM((nb, CONV_WIDTH), F32),
                        pltpu.VMEM((nb, HV), F32)],
        compiler_params=pltpu.CompilerParams(
            dimension_semantics=("arbitrary",), vmem_limit_bytes=_vmem_limit(est)),
        name="mixer_sample",
    )(h, s0, c0, *consts)


TM_PROMPT = 512
FFN_SIZES = (768, 256)
FFN_PLE_SIZES = (256, 256, 256, 256)
BS_SAMPLE = 16


def kernel(x_prompt, x_sample, state_gla, state_conv, p_prompt, p_sample, norm_g, w_in, w_gate_up, b_gate,
           gla_norm_g, conv_w, w_out, ffn1_gate, ffn1_up, ffn1_down, ffn2_gate, ffn2_up, ffn2_down,
           w_ple_proj, w_ple_gate):
    batch, seq, _ = x_prompt.shape
    nb = x_sample.shape[0]
    assert norm_g.shape[0] == 1 and x_sample.shape[1] == 1

    ng = norm_g[0]
    f1 = (_gain_into_rows(ffn1_gate[0], ng[0]), _gain_into_rows(ffn1_up[0], ng[0]), ffn1_down[0].astype(BF16))
    wgu = jnp.pad(w_gate_up[0], ((0, LANES - GLA_RANK), (0, 0))).astype(BF16)
    bg = b_gate[0].reshape(1, HK)
    gng = gla_norm_g[0].reshape(1, GLA_DV)
    cw = conv_w[0]
    r0 = 2 * HK + 2 * HV
    r1 = r0 + GLA_RANK
    jobs = [
        _PrepJob(ffn2_gate[0], gain=4), _PrepJob(ffn2_up[0], gain=4), _PrepJob(ffn2_down[0]),
        _PrepJob(jnp.transpose(w_in[0]), gain=2, transposed=True,
                 pieces=[(0, r0, r0), (r0, GLA_RANK, LANES), (r1, w_in.shape[2] - r1, w_in.shape[2] - r1)]),
        _PrepJob(w_out[0]), _PrepJob(w_ple_gate[0], gain=6), _PrepJob(w_ple_proj[0]),
    ]

    xp = x_prompt.reshape(batch * seq, D_MODEL)
    pp = p_prompt[0].reshape(batch * seq, PLE_DIM)
    hp, f2g, f2u, f2d, wa, wr, wc, wout, wpg, wpp = _ffn(xp, ng, *f1, 1, FFN_SIZES, prep=(ng.T, jobs))
    f2 = (f2g, f2u, f2d)
    mix_w = (ng, wa, wr, wgu, bg, wc, gng, cw, wout)
    hp, sg_p, sc_p = _mixer_prompt(hp, batch, seq, TM_PROMPT, *mix_w)
    hp = _ffn(hp, ng, *f2, 5, FFN_PLE_SIZES, ple_args=(pp, wpp, wpg))

    xs = x_sample.reshape(nb, D_MODEL)
    ps = p_sample[0].reshape(nb, PLE_DIM)
    s0 = state_gla[0].reshape(nb, HK, GLA_DV)
    c0 = state_conv[0].reshape(nb, (CONV_K - 1) * CONV_WIDTH)
    hs = _ffn(xs, ng, *f1, 1, (nb,))
    hs, sg_s, sc_s = _mixer_sample(hs, s0, c0, BS_SAMPLE, *mix_w)
    hs = _ffn(hs, ng, *f2, 5, (nb,), ple_args=(ps, wpp, wpg))

    return (hp.reshape(batch, seq, D_MODEL),
            hs.reshape(nb, 1, D_MODEL),
            sg_p.reshape(1, batch, GLA_HEADS, GLA_DK, GLA_DV),
            sc_p.reshape(1, batch, CONV_K - 1, CONV_WIDTH),
            sg_s.reshape(1, nb, GLA_HEADS, GLA_DK, GLA_DV),
            sc_s.reshape(1, nb, CONV_K - 1, CONV_WIDTH))
```

```python
import functools

import numpy as np
import jax
import jax.numpy as jnp
from jax import lax
from jax.experimental import pallas as pl
from jax.experimental.pallas import tpu as pltpu

F32 = jnp.float32
BF16 = jnp.bfloat16

D_MODEL = 1024
PLE_DIM = 256
GLA_HEADS = 4
GLA_DK = 64
GLA_DV = 128
HK = GLA_HEADS * GLA_DK
HV = GLA_HEADS * GLA_DV
GLA_RANK = 16
GLA_GATE_NORM = 16.0
CONV_WIDTH = 512
CONV_K = 3
D_FF = 2816
EPS = 1e-6

LANES = 128
MXU_COLS = 256
CHUNK = 128
N_LEVELS = 8
VMEM_CAP = 56 * 1024 * 1024

NT_DIMS = (((1,), (1,)), ((), ()))
TN_DIMS = (((0,), (0,)), ((), ()))


def _inv_rms(x):
    return lax.rsqrt(jnp.mean(x * x, axis=-1, keepdims=True) + EPS)


def _rms(x, g):
    return x * _inv_rms(x) * g


def _gain_into_rows(w, g):
    return (g[:, None] * w).astype(BF16)


def _sigmoid(x):
    return 1.0 / (1.0 + jnp.exp(-x))


def _log_sigmoid(x):
    return jnp.minimum(x, 0.0) - jnp.log(1.0 + jnp.exp(-jnp.abs(x)))


def _zero_from(x):
    bits = x.astype(jnp.int32)
    sixteen = jnp.full(x.shape, 16, jnp.int32)
    return lax.shift_right_logical(lax.shift_right_logical(bits, sixteen), sixteen).astype(F32)


def _split2(x):
    hi = x.astype(BF16)
    return hi, (x - hi.astype(F32)).astype(BF16)


def _split3(x):
    hi = x.astype(BF16)
    r1 = x - hi.astype(F32)
    mid = r1.astype(BF16)
    lo = (r1 - mid.astype(F32)).astype(BF16)
    return hi, mid, lo


def _resident(shape):
    nd = len(shape)
    return pl.BlockSpec(shape, lambda *_: (0,) * nd, pipeline_mode=pl.Buffered(1))


def _nbytes(shape, dtype):
    return int(np.prod(shape)) * jnp.dtype(dtype).itemsize


def _vmem_limit(est_bytes):
    return int(min(VMEM_CAP, max(est_bytes * 5 // 4, 16 * 1024 * 1024)))


class _PrepJob:
    def __init__(self, w, gain=None, pieces=None, transposed=False):
        self.w = w
        self.gain = gain
        self.transposed = transposed
        self.rows, self.cols = (w.shape[1], w.shape[0]) if transposed else w.shape
        self.pieces = pieces or [(0, self.cols, self.cols)]


def _rows_of_transposed(w_t, rb):
    cols = w_t.shape[0]
    whole = cols // LANES * LANES
    parts = [w_t[j:j + LANES, :].T for j in range(0, whole, LANES)]
    if cols > whole:
        parts.append(w_t[cols - LANES:cols, :].T[:, LANES - (cols - whole):])
    t = jnp.concatenate(parts, axis=1)
    which = pl.program_id(0) % (LANES // rb)
    w = t[0:rb, :]
    for j in range(1, LANES // rb):
        w = jnp.where(which == j, t[j * rb:(j + 1) * rb, :], w)
    return w


def _prep_rows(job, g_ref, w_ref, out_refs):
    w = w_ref[...]
    if job.transposed:
        w = _rows_of_transposed(w, job.rb)
    if job.gain is not None:
        w = w * g_ref[:, job.gain:job.gain + 1]
    for (c0, keep, width), out_ref in zip(job.pieces, out_refs):
        piece = w[:, c0:c0 + width]
        if keep < width:
            lane = lax.broadcasted_iota(jnp.int32, piece.shape, 1)
            piece = jnp.where(lane < keep, piece, 0.0)
        out_ref[...] = piece.astype(BF16)


def _ffn_body(post, ple, sizes, jobs, *refs):
    refs = list(refs)
    x_ref, ng_ref, wg_ref, wu_ref, wd_ref = refs[:5]
    del refs[:5]
    if ple:
        p_ref, wpp_ref, wpg_ref = refs[:3]
        del refs[:3]
    if jobs:
        g_ref = refs.pop(0)
        job_in = refs[:len(jobs)]
        del refs[:len(jobs)]
    o_ref = refs.pop(0)
    h_scr = refs.pop()
    for job, w_ref in zip(jobs, job_in if jobs else ()):
        outs = refs[:len(job.pieces)]
        del refs[:len(job.pieces)]
        _prep_rows(job, g_ref, w_ref, outs)

    starts = [sum(sizes[:s]) for s in range(len(sizes))]

    n_chunks = D_FF // MXU_COLS

    def gate_up(s, beside):
        x = x_ref[starts[s]:starts[s] + sizes[s], :]
        xb = x.astype(BF16)
        inv = _inv_rms(x)
        for c in range(n_chunks):
            sl = slice(c * MXU_COLS, (c + 1) * MXU_COLS)
            if c == n_chunks // 2 and beside is not None:
                inv = inv + _zero_from(beside()[0:sizes[s], 0:1])
            g = jnp.dot(xb, wg_ref[:, sl], preferred_element_type=F32) * inv
            u = jnp.dot(xb, wu_ref[:, sl], preferred_element_type=F32) * inv
            h_scr[s % 2, 0:sizes[s], sl] = (g * _sigmoid(g) * u).astype(BF16)

    def down(s):
        return jnp.dot(h_scr[s % 2, 0:sizes[s], :], wd_ref[...], preferred_element_type=F32)

    def epilogue(s, y):
        rows = slice(starts[s], starts[s] + sizes[s])
        h = x_ref[rows, :] + 0.5 * _rms(y, ng_ref[post:post + 1, :])
        if ple:
            gate = _sigmoid(jnp.dot(h.astype(BF16), wpg_ref[...], preferred_element_type=F32) * _inv_rms(h))
            proj = jnp.dot(p_ref[rows, :].astype(BF16), wpp_ref[...], preferred_element_type=F32)
            h = h + _rms(proj * gate, ng_ref[7:8, :])
        o_ref[rows, :] = h
        return h

    pending = None
    for s in range(len(sizes)):
        gate_up(s, pending)
        y = down(s)
        pending = functools.partial(epilogue, s, y)
    pending()


def _ffn(x, ng, wg, wu, wd, post, sizes, ple_args=None, prep=None):
    rows = x.shape[0]
    tm = sum(sizes)
    ts = max(sizes)
    assert rows % tm == 0 and all(sz % 16 == 0 for sz in sizes)
    assert all(a >= b for a, b in zip(sizes, sizes[1:]))
    steps = rows // tm
    ple = ple_args is not None
    row_spec = pl.BlockSpec((tm, D_MODEL), lambda i: (i, 0))
    in_specs = [row_spec, _resident(ng.shape), _resident(wg.shape), _resident(wu.shape), _resident(wd.shape)]
    args = [x, ng, wg, wu, wd]
    est = (4 * _nbytes((tm, D_MODEL), F32) + 3 * _nbytes(wg.shape, BF16)
           + 2 * _nbytes((ts, D_FF), BF16) + _nbytes((ts, D_MODEL), BF16)
           + 6 * _nbytes((ts, MXU_COLS), F32) + 6 * _nbytes((ts, D_MODEL), F32))
    if ple:
        p, wpp, wpg = ple_args
        in_specs += [pl.BlockSpec((tm, PLE_DIM), lambda i: (i, 0)), _resident(wpp.shape), _resident(wpg.shape)]
        args += [p, wpp, wpg]
        est += 2 * _nbytes((tm, PLE_DIM), F32) + _nbytes(wpp.shape, BF16) + _nbytes(wpg.shape, BF16)
        est += 3 * _nbytes((ts, D_MODEL), F32)
    jobs = []
    out_specs = [row_spec]
    out_shape = [jax.ShapeDtypeStruct((rows, D_MODEL), F32)]
    if prep is not None:
        gains, jobs = prep
        in_specs.append(pl.BlockSpec((gains.shape[0] // steps, gains.shape[1]), lambda i: (i, 0)))
        args.append(gains)
        for job in jobs:
            r, c = job.rows, job.cols
            rb = job.rb = r // steps
            assert r % steps == 0 and rb % 16 == 0
            assert job.gain is None or r == gains.shape[0]
            if job.transposed:
                assert LANES % rb == 0
                in_specs.append(pl.BlockSpec((c, LANES), lambda i, k=LANES // rb: (0, i // k)))
            else:
                in_specs.append(pl.BlockSpec((rb, c), lambda i: (i, 0)))
            args.append(job.w)
            est += 2 * _nbytes((rb, c), F32)
            for _, _, width in job.pieces:
                out_specs.append(pl.BlockSpec((rb, width), lambda i: (i, 0)))
                out_shape.append(jax.ShapeDtypeStruct((r, width), BF16))
                est += 2 * _nbytes((rb, width), BF16)
    res = pl.pallas_call(
        functools.partial(_ffn_body, post, ple, tuple(sizes), jobs),
        grid=(steps,),
        in_specs=in_specs,
        out_specs=out_specs,
        out_shape=out_shape,
        scratch_shapes=[pltpu.VMEM((2, ts, D_FF), BF16)],
        compiler_params=pltpu.CompilerParams(
            dimension_semantics=("arbitrary",), vmem_limit_bytes=_vmem_limit(est)),
        name="ffn_ple" if ple else "ffn",
    )(*args)
    return res if prep is not None else res[0]


def _run_together(*stages):
    results = [None] * len(stages)
    live = list(enumerate(stages))
    while live:
        still = []
        for idx, gen in live:
            try:
                next(gen)
                still.append((idx, gen))
            except StopIteration as stop:
                results[idx] = stop.value
        live = still
    return results


def _finish(stages):
    return _run_together(stages)[0]


def _mixer_inputs(h, ng_ref, wa_ref, wr_ref, wgu_ref, bg_ref, wc_ref):
    hb = h.astype(BF16)
    inv = _inv_rms(h)
    yield
    r = jnp.dot(hb, wr_ref[...], preferred_element_type=F32) * inv
    logit = jnp.dot(r.astype(BF16), wgu_ref[...], preferred_element_type=F32) + bg_ref[...]
    la = _log_sigmoid(logit) * (1.0 / GLA_GATE_NORM)
    yield
    cols = []
    for w_ref in (wa_ref, wc_ref):
        for c in range(0, w_ref.shape[1], MXU_COLS):
            cols.append(jnp.dot(hb, w_ref[:, c:c + MXU_COLS], preferred_element_type=F32) * inv)
            yield
    z = jnp.concatenate(cols, axis=1)
    q = z[:, 0:HK] * (GLA_DK ** -0.5)
    k = z[:, HK:2 * HK]
    v = z[:, 2 * HK:2 * HK + HV]
    g = z[:, 2 * HK + HV:2 * HK + 2 * HV]
    zc = z[:, 2 * HK + 2 * HV:]
    cb = zc[:, 0:CONV_WIDTH]
    cc = zc[:, CONV_WIDTH:2 * CONV_WIDTH]
    ch = zc[:, 2 * CONV_WIDTH:3 * CONV_WIDTH]
    return q, k, v, g, la, cb, cc, ch


def _short_conv(cb, uc, tail, cw_ref):
    row = lax.broadcasted_iota(jnp.int32, uc.shape, 0)
    c1 = tail[7:8, :]
    c2 = tail[6:7, :]
    prev1 = jnp.where(row == 0, c1, pltpu.roll(uc, 1, axis=0))
    prev2 = jnp.where(row == 0, c2, jnp.where(row == 1, c1, pltpu.roll(uc, 2, axis=0)))
    return cb * (cw_ref[0:1, :] * prev2 + cw_ref[1:2, :] * prev1 + cw_ref[2:3, :] * uc)


def _mixer_output(h, o, g, oc, ng_ref, gng_ref, wout_ref):
    heads = []
    for hd in range(GLA_HEADS):
        sl = slice(hd * GLA_DV, (hd + 1) * GLA_DV)
        heads.append(_rms(o[:, sl], gng_ref[...]))
    on = jnp.concatenate(heads, axis=1) * (g * _sigmoid(g))
    cat = jnp.concatenate([on.astype(BF16), oc.astype(BF16)], axis=1)
    yield
    cols = []
    for c in range(0, wout_ref.shape[1], MXU_COLS):
        cols.append(jnp.dot(cat, wout_ref[:, c:c + MXU_COLS], preferred_element_type=F32))
        yield
    return h + _rms(jnp.concatenate(cols, axis=1), ng_ref[3:4, :])


def _chunk_constants():
    t = np.arange(CHUNK)[:, None]
    s = np.arange(CHUNK)[None, :]
    blocks = []
    for j in range(1, N_LEVELS - 1):
        half = 1 << j
        mid = (t // (2 * half)) * (2 * half) + half
        blocks.append((s <= t).astype(np.float32) - (s < mid).astype(np.float32))
    blocks.append((s <= t).astype(np.float32))
    m = np.concatenate(blocks, axis=0)
    mst = np.concatenate([m, m], axis=1)
    x = np.bitwise_xor(t, s)
    top = np.floor(np.log2(np.maximum(x, 1))).astype(np.int32) + 1
    lvl = np.where(s > t, -1, np.where(s == t, 0, top)).astype(np.int32)
    return jnp.asarray(mst, BF16), jnp.asarray(np.tile(lvl, (1, 2)), jnp.int32)


def _pair_diag(a, b):
    z = jnp.zeros_like(a)
    return jnp.concatenate([jnp.concatenate([a, z], axis=1), jnp.concatenate([z, b], axis=1)], axis=0)


def _gla_tile(q, k, v, la, st, mst, lvl):
    c = CHUNK
    chunks = [slice(r, r + c) for r in range(0, q.shape[0], c)]
    pairs = [slice(p * 2 * GLA_DK, (p + 1) * 2 * GLA_DK) for p in range(GLA_HEADS // 2)]
    half = len(chunks) // 2
    cat = lambda xs: jnp.concatenate(xs, axis=0)

    def key_weights(kt, r, p):
        lo = 2 * p * GLA_DK
        return _pair_diag(kt[lo:lo + GLA_DK, r], kt[lo + GLA_DK:lo + 2 * GLA_DK, r])

    hi, lo = _split2(la)
    e = []
    for n, r in enumerate(chunks):
        e.append(jnp.dot(mst, cat([hi[r, :], lo[r, :]]), preferred_element_type=F32))
        if n == half - 1:
            yield
    yield
    bc = cat([en[(N_LEVELS - 2) * c:(N_LEVELS - 1) * c, :] for en in e])
    k_t = k.T
    bc_t = bc.T
    b_last = [bc_t[:, r.stop - 1:r.stop] for r in chunks]
    b_last_cols = jnp.concatenate([jnp.broadcast_to(b, (HK, c)) for b in b_last], axis=1)

    vb = v.astype(BF16)
    k_dec = (k_t * jnp.exp(b_last_cols - bc_t)).astype(BF16)
    upd_all = []
    for n, r in enumerate(chunks):
        upd_all.append(jnp.dot(k_dec[:, r], vb[r, :], preferred_element_type=F32))
        if n == half - 1:
            yield
    yield
    states = []
    for n in range(len(chunks)):
        states.append(st)
        upd = cat([upd_all[n][hd * GLA_DK:(hd + 1) * GLA_DK, hd * GLA_DV:(hd + 1) * GLA_DV]
                   for hd in range(GLA_HEADS)])
        st = st * jnp.exp(b_last[n]) + upd

    qb = q.astype(BF16)
    q1 = (q * jnp.exp(la)).astype(BF16)
    kb = k_t.astype(BF16)
    att = []
    for n, r in enumerate(chunks):
        row = []
        for p, cols in enumerate(pairs):
            x = jnp.dot(cat([qb[r, cols], q1[r, cols]]), key_weights(kb, r, p), preferred_element_type=F32)
            row.append(jnp.where(lvl == 1, x[c:, :], jnp.where(lvl == 0, x[:c, :], 0.0)))
        att.append(row)
        if n == half - 1:
            yield
    yield
    for lv in range(2, N_LEVELS):
        el = cat([en[(lv - 2) * c:(lv - 1) * c, :] for en in e])
        ql = (q * jnp.exp(jnp.minimum(el, 0.0))).astype(BF16)
        kl = (k_t * jnp.exp(jnp.minimum(-el.T, 0.0))).astype(BF16)
        for n, r in enumerate(chunks):
            for p, cols in enumerate(pairs):
                x = jnp.dot(ql[r, cols], key_weights(kl, r, p), preferred_element_type=F32)
                att[n][p] = jnp.where(lvl == lv, x, att[n][p])
        yield

    q_in = (q * jnp.exp(bc)).astype(BF16)
    o = []
    for n, r in enumerate(chunks):
        sb = states[n].astype(BF16)
        row = []
        for p, cols in enumerate(pairs):
            h0 = 2 * p
            s_w = _pair_diag(sb[h0 * GLA_DK:(h0 + 1) * GLA_DK, :], sb[(h0 + 1) * GLA_DK:(h0 + 2) * GLA_DK, :])
            v_w = _pair_diag(vb[r, h0 * GLA_DV:(h0 + 1) * GLA_DV], vb[r, (h0 + 1) * GLA_DV:(h0 + 2) * GLA_DV])
            row.append(jnp.dot(q_in[r, cols], s_w, preferred_element_type=F32)
                       + jnp.dot(att[n][p].astype(BF16), v_w, preferred_element_type=F32))
        o.append(jnp.concatenate(row, axis=1))
        if n == half - 1:
            yield
    return cat(o), st


def _mixer_prompt_body(tm, h_ref, ng_ref, wa_ref, wr_ref, wgu_ref, bg_ref, wc_ref, gng_ref, cw_ref,
                       wout_ref, mst_ref, lvl_ref, o_ref, sg_ref, sc_ref, st_scr, carry_scr):
    j = pl.program_id(1)

    @pl.when(j == 0)
    def _():
        st_scr[...] = jnp.zeros_like(st_scr)
        carry_scr[...] = jnp.zeros_like(carry_scr)

    def inputs(h):
        return _mixer_inputs(h, ng_ref, wa_ref, wr_ref, wgu_ref, bg_ref, wc_ref)

    def gla(q, k, v, la, st):
        return _gla_tile(q, k, v, la, st, mst_ref[...], lvl_ref[...])

    def output(h, og, g, cb, uc, tail):
        return _mixer_output(h, og, g, _short_conv(cb, uc, tail, cw_ref), ng_ref, gng_ref, wout_ref)

    h_a = h_ref[0:tm, :]
    h_b = h_ref[tm:2 * tm, :]
    q, k, v, g_a, la, cb_a, cc, ch = _finish(inputs(h_a))
    uc_a = cc * ch
    (og_a, st), (q, k, v, g_b, la, cb_b, cc, ch) = _run_together(gla(q, k, v, la, st_scr[...]), inputs(h_b))
    uc_b = cc * ch
    out_a, (og_b, st) = _run_together(output(h_a, og_a, g_a, cb_a, uc_a, carry_scr[...]), gla(q, k, v, la, st))
    o_ref[0:tm, :] = out_a
    o_ref[tm:2 * tm, :] = _finish(output(h_b, og_b, g_b, cb_b, uc_b, uc_a[tm - 8:tm, :]))
    st_scr[...] = st
    carry_scr[...] = uc_b[tm - 8:tm, :]

    @pl.when(j == pl.num_programs(1) - 1)
    def _():
        sg_ref[0] = st
        sc_ref[0] = carry_scr[6:8, :]


def _mixer_prompt(h, batch, seq, tm, ng, wa, wr, wgu, bg, wc, gng, cw, wout):
    assert seq % (2 * tm) == 0 and tm % (2 * CHUNK) == 0
    nt = seq // (2 * tm)
    mst, lvl = _chunk_constants()
    consts = [ng, wa, wr, wgu, bg, wc, gng, cw, wout, mst, lvl]
    row_spec = pl.BlockSpec((2 * tm, D_MODEL), lambda b, j: (b * nt + j, 0))
    est = (8 * _nbytes((tm, D_MODEL), F32) + sum(_nbytes(c.shape, c.dtype) for c in consts)
           + 4 * _nbytes((tm, 3 * HV), F32) + 2 * _nbytes((tm, 3 * HK + 2 * HV), F32)
           + 8 * _nbytes((tm, D_MODEL), F32))
    return pl.pallas_call(
        functools.partial(_mixer_prompt_body, tm),
        grid=(batch, nt),
        in_specs=[row_spec] + [_resident(c.shape) for c in consts],
        out_specs=[row_spec,
                   pl.BlockSpec((1, HK, GLA_DV), lambda b, j: (b, 0, 0)),
                   pl.BlockSpec((1, CONV_K - 1, CONV_WIDTH), lambda b, j: (b, 0, 0))],
        out_shape=[jax.ShapeDtypeStruct((batch * seq, D_MODEL), F32),
                   jax.ShapeDtypeStruct((batch, HK, GLA_DV), F32),
                   jax.ShapeDtypeStruct((batch, CONV_K - 1, CONV_WIDTH), F32)],
        scratch_shapes=[pltpu.VMEM((HK, GLA_DV), F32), pltpu.VMEM((8, CONV_WIDTH), F32)],
        compiler_params=pltpu.CompilerParams(
            dimension_semantics=("arbitrary", "arbitrary"), vmem_limit_bytes=_vmem_limit(est)),
        name="mixer_prompt",
    )(h, *consts)


def _mixer_sample_body(nb, bs, h_ref, s_ref, c0_ref, ng_ref, wa_ref, wr_ref, wgu_ref, bg_ref, wc_ref,
                       gng_ref, cw_ref, wout_ref, o_ref, so_ref, co_ref,
                       q_scr, k_scr, la3_scr, v_scr, g_scr, oc_scr, og_scr):
    i = pl.program_id(0)

    @pl.when(i == 0)
    def _():
        q, k, v, g, la, cb, cc, ch = _finish(
            _mixer_inputs(h_ref[...], ng_ref, wa_ref, wr_ref, wgu_ref, bg_ref, wc_ref))
        q_scr[...] = q.astype(BF16)
        k_scr[...] = k.astype(BF16)
        la3_scr[...] = jnp.concatenate(_split3(la), axis=0)
        v_scr[...] = v
        g_scr[...] = g
        uc = cc * ch
        c0 = c0_ref[...]
        old2 = c0[:, 0:CONV_WIDTH]
        old1 = c0[:, CONV_WIDTH:2 * CONV_WIDTH]
        y = cw_ref[0:1, :] * old2 + cw_ref[1:2, :] * old1 + cw_ref[2:3, :] * uc
        oc_scr[...] = cb * y
        co_ref[...] = jnp.concatenate([old1, uc], axis=1)

    first = pl.multiple_of(i * bs, bs)
    seq_row = lax.broadcasted_iota(jnp.int32, (nb, bs * LANES), 0)
    seq_col = lax.broadcasted_iota(jnp.int32, (nb, bs * LANES), 1) // LANES + first
    pick = jnp.where(seq_row == seq_col, 1.0, 0.0).astype(BF16)
    pick3 = jnp.concatenate([pick, pick, pick], axis=0)
    decay_t = jnp.exp(lax.dot_general(la3_scr[...], pick3, TN_DIMS, preferred_element_type=F32))
    k_t = lax.dot_general(k_scr[...], pick, TN_DIMS, preferred_element_type=F32)
    q_t = lax.dot_general(q_scr[...], pick, TN_DIMS, preferred_element_type=F32)
    v_blk = v_scr[pl.ds(first, bs), :]
    o_rows = []
    for t in range(bs):
        cols = slice(t * LANES, (t + 1) * LANES)
        v_b = jnp.concatenate(
            [jnp.broadcast_to(v_blk[t:t + 1, hd * GLA_DV:(hd + 1) * GLA_DV], (GLA_DK, GLA_DV))
             for hd in range(GLA_HEADS)], axis=0)
        s_new = decay_t[:, cols] * s_ref[t] + k_t[:, cols] * v_b
        so_ref[t] = s_new
        qs = q_t[:, cols] * s_new
        o_rows.append(jnp.concatenate(
            [jnp.sum(qs[hd * GLA_DK:(hd + 1) * GLA_DK, :], axis=0, keepdims=True)
             for hd in range(GLA_HEADS)], axis=1))
    og_scr[pl.ds(first, bs), :] = jnp.concatenate(o_rows, axis=0)

    @pl.when(i == pl.num_programs(0) - 1)
    def _():
        o_ref[...] = _finish(
            _mixer_output(h_ref[...], og_scr[...], g_scr[...], oc_scr[...], ng_ref, gng_ref, wout_ref))


def _mixer_sample(h, s0, c0, bs, ng, wa, wr, wgu, bg, wc, gng, cw, wout):
    nb = h.shape[0]
    assert nb == LANES and nb % bs == 0
    consts = [ng, wa, wr, wgu, bg, wc, gng, cw, wout]
    state_spec = pl.BlockSpec((bs, HK, GLA_DV), lambda i: (i, 0, 0))
    est = (sum(_nbytes(c.shape, c.dtype) for c in consts) + 4 * _nbytes((bs, HK, GLA_DV), F32)
           + 12 * _nbytes((nb, D_MODEL), F32) + 2 * _nbytes((nb, 3 * HV), F32))
    return pl.pallas_call(
        functools.partial(_mixer_sample_body, nb, bs),
        grid=(nb // bs,),
        in_specs=[_resident(h.shape), state_spec, _resident(c0.shape)] + [_resident(c.shape) for c in consts],
        out_specs=[pl.BlockSpec((nb, D_MODEL), lambda i: (0, 0)), state_spec,
                   pl.BlockSpec(c0.shape, lambda i: (0, 0))],
        out_shape=[jax.ShapeDtypeStruct((nb, D_MODEL), F32),
                   jax.ShapeDtypeStruct(s0.shape, F32),
                   jax.ShapeDtypeStruct(c0.shape, F32)],
        scratch_shapes=[pltpu.VMEM((nb, HK), BF16), pltpu.VMEM((nb, HK), BF16), pltpu.VMEM((3 * nb, HK), BF16),
                        pltpu.VMEM((nb, HV), F32), pltpu.VMEM((nb, HV), F32), pltpu.VMEM((nb, CONV_WIDTH), F32),
                        pltpu.VMEM((nb, HV), F32)],
        compiler_params=pltpu.CompilerParams(
            dimension_semantics=("arbitrary",), vmem_limit_bytes=_vmem_limit(est)),
        name="mixer_sample",
    )(h, s0, c0, *consts)


TM_PROMPT = 512
FFN_SIZES = (768, 256)
FFN_PLE_SIZES = (256, 256, 256, 256)
BS_SAMPLE = 32


def kernel(x_prompt, x_sample, state_gla, state_conv, p_prompt, p_sample, norm_g, w_in, w_gate_up, b_gate,
           gla_norm_g, conv_w, w_out, ffn1_gate, ffn1_up, ffn1_down, ffn2_gate, ffn2_up, ffn2_down,
           w_ple_proj, w_ple_gate):
    batch, seq, _ = x_prompt.shape
    nb = x_sample.shape[0]
    assert norm_g.shape[0] == 1 and x_sample.shape[1] == 1

    ng = norm_g[0]
    f1 = (_gain_into_rows(ffn1_gate[0], ng[0]), _gain_into_rows(ffn1_up[0], ng[0]), ffn1_down[0].astype(BF16))
    wgu = jnp.pad(w_gate_up[0], ((0, LANES - GLA_RANK), (0, 0))).astype(BF16)
    bg = b_gate[0].reshape(1, HK)
    gng = gla_norm_g[0].reshape(1, GLA_DV)
    cw = conv_w[0]
    r0 = 2 * HK + 2 * HV
    r1 = r0 + GLA_RANK
    jobs = [
        _PrepJob(ffn2_gate[0], gain=4), _PrepJob(ffn2_up[0], gain=4), _PrepJob(ffn2_down[0]),
        _PrepJob(jnp.transpose(w_in[0]), gain=2, transposed=True,
                 pieces=[(0, r0, r0), (r0, GLA_RANK, LANES), (r1, w_in.shape[2] - r1, w_in.shape[2] - r1)]),
        _PrepJob(w_out[0]), _PrepJob(w_ple_gate[0], gain=6), _PrepJob(w_ple_proj[0]),
    ]

    xp = x_prompt.reshape(batch * seq, D_MODEL)
    pp = p_prompt[0].reshape(batch * seq, PLE_DIM)
    hp, f2g, f2u, f2d, wa, wr, wc, wout, wpg, wpp = _ffn(xp, ng, *f1, 1, FFN_SIZES, prep=(ng.T, jobs))
    f2 = (f2g, f2u, f2d)
    mix_w = (ng, wa, wr, wgu, bg, wc, gng, cw, wout)
    hp, sg_p, sc_p = _mixer_prompt(hp, batch, seq, TM_PROMPT, *mix_w)
    hp = _ffn(hp, ng, *f2, 5, FFN_PLE_SIZES, ple_args=(pp, wpp, wpg))

    xs = x_sample.reshape(nb, D_MODEL)
    ps = p_sample[0].reshape(nb, PLE_DIM)
    s0 = state_gla[0].reshape(nb, HK, GLA_DV)
    c0 = state_conv[0].reshape(nb, (CONV_K - 1) * CONV_WIDTH)
    hs = _ffn(xs, ng, *f1, 1, (nb,))
    hs, sg_s, sc_s = _mixer_sample(hs, s0, c0, BS_SAMPLE, *mix_w)
    hs = _ffn(hs, ng, *f2, 5, (nb,), ple_args=(ps, wpp, wpg))

    return (hp.reshape(batch, seq, D_MODEL),
            hs.reshape(nb, 1, D_MODEL),
            sg_p.reshape(1, batch, GLA_HEADS, GLA_DK, GLA_DV),
            sc_p.reshape(1, batch, CONV_K - 1, CONV_WIDTH),
            sg_s.reshape(1, nb, GLA_HEADS, GLA_DK, GLA_DV),
            sc_s.reshape(1, nb, CONV_K - 1, CONV_WIDTH))
```

```python
import functools

import numpy as np
import jax
import jax.numpy as jnp
from jax import lax
from jax.experimental import pallas as pl
from jax.experimental.pallas import tpu as pltpu

F32 = jnp.float32
BF16 = jnp.bfloat16

D_MODEL = 1024
PLE_DIM = 256
GLA_HEADS = 4
GLA_DK = 64
GLA_DV = 128
HK = GLA_HEADS * GLA_DK
HV = GLA_HEADS * GLA_DV
GLA_RANK = 16
GLA_GATE_NORM = 16.0
CONV_WIDTH = 512
CONV_K = 3
D_FF = 2816
EPS = 1e-6

LANES = 128
MXU_COLS = 256
CHUNK = 128
N_LEVELS = 8
VMEM_CAP = 56 * 1024 * 1024

TN_DIMS = (((0,), (0,)), ((), ()))


def _inv_rms(x):
    return lax.rsqrt(jnp.mean(x * x, axis=-1, keepdims=True) + EPS)


def _rms(x, g):
    return x * _inv_rms(x) * g


def _gain_into_rows(w, g):
    return (g[:, None] * w).astype(BF16)


def _sigmoid(x):
    return 1.0 / (1.0 + jnp.exp(-x))


def _log_sigmoid(x):
    return jnp.minimum(x, 0.0) - jnp.log(1.0 + jnp.exp(-jnp.abs(x)))


def _zero_from(x):
    bits = x.astype(jnp.int32)
    sixteen = jnp.full(x.shape, 16, jnp.int32)
    return lax.shift_right_logical(lax.shift_right_logical(bits, sixteen), sixteen).astype(F32)


def _split2(x):
    hi = x.astype(BF16)
    return hi, (x - hi.astype(F32)).astype(BF16)


def _split3(x):
    hi = x.astype(BF16)
    r1 = x - hi.astype(F32)
    mid = r1.astype(BF16)
    lo = (r1 - mid.astype(F32)).astype(BF16)
    return hi, mid, lo


def _resident(shape):
    nd = len(shape)
    return pl.BlockSpec(shape, lambda *_: (0,) * nd, pipeline_mode=pl.Buffered(1))


def _nbytes(shape, dtype):
    return int(np.prod(shape)) * jnp.dtype(dtype).itemsize


def _vmem_limit(est_bytes):
    return int(min(VMEM_CAP, max(est_bytes * 5 // 4, 16 * 1024 * 1024)))


class _PrepJob:
    def __init__(self, w, gain=None, pieces=None, transposed=False):
        self.w = w
        self.gain = gain
        self.transposed = transposed
        self.rows, self.cols = (w.shape[1], w.shape[0]) if transposed else w.shape
        self.pieces = pieces or [(0, self.cols, self.cols)]


def _rows_of_transposed(w_t, rb):
    cols = w_t.shape[0]
    whole = cols // LANES * LANES
    parts = [w_t[j:j + LANES, :].T for j in range(0, whole, LANES)]
    if cols > whole:
        parts.append(w_t[cols - LANES:cols, :].T[:, LANES - (cols - whole):])
    t = jnp.concatenate(parts, axis=1)
    which = pl.program_id(0) % (LANES // rb)
    w = t[0:rb, :]
    for j in range(1, LANES // rb):
        w = jnp.where(which == j, t[j * rb:(j + 1) * rb, :], w)
    return w


def _prep_rows(job, g_ref, w_ref, out_refs):
    w = w_ref[...]
    if job.transposed:
        w = _rows_of_transposed(w, job.rb)
    if job.gain is not None:
        w = w * g_ref[:, job.gain:job.gain + 1]
    for (c0, keep, width), out_ref in zip(job.pieces, out_refs):
        piece = w[:, c0:c0 + width]
        if keep < width:
            lane = lax.broadcasted_iota(jnp.int32, piece.shape, 1)
            piece = jnp.where(lane < keep, piece, 0.0)
        out_ref[...] = piece.astype(BF16)


def _ffn_body(post, ple, sizes, jobs, *refs):
    refs = list(refs)
    x_ref, ng_ref, wg_ref, wu_ref, wd_ref = refs[:5]
    del refs[:5]
    if ple:
        p_ref, wpp_ref, wpg_ref = refs[:3]
        del refs[:3]
    if jobs:
        g_ref = refs.pop(0)
        job_in = refs[:len(jobs)]
        del refs[:len(jobs)]
    o_ref = refs.pop(0)
    h_scr = refs.pop()
    for job, w_ref in zip(jobs, job_in if jobs else ()):
        outs = refs[:len(job.pieces)]
        del refs[:len(job.pieces)]
        _prep_rows(job, g_ref, w_ref, outs)

    starts = [sum(sizes[:s]) for s in range(len(sizes))]

    n_chunks = D_FF // MXU_COLS

    def gate_up(s, beside):
        x = x_ref[starts[s]:starts[s] + sizes[s], :]
        xb = x.astype(BF16)
        inv = _inv_rms(x)
        for c in range(n_chunks):
            sl = slice(c * MXU_COLS, (c + 1) * MXU_COLS)
            if c == n_chunks // 2 and beside is not None:
                inv = inv + _zero_from(beside()[0:sizes[s], 0:1])
            g = jnp.dot(xb, wg_ref[:, sl], preferred_element_type=F32) * inv
            u = jnp.dot(xb, wu_ref[:, sl], preferred_element_type=F32) * inv
            h_scr[s % 2, 0:sizes[s], sl] = (g * _sigmoid(g) * u).astype(BF16)

    def down(s):
        return jnp.dot(h_scr[s % 2, 0:sizes[s], :], wd_ref[...], preferred_element_type=F32)

    def epilogue(s, y):
        rows = slice(starts[s], starts[s] + sizes[s])
        h = x_ref[rows, :] + 0.5 * _rms(y, ng_ref[post:post + 1, :])
        if ple:
            gate = _sigmoid(jnp.dot(h.astype(BF16), wpg_ref[...], preferred_element_type=F32) * _inv_rms(h))
            proj = jnp.dot(p_ref[rows, :].astype(BF16), wpp_ref[...], preferred_element_type=F32)
            h = h + _rms(proj * gate, ng_ref[7:8, :])
        o_ref[rows, :] = h
        return h

    pending = None
    for s in range(len(sizes)):
        gate_up(s, pending)
        y = down(s)
        pending = functools.partial(epilogue, s, y)
    pending()


def _ffn(x, ng, wg, wu, wd, post, sizes, ple_args=None, prep=None):
    rows = x.shape[0]
    tm = sum(sizes)
    ts = max(sizes)
    assert rows % tm == 0 and all(sz % 16 == 0 for sz in sizes)
    assert all(a >= b for a, b in zip(sizes, sizes[1:]))
    steps = rows // tm
    ple = ple_args is not None
    row_spec = pl.BlockSpec((tm, D_MODEL), lambda i: (i, 0))
    in_specs = [row_spec, _resident(ng.shape), _resident(wg.shape), _resident(wu.shape), _resident(wd.shape)]
    args = [x, ng, wg, wu, wd]
    est = (4 * _nbytes((tm, D_MODEL), F32) + 3 * _nbytes(wg.shape, BF16)
           + 2 * _nbytes((ts, D_FF), BF16) + _nbytes((ts, D_MODEL), BF16)
           + 6 * _nbytes((ts, MXU_COLS), F32) + 6 * _nbytes((ts, D_MODEL), F32))
    if ple:
        p, wpp, wpg = ple_args
        in_specs += [pl.BlockSpec((tm, PLE_DIM), lambda i: (i, 0)), _resident(wpp.shape), _resident(wpg.shape)]
        args += [p, wpp, wpg]
        est += 2 * _nbytes((tm, PLE_DIM), F32) + _nbytes(wpp.shape, BF16) + _nbytes(wpg.shape, BF16)
        est += 3 * _nbytes((ts, D_MODEL), F32)
    jobs = []
    out_specs = [row_spec]
    out_shape = [jax.ShapeDtypeStruct((rows, D_MODEL), F32)]
    if prep is not None:
        gains, jobs = prep
        in_specs.append(pl.BlockSpec((gains.shape[0] // steps, gains.shape[1]), lambda i: (i, 0)))
        args.append(gains)
        for job in jobs:
            r, c = job.rows, job.cols
            rb = job.rb = r // steps
            assert r % steps == 0 and rb % 16 == 0
            assert job.gain is None or r == gains.shape[0]
            if job.transposed:
                assert LANES % rb == 0
                in_specs.append(pl.BlockSpec((c, LANES), lambda i, k=LANES // rb: (0, i // k)))
            else:
                in_specs.append(pl.BlockSpec((rb, c), lambda i: (i, 0)))
            args.append(job.w)
            est += 2 * _nbytes((rb, c), F32)
            for _, _, width in job.pieces:
                out_specs.append(pl.BlockSpec((rb, width), lambda i: (i, 0)))
                out_shape.append(jax.ShapeDtypeStruct((r, width), BF16))
                est += 2 * _nbytes((rb, width), BF16)
    res = pl.pallas_call(
        functools.partial(_ffn_body, post, ple, tuple(sizes), jobs),
        grid=(steps,),
        in_specs=in_specs,
        out_specs=out_specs,
        out_shape=out_shape,
        scratch_shapes=[pltpu.VMEM((2, ts, D_FF), BF16)],
        compiler_params=pltpu.CompilerParams(
            dimension_semantics=("arbitrary",), vmem_limit_bytes=_vmem_limit(est)),
        name="ffn_ple" if ple else "ffn",
    )(*args)
    return res if prep is not None else res[0]


def _run_together(*stages):
    results = [None] * len(stages)
    live = list(enumerate(stages))
    while live:
        still = []
        for idx, gen in live:
            try:
                next(gen)
                still.append((idx, gen))
            except StopIteration as stop:
                results[idx] = stop.value
        live = still
    return results


def _finish(stages):
    return _run_together(stages)[0]


def _mixer_inputs(h, ng_ref, wa_ref, wr_ref, wgu_ref, bg_ref, wc_ref):
    hb = h.astype(BF16)
    inv = _inv_rms(h)
    yield
    r = jnp.dot(hb, wr_ref[...], preferred_element_type=F32) * inv
    logit = jnp.dot(r.astype(BF16), wgu_ref[...], preferred_element_type=F32) + bg_ref[...]
    la = _log_sigmoid(logit) * (1.0 / GLA_GATE_NORM)
    yield
    cols = []
    for w_ref in (wa_ref, wc_ref):
        for c in range(0, w_ref.shape[1], MXU_COLS):
            cols.append(jnp.dot(hb, w_ref[:, c:c + MXU_COLS], preferred_element_type=F32) * inv)
            yield
    z = jnp.concatenate(cols, axis=1)
    q = z[:, 0:HK] * (GLA_DK ** -0.5)
    k = z[:, HK:2 * HK]
    v = z[:, 2 * HK:2 * HK + HV]
    g = z[:, 2 * HK + HV:2 * HK + 2 * HV]
    zc = z[:, 2 * HK + 2 * HV:]
    cb = zc[:, 0:CONV_WIDTH]
    cc = zc[:, CONV_WIDTH:2 * CONV_WIDTH]
    ch = zc[:, 2 * CONV_WIDTH:3 * CONV_WIDTH]
    return q, k, v, g, la, cb, cc, ch


def _short_conv(cb, uc, tail, cw_ref):
    row = lax.broadcasted_iota(jnp.int32, uc.shape, 0)
    c1 = tail[7:8, :]
    c2 = tail[6:7, :]
    prev1 = jnp.where(row == 0, c1, pltpu.roll(uc, 1, axis=0))
    prev2 = jnp.where(row == 0, c2, jnp.where(row == 1, c1, pltpu.roll(uc, 2, axis=0)))
    return cb * (cw_ref[0:1, :] * prev2 + cw_ref[1:2, :] * prev1 + cw_ref[2:3, :] * uc)


def _mixer_output(h, o, g, oc, ng_ref, gng_ref, wout_ref):
    heads = []
    for hd in range(GLA_HEADS):
        sl = slice(hd * GLA_DV, (hd + 1) * GLA_DV)
        heads.append(_rms(o[:, sl], gng_ref[...]))
    on = jnp.concatenate(heads, axis=1) * (g * _sigmoid(g))
    cat = jnp.concatenate([on.astype(BF16), oc.astype(BF16)], axis=1)
    yield
    cols = []
    for c in range(0, wout_ref.shape[1], MXU_COLS):
        cols.append(jnp.dot(cat, wout_ref[:, c:c + MXU_COLS], preferred_element_type=F32))
        yield
    return h + _rms(jnp.concatenate(cols, axis=1), ng_ref[3:4, :])


def _chunk_constants():
    t = np.arange(CHUNK)[:, None]
    s = np.arange(CHUNK)[None, :]
    blocks = []
    for j in range(1, N_LEVELS - 1):
        half = 1 << j
        mid = (t // (2 * half)) * (2 * half) + half
        blocks.append((s <= t).astype(np.float32) - (s < mid).astype(np.float32))
    blocks.append((s <= t).astype(np.float32))
    m = np.concatenate(blocks, axis=0)
    mst = np.concatenate([m, m], axis=1)
    x = np.bitwise_xor(t, s)
    top = np.floor(np.log2(np.maximum(x, 1))).astype(np.int32) + 1
    lvl = np.where(s > t, -1, np.where(s == t, 0, top)).astype(np.int32)
    return jnp.asarray(mst, BF16), jnp.asarray(np.tile(lvl, (1, 2)), jnp.int32)


def _pair_diag(a, b):
    z = jnp.zeros_like(a)
    return jnp.concatenate([jnp.concatenate([a, z], axis=1), jnp.concatenate([z, b], axis=1)], axis=0)


def _gla_tile(q, k, v, la, st, mst, lvl):
    c = CHUNK
    chunks = [slice(r, r + c) for r in range(0, q.shape[0], c)]
    pairs = [slice(p * 2 * GLA_DK, (p + 1) * 2 * GLA_DK) for p in range(GLA_HEADS // 2)]
    half = len(chunks) // 2
    cat = lambda xs: jnp.concatenate(xs, axis=0)

    def key_weights(kt, r, p):
        lo = 2 * p * GLA_DK
        return _pair_diag(kt[lo:lo + GLA_DK, r], kt[lo + GLA_DK:lo + 2 * GLA_DK, r])

    hi, lo = _split2(la)
    e = []
    for n, r in enumerate(chunks):
        e.append(jnp.dot(mst, cat([hi[r, :], lo[r, :]]), preferred_element_type=F32))
        if n == half - 1:
            yield
    yield
    bc = cat([en[(N_LEVELS - 2) * c:(N_LEVELS - 1) * c, :] for en in e])
    k_t = k.T
    bc_t = bc.T
    b_last = [bc_t[:, r.stop - 1:r.stop] for r in chunks]
    b_last_cols = jnp.concatenate([jnp.broadcast_to(b, (HK, c)) for b in b_last], axis=1)

    vb = v.astype(BF16)
    k_dec = (k_t * jnp.exp(b_last_cols - bc_t)).astype(BF16)
    upd_all = []
    for n, r in enumerate(chunks):
        upd_all.append(jnp.dot(k_dec[:, r], vb[r, :], preferred_element_type=F32))
        if n == half - 1:
            yield
    yield
    states = []
    for n in range(len(chunks)):
        states.append(st)
        upd = cat([upd_all[n][hd * GLA_DK:(hd + 1) * GLA_DK, hd * GLA_DV:(hd + 1) * GLA_DV]
                   for hd in range(GLA_HEADS)])
        st = st * jnp.exp(b_last[n]) + upd

    qb = q.astype(BF16)
    q1 = (q * jnp.exp(la)).astype(BF16)
    kb = k_t.astype(BF16)
    att = []
    for n, r in enumerate(chunks):
        row = []
        for p, cols in enumerate(pairs):
            x = jnp.dot(cat([qb[r, cols], q1[r, cols]]), key_weights(kb, r, p), preferred_element_type=F32)
            row.append(jnp.where(lvl == 1, x[c:, :], jnp.where(lvl == 0, x[:c, :], 0.0)))
        att.append(row)
        if n == half - 1:
            yield
    yield
    for lv in range(2, N_LEVELS):
        el = cat([en[(lv - 2) * c:(lv - 1) * c, :] for en in e])
        ql = (q * jnp.exp(jnp.minimum(el, 0.0))).astype(BF16)
        kl = (k_t * jnp.exp(jnp.minimum(-el.T, 0.0))).astype(BF16)
        for n, r in enumerate(chunks):
            for p, cols in enumerate(pairs):
                x = jnp.dot(ql[r, cols], key_weights(kl, r, p), preferred_element_type=F32)
                att[n][p] = jnp.where(lvl == lv, x, att[n][p])
        yield

    q_in = (q * jnp.exp(bc)).astype(BF16)
    o = []
    for n, r in enumerate(chunks):
        sb = states[n].astype(BF16)
        row = []
        for p, cols in enumerate(pairs):
            h0 = 2 * p
            s_w = _pair_diag(sb[h0 * GLA_DK:(h0 + 1) * GLA_DK, :], sb[(h0 + 1) * GLA_DK:(h0 + 2) * GLA_DK, :])
            v_w = _pair_diag(vb[r, h0 * GLA_DV:(h0 + 1) * GLA_DV], vb[r, (h0 + 1) * GLA_DV:(h0 + 2) * GLA_DV])
            row.append(jnp.dot(q_in[r, cols], s_w, preferred_element_type=F32)
                       + jnp.dot(att[n][p].astype(BF16), v_w, preferred_element_type=F32))
        o.append(jnp.concatenate(row, axis=1))
        if n == half - 1:
            yield
    return cat(o), st


def _mixer_prompt_body(tm, h_ref, ng_ref, wa_ref, wr_ref, wgu_ref, bg_ref, wc_ref, gng_ref, cw_ref,
                       wout_ref, mst_ref, lvl_ref, o_ref, sg_ref, sc_ref, st_scr, carry_scr):
    j = pl.program_id(1)

    @pl.when(j == 0)
    def _():
        st_scr[...] = jnp.zeros_like(st_scr)
        carry_scr[...] = jnp.zeros_like(carry_scr)

    def inputs(h):
        return _mixer_inputs(h, ng_ref, wa_ref, wr_ref, wgu_ref, bg_ref, wc_ref)

    def gla(q, k, v, la, st):
        return _gla_tile(q, k, v, la, st, mst_ref[...], lvl_ref[...])

    def output(h, og, g, cb, uc, tail):
        return _mixer_output(h, og, g, _short_conv(cb, uc, tail, cw_ref), ng_ref, gng_ref, wout_ref)

    h_a = h_ref[0:tm, :]
    h_b = h_ref[tm:2 * tm, :]
    q, k, v, g_a, la, cb_a, cc, ch = _finish(inputs(h_a))
    uc_a = cc * ch
    (og_a, st), (q, k, v, g_b, la, cb_b, cc, ch) = _run_together(gla(q, k, v, la, st_scr[...]), inputs(h_b))
    uc_b = cc * ch
    out_a, (og_b, st) = _run_together(output(h_a, og_a, g_a, cb_a, uc_a, carry_scr[...]), gla(q, k, v, la, st))
    o_ref[0:tm, :] = out_a
    o_ref[tm:2 * tm, :] = _finish(output(h_b, og_b, g_b, cb_b, uc_b, uc_a[tm - 8:tm, :]))
    st_scr[...] = st
    carry_scr[...] = uc_b[tm - 8:tm, :]

    @pl.when(j == pl.num_programs(1) - 1)
    def _():
        sg_ref[0] = st
        sc_ref[0] = carry_scr[6:8, :]


def _mixer_prompt(h, batch, seq, tm, ng, wa, wr, wgu, bg, wc, gng, cw, wout):
    assert seq % (2 * tm) == 0 and tm % (2 * CHUNK) == 0
    nt = seq // (2 * tm)
    mst, lvl = _chunk_constants()
    consts = [ng, wa, wr, wgu, bg, wc, gng, cw, wout, mst, lvl]
    row_spec = pl.BlockSpec((2 * tm, D_MODEL), lambda b, j: (b * nt + j, 0))
    est = (8 * _nbytes((tm, D_MODEL), F32) + sum(_nbytes(c.shape, c.dtype) for c in consts)
           + 4 * _nbytes((tm, 3 * HV), F32) + 2 * _nbytes((tm, 3 * HK + 2 * HV), F32)
           + 8 * _nbytes((tm, D_MODEL), F32))
    return pl.pallas_call(
        functools.partial(_mixer_prompt_body, tm),
        grid=(batch, nt),
        in_specs=[row_spec] + [_resident(c.shape) for c in consts],
        out_specs=[row_spec,
                   pl.BlockSpec((1, HK, GLA_DV), lambda b, j: (b, 0, 0)),
                   pl.BlockSpec((1, CONV_K - 1, CONV_WIDTH), lambda b, j: (b, 0, 0))],
        out_shape=[jax.ShapeDtypeStruct((batch * seq, D_MODEL), F32),
                   jax.ShapeDtypeStruct((batch, HK, GLA_DV), F32),
                   jax.ShapeDtypeStruct((batch, CONV_K - 1, CONV_WIDTH), F32)],
        scratch_shapes=[pltpu.VMEM((HK, GLA_DV), F32), pltpu.VMEM((8, CONV_WIDTH), F32)],
        compiler_params=pltpu.CompilerParams(
            dimension_semantics=("arbitrary", "arbitrary"), vmem_limit_bytes=_vmem_limit(est)),
        name="mixer_prompt",
    )(h, *consts)


def _mixer_sample_body(nb, bs, h_ref, s_ref, c0_ref, ng_ref, wa_ref, wr_ref, wgu_ref, bg_ref, wc_ref,
                       gng_ref, cw_ref, wout_ref, o_ref, so_ref, co_ref,
                       q_scr, k_scr, la3_scr, v_scr, g_scr, oc_scr, og_scr):
    i = pl.program_id(0)

    @pl.when(i == 0)
    def _():
        q, k, v, g, la, cb, cc, ch = _finish(
            _mixer_inputs(h_ref[...], ng_ref, wa_ref, wr_ref, wgu_ref, bg_ref, wc_ref))
        q_scr[...] = q.astype(BF16)
        k_scr[...] = k.astype(BF16)
        la3_scr[...] = jnp.concatenate(_split3(la), axis=0)
        v_scr[...] = v
        g_scr[...] = g
        uc = cc * ch
        c0 = c0_ref[...]
        old2 = c0[:, 0:CONV_WIDTH]
        old1 = c0[:, CONV_WIDTH:2 * CONV_WIDTH]
        y = cw_ref[0:1, :] * old2 + cw_ref[1:2, :] * old1 + cw_ref[2:3, :] * uc
        oc_scr[...] = cb * y
        co_ref[...] = jnp.concatenate([old1, uc], axis=1)

    first = pl.multiple_of(i * bs, bs)
    seq_row = lax.broadcasted_iota(jnp.int32, (nb, bs * LANES), 0)
    seq_col = lax.broadcasted_iota(jnp.int32, (nb, bs * LANES), 1) // LANES + first
    pick = jnp.where(seq_row == seq_col, 1.0, 0.0).astype(BF16)
    pick3 = jnp.concatenate([pick, pick, pick], axis=0)
    decay_t = jnp.exp(lax.dot_general(la3_scr[...], pick3, TN_DIMS, preferred_element_type=F32))
    k_t = lax.dot_general(k_scr[...], pick, TN_DIMS, preferred_element_type=F32)
    q_t = lax.dot_general(q_scr[...], pick, TN_DIMS, preferred_element_type=F32)
    v_blk = v_scr[pl.ds(first, bs), :]
    o_rows = []
    for t in range(bs):
        cols = slice(t * LANES, (t + 1) * LANES)
        v_b = jnp.concatenate(
            [jnp.broadcast_to(v_blk[t:t + 1, hd * GLA_DV:(hd + 1) * GLA_DV], (GLA_DK, GLA_DV))
             for hd in range(GLA_HEADS)], axis=0)
        s_new = decay_t[:, cols] * s_ref[t] + k_t[:, cols] * v_b
        so_ref[t] = s_new
        qs = q_t[:, cols] * s_new
        o_rows.append(jnp.concatenate(
            [jnp.sum(qs[hd * GLA_DK:(hd + 1) * GLA_DK, :], axis=0, keepdims=True)
             for hd in range(GLA_HEADS)], axis=1))
    og_scr[pl.ds(first, bs), :] = jnp.concatenate(o_rows, axis=0)

    @pl.when(i == pl.num_programs(0) - 1)
    def _():
        o_ref[...] = _finish(
            _mixer_output(h_ref[...], og_scr[...], g_scr[...], oc_scr[...], ng_ref, gng_ref, wout_ref))


def _mixer_sample(h, s0, c0, bs, ng, wa, wr, wgu, bg, wc, gng, cw, wout):
    nb = h.shape[0]
    assert nb == LANES and nb % bs == 0
    consts = [ng, wa, wr, wgu, bg, wc, gng, cw, wout]
    state_spec = pl.BlockSpec((bs, HK, GLA_DV), lambda i: (i, 0, 0))
    est = (sum(_nbytes(c.shape, c.dtype) for c in consts) + 4 * _nbytes((bs, HK, GLA_DV), F32)
           + 12 * _nbytes((nb, D_MODEL), F32) + 2 * _nbytes((nb, 3 * HV), F32))
    return pl.pallas_call(
        functools.partial(_mixer_sample_body, nb, bs),
        grid=(nb // bs,),
        in_specs=[_resident(h.shape), state_spec, _resident(c0.shape)] + [_resident(c.shape) for c in consts],
        out_specs=[pl.BlockSpec((nb, D_MODEL), lambda i: (0, 0)), state_spec,
                   pl.BlockSpec(c0.shape, lambda i: (0, 0))],
        out_shape=[jax.ShapeDtypeStruct((nb, D_MODEL), F32),
                   jax.ShapeDtypeStruct(s0.shape, F32),
                   jax.ShapeDtypeStruct(c0.shape, F32)],
        scratch_shapes=[pltpu.VMEM((nb, HK), BF16), pltpu.VMEM((nb, HK), BF16), pltpu.VMEM((3 * nb, HK), BF16),
                        pltpu.VMEM((nb, HV), F32), pltpu.VMEM((nb, HV), F32), pltpu.VMEM((nb, CONV_WIDTH), F32),
                        pltpu.VMEM((nb, HV), F32)],
        compiler_params=pltpu.CompilerParams(
            dimension_semantics=("arbitrary",), vmem_limit_bytes=_vmem_limit(est)),
        name="mixer_sample",
    )(h, s0, c0, *consts)


TM_PROMPT = 512
FFN_SIZES = (768, 256)
FFN_PLE_SIZES = (256, 256, 256, 256)
BS_SAMPLE = 32


def kernel(x_prompt, x_sample, state_gla, state_conv, p_prompt, p_sample, norm_g, w_in, w_gate_up, b_gate,
           gla_norm_g, conv_w, w_out, ffn1_gate, ffn1_up, ffn1_down, ffn2_gate, ffn2_up, ffn2_down,
           w_ple_proj, w_ple_gate):
    batch, seq, _ = x_prompt.shape
    nb = x_sample.shape[0]
    assert norm_g.shape[0] == 1 and x_sample.shape[1] == 1

    ng = norm_g[0]
    f1 = (_gain_into_rows(ffn1_gate[0], ng[0]), _gain_into_rows(ffn1_up[0], ng[0]), ffn1_down[0].astype(BF16))
    wgu = jnp.pad(w_gate_up[0], ((0, LANES - GLA_RANK), (0, 0))).astype(BF16)
    bg = b_gate[0].reshape(1, HK)
    gng = gla_norm_g[0].reshape(1, GLA_DV)
    cw = conv_w[0]
    r0 = 2 * HK + 2 * HV
    r1 = r0 + GLA_RANK
    jobs = [
        _PrepJob(ffn2_gate[0], gain=4), _PrepJob(ffn2_up[0], gain=4), _PrepJob(ffn2_down[0]),
        _PrepJob(jnp.transpose(w_in[0]), gain=2, transposed=True,
                 pieces=[(0, r0, r0), (r0, GLA_RANK, LANES), (r1, w_in.shape[2] - r1, w_in.shape[2] - r1)]),
        _PrepJob(w_out[0]), _PrepJob(w_ple_gate[0], gain=6), _PrepJob(w_ple_proj[0]),
    ]

    xp = x_prompt.reshape(batch * seq, D_MODEL)
    pp = p_prompt[0].reshape(batch * seq, PLE_DIM)
    hp, f2g, f2u, f2d, wa, wr, wc, wout, wpg, wpp = _ffn(xp, ng, *f1, 1, FFN_SIZES, prep=(ng.T, jobs))
    f2 = (f2g, f2u, f2d)
    mix_w = (ng, wa, wr, wgu, bg, wc, gng, cw, wout)
    hp, sg_p, sc_p = _mixer_prompt(hp, batch, seq, TM_PROMPT, *mix_w)
    hp = _ffn(hp, ng, *f2, 5, FFN_PLE_SIZES, ple_args=(pp, wpp, wpg))

    xs = x_sample.reshape(nb, D_MODEL)
    ps = p_sample[0].reshape(nb, PLE_DIM)
    s0 = state_gla[0].reshape(nb, HK, GLA_DV)
    c0 = state_conv[0].reshape(nb, (CONV_K - 1) * CONV_WIDTH)
    hs = _ffn(xs, ng, *f1, 1, (nb,))
    hs, sg_s, sc_s = _mixer_sample(hs, s0, c0, BS_SAMPLE, *mix_w)
    hs = _ffn(hs, ng, *f2, 5, (nb,), ple_args=(ps, wpp, wpg))

    return (hp.reshape(batch, seq, D_MODEL),
            hs.reshape(nb, 1, D_MODEL),
            sg_p.reshape(1, batch, GLA_HEADS, GLA_DK, GLA_DV),
            sc_p.reshape(1, batch, CONV_K - 1, CONV_WIDTH),
            sg_s.reshape(1, nb, GLA_HEADS, GLA_DK, GLA_DV),
            sc_s.reshape(1, nb, CONV_K - 1, CONV_WIDTH))
```

```python
import functools

import numpy as np
import jax
import jax.numpy as jnp
from jax import lax
from jax.experimental import pallas as pl
from jax.experimental.pallas import tpu as pltpu

F32 = jnp.float32
BF16 = jnp.bfloat16

D_MODEL = 1024
PLE_DIM = 256
GLA_HEADS = 4
GLA_DK = 64
GLA_DV = 128
HK = GLA_HEADS * GLA_DK
HV = GLA_HEADS * GLA_DV
GLA_RANK = 16
GLA_GATE_NORM = 16.0
CONV_WIDTH = 512
CONV_K = 3
D_FF = 2816
EPS = 1e-6

LANES = 128
MXU_COLS = 256
CHUNK = 128
N_LEVELS = 8
VMEM_CAP = 56 * 1024 * 1024

TN_DIMS = (((0,), (0,)), ((), ()))


def _inv_rms(x):
    return lax.rsqrt(jnp.mean(x * x, axis=-1, keepdims=True) + EPS)


def _rms(x, g):
    return x * _inv_rms(x) * g


def _gain_into_rows(w, g):
    return (g[:, None] * w).astype(BF16)


def _sigmoid(x):
    return 1.0 / (1.0 + jnp.exp(-x))


def _log_sigmoid(x):
    return jnp.minimum(x, 0.0) - jnp.log(1.0 + jnp.exp(-jnp.abs(x)))


def _zero_from(x):
    bits = x.astype(jnp.int32)
    sixteen = jnp.full(x.shape, 16, jnp.int32)
    return lax.shift_right_logical(lax.shift_right_logical(bits, sixteen), sixteen).astype(F32)


def _split2(x):
    hi = x.astype(BF16)
    return hi, (x - hi.astype(F32)).astype(BF16)


def _split3(x):
    hi = x.astype(BF16)
    r1 = x - hi.astype(F32)
    mid = r1.astype(BF16)
    lo = (r1 - mid.astype(F32)).astype(BF16)
    return hi, mid, lo


def _resident(shape):
    nd = len(shape)
    return pl.BlockSpec(shape, lambda *_: (0,) * nd, pipeline_mode=pl.Buffered(1))


def _nbytes(shape, dtype):
    return int(np.prod(shape)) * jnp.dtype(dtype).itemsize


def _vmem_limit(est_bytes):
    return int(min(VMEM_CAP, max(est_bytes * 5 // 4, 16 * 1024 * 1024)))


class _PrepJob:
    def __init__(self, w, gain=None, pieces=None, transposed=False):
        self.w = w
        self.gain = gain
        self.transposed = transposed
        self.rows, self.cols = (w.shape[1], w.shape[0]) if transposed else w.shape
        self.pieces = pieces or [(0, self.cols, self.cols)]


def _rows_of_transposed(w_t, rb):
    cols = w_t.shape[0]
    whole = cols // LANES * LANES
    parts = [w_t[j:j + LANES, :].T for j in range(0, whole, LANES)]
    if cols > whole:
        parts.append(w_t[cols - LANES:cols, :].T[:, LANES - (cols - whole):])
    t = jnp.concatenate(parts, axis=1)
    which = pl.program_id(0) % (LANES // rb)
    w = t[0:rb, :]
    for j in range(1, LANES // rb):
        w = jnp.where(which == j, t[j * rb:(j + 1) * rb, :], w)
    return w


def _prep_rows(job, g_ref, w_ref, out_refs):
    w = w_ref[...]
    if job.transposed:
        w = _rows_of_transposed(w, job.rb)
    if job.gain is not None:
        w = w * g_ref[:, job.gain:job.gain + 1]
    for (c0, keep, width), out_ref in zip(job.pieces, out_refs):
        piece = w[:, c0:c0 + width]
        if keep < width:
            lane = lax.broadcasted_iota(jnp.int32, piece.shape, 1)
            piece = jnp.where(lane < keep, piece, 0.0)
        out_ref[...] = piece.astype(BF16)


def _ffn_body(post, ple, sizes, jobs, *refs):
    refs = list(refs)
    x_ref, ng_ref, wg_ref, wu_ref, wd_ref = refs[:5]
    del refs[:5]
    if ple:
        p_ref, wpp_ref, wpg_ref = refs[:3]
        del refs[:3]
    if jobs:
        g_ref = refs.pop(0)
        job_in = refs[:len(jobs)]
        del refs[:len(jobs)]
    o_ref = refs.pop(0)
    h_scr = refs.pop()
    for job, w_ref in zip(jobs, job_in if jobs else ()):
        outs = refs[:len(job.pieces)]
        del refs[:len(job.pieces)]
        _prep_rows(job, g_ref, w_ref, outs)

    starts = [sum(sizes[:s]) for s in range(len(sizes))]

    n_chunks = D_FF // MXU_COLS

    def gate_up(s, beside):
        x = x_ref[starts[s]:starts[s] + sizes[s], :]
        xb = x.astype(BF16)
        inv = _inv_rms(x)
        for c in range(n_chunks):
            sl = slice(c * MXU_COLS, (c + 1) * MXU_COLS)
            if c == n_chunks // 2 and beside is not None:
                inv = inv + _zero_from(beside()[0:sizes[s], 0:1])
            g = jnp.dot(xb, wg_ref[:, sl], preferred_element_type=F32) * inv
            u = jnp.dot(xb, wu_ref[:, sl], preferred_element_type=F32) * inv
            h_scr[s % 2, 0:sizes[s], sl] = (g * _sigmoid(g) * u).astype(BF16)

    def down(s):
        return jnp.dot(h_scr[s % 2, 0:sizes[s], :], wd_ref[...], preferred_element_type=F32)

    def epilogue(s, y):
        rows = slice(starts[s], starts[s] + sizes[s])
        h = x_ref[rows, :] + 0.5 * _rms(y, ng_ref[post:post + 1, :])
        if ple:
            gate = _sigmoid(jnp.dot(h.astype(BF16), wpg_ref[...], preferred_element_type=F32) * _inv_rms(h))
            proj = jnp.dot(p_ref[rows, :].astype(BF16), wpp_ref[...], preferred_element_type=F32)
            h = h + _rms(proj * gate, ng_ref[7:8, :])
        o_ref[rows, :] = h
        return h

    pending = None
    for s in range(len(sizes)):
        gate_up(s, pending)
        y = down(s)
        pending = functools.partial(epilogue, s, y)
    pending()


def _ffn(x, ng, wg, wu, wd, post, sizes, ple_args=None, prep=None):
    rows = x.shape[0]
    tm = sum(sizes)
    ts = max(sizes)
    assert rows % tm == 0 and all(sz % 16 == 0 for sz in sizes)
    assert all(a >= b for a, b in zip(sizes, sizes[1:]))
    steps = rows // tm
    ple = ple_args is not None
    row_spec = pl.BlockSpec((tm, D_MODEL), lambda i: (i, 0))
    in_specs = [row_spec, _resident(ng.shape), _resident(wg.shape), _resident(wu.shape), _resident(wd.shape)]
    args = [x, ng, wg, wu, wd]
    est = (4 * _nbytes((tm, D_MODEL), F32) + 3 * _nbytes(wg.shape, BF16)
           + 2 * _nbytes((ts, D_FF), BF16) + _nbytes((ts, D_MODEL), BF16)
           + 6 * _nbytes((ts, MXU_COLS), F32) + 6 * _nbytes((ts, D_MODEL), F32))
    if ple:
        p, wpp, wpg = ple_args
        in_specs += [pl.BlockSpec((tm, PLE_DIM), lambda i: (i, 0)), _resident(wpp.shape), _resident(wpg.shape)]
        args += [p, wpp, wpg]
        est += 2 * _nbytes((tm, PLE_DIM), F32) + _nbytes(wpp.shape, BF16) + _nbytes(wpg.shape, BF16)
        est += 3 * _nbytes((ts, D_MODEL), F32)
    jobs = []
    out_specs = [row_spec]
    out_shape = [jax.ShapeDtypeStruct((rows, D_MODEL), F32)]
    if prep is not None:
        gains, jobs = prep
        in_specs.append(pl.BlockSpec((gains.shape[0] // steps, gains.shape[1]), lambda i: (i, 0)))
        args.append(gains)
        for job in jobs:
            r, c = job.rows, job.cols
            rb = job.rb = r // steps
            assert r % steps == 0 and rb % 16 == 0
            assert job.gain is None or r == gains.shape[0]
            if job.transposed:
                assert LANES % rb == 0
                in_specs.append(pl.BlockSpec((c, LANES), lambda i, k=LANES // rb: (0, i // k)))
            else:
                in_specs.append(pl.BlockSpec((rb, c), lambda i: (i, 0)))
            args.append(job.w)
            est += 2 * _nbytes((rb, c), F32)
            for _, _, width in job.pieces:
                out_specs.append(pl.BlockSpec((rb, width), lambda i: (i, 0)))
                out_shape.append(jax.ShapeDtypeStruct((r, width), BF16))
                est += 2 * _nbytes((rb, width), BF16)
    res = pl.pallas_call(
        functools.partial(_ffn_body, post, ple, tuple(sizes), jobs),
        grid=(steps,),
        in_specs=in_specs,
        out_specs=out_specs,
        out_shape=out_shape,
        scratch_shapes=[pltpu.VMEM((2, ts, D_FF), BF16)],
        compiler_params=pltpu.CompilerParams(
            dimension_semantics=("arbitrary",), vmem_limit_bytes=_vmem_limit(est)),
        name="ffn_ple" if ple else "ffn",
    )(*args)
    return res if prep is not None else res[0]


def _run_together(*stages):
    results = [None] * len(stages)
    live = list(enumerate(stages))
    while live:
        still = []
        for idx, gen in live:
            try:
                next(gen)
                still.append((idx, gen))
            except StopIteration as stop:
                results[idx] = stop.value
        live = still
    return results


def _finish(stages):
    return _run_together(stages)[0]


def _mixer_inputs(h, ng_ref, wa_ref, wr_ref, wgu_ref, bg_ref, wc_ref):
    hb = h.astype(BF16)
    inv = _inv_rms(h)
    yield
    r = jnp.dot(hb, wr_ref[...], preferred_element_type=F32) * inv
    logit = jnp.dot(r.astype(BF16), wgu_ref[...], preferred_element_type=F32) + bg_ref[...]
    la = _log_sigmoid(logit) * (1.0 / GLA_GATE_NORM)
    yield
    cols = []
    for w_ref in (wa_ref, wc_ref):
        for c in range(0, w_ref.shape[1], MXU_COLS):
            cols.append(jnp.dot(hb, w_ref[:, c:c + MXU_COLS], preferred_element_type=F32) * inv)
            yield
    z = jnp.concatenate(cols, axis=1)
    q = z[:, 0:HK] * (GLA_DK ** -0.5)
    k = z[:, HK:2 * HK]
    v = z[:, 2 * HK:2 * HK + HV]
    g = z[:, 2 * HK + HV:2 * HK + 2 * HV]
    zc = z[:, 2 * HK + 2 * HV:]
    cb = zc[:, 0:CONV_WIDTH]
    cc = zc[:, CONV_WIDTH:2 * CONV_WIDTH]
    ch = zc[:, 2 * CONV_WIDTH:3 * CONV_WIDTH]
    return q, k, v, g, la, cb, cc, ch


def _short_conv(cb, uc, tail, cw_ref):
    row = lax.broadcasted_iota(jnp.int32, uc.shape, 0)
    c1 = tail[7:8, :]
    c2 = tail[6:7, :]
    prev1 = jnp.where(row == 0, c1, pltpu.roll(uc, 1, axis=0))
    prev2 = jnp.where(row == 0, c2, jnp.where(row == 1, c1, pltpu.roll(uc, 2, axis=0)))
    return cb * (cw_ref[0:1, :] * prev2 + cw_ref[1:2, :] * prev1 + cw_ref[2:3, :] * uc)


def _mixer_output(h, o, g, oc, ng_ref, gng_ref, wout_ref):
    heads = []
    for hd in range(GLA_HEADS):
        sl = slice(hd * GLA_DV, (hd + 1) * GLA_DV)
        heads.append(_rms(o[:, sl], gng_ref[...]))
    on = jnp.concatenate(heads, axis=1) * (g * _sigmoid(g))
    cat = jnp.concatenate([on.astype(BF16), oc.astype(BF16)], axis=1)
    yield
    cols = []
    for c in range(0, wout_ref.shape[1], MXU_COLS):
        cols.append(jnp.dot(cat, wout_ref[:, c:c + MXU_COLS], preferred_element_type=F32))
        yield
    return h + _rms(jnp.concatenate(cols, axis=1), ng_ref[3:4, :])


def _chunk_constants():
    t = np.arange(CHUNK)[:, None]
    s = np.arange(CHUNK)[None, :]
    blocks = []
    for j in range(1, N_LEVELS - 1):
        half = 1 << j
        mid = (t // (2 * half)) * (2 * half) + half
        blocks.append((s <= t).astype(np.float32) - (s < mid).astype(np.float32))
    blocks.append((s <= t).astype(np.float32))
    m = np.concatenate(blocks, axis=0)
    mst = np.concatenate([m, m], axis=1)
    x = np.bitwise_xor(t, s)
    top = np.floor(np.log2(np.maximum(x, 1))).astype(np.int32) + 1
    lvl = np.where(s > t, -1, np.where(s == t, 0, top)).astype(np.int32)
    return jnp.asarray(mst, BF16), jnp.asarray(np.tile(lvl, (1, 2)), jnp.int32)


def _pair_diag(a, b):
    z = jnp.zeros_like(a)
    return jnp.concatenate([jnp.concatenate([a, z], axis=1), jnp.concatenate([z, b], axis=1)], axis=0)


def _gla_tile(q, k, v, la, st, mst, lvl):
    c = CHUNK
    chunks = [slice(r, r + c) for r in range(0, q.shape[0], c)]
    pairs = [slice(p * 2 * GLA_DK, (p + 1) * 2 * GLA_DK) for p in range(GLA_HEADS // 2)]
    half = len(chunks) // 2
    cat = lambda xs: jnp.concatenate(xs, axis=0)

    def key_weights(kt, r, p):
        lo = 2 * p * GLA_DK
        return _pair_diag(kt[lo:lo + GLA_DK, r], kt[lo + GLA_DK:lo + 2 * GLA_DK, r])

    hi, lo = _split2(la)
    e = []
    for n, r in enumerate(chunks):
        e.append(jnp.dot(mst, cat([hi[r, :], lo[r, :]]), preferred_element_type=F32))
        if n == half - 1:
            yield
    yield
    bc = cat([en[(N_LEVELS - 2) * c:(N_LEVELS - 1) * c, :] for en in e])
    k_t = k.T
    bc_t = bc.T
    b_last = [bc_t[:, r.stop - 1:r.stop] for r in chunks]
    b_last_cols = jnp.concatenate([jnp.broadcast_to(b, (HK, c)) for b in b_last], axis=1)

    vb = v.astype(BF16)
    k_dec = (k_t * jnp.exp(b_last_cols - bc_t)).astype(BF16)
    upd_all = []
    for n, r in enumerate(chunks):
        upd_all.append(jnp.dot(k_dec[:, r], vb[r, :], preferred_element_type=F32))
        if n == half - 1:
            yield
    yield
    states = []
    for n in range(len(chunks)):
        states.append(st)
        upd = cat([upd_all[n][hd * GLA_DK:(hd + 1) * GLA_DK, hd * GLA_DV:(hd + 1) * GLA_DV]
                   for hd in range(GLA_HEADS)])
        st = st * jnp.exp(b_last[n]) + upd

    qb = q.astype(BF16)
    q1 = (q * jnp.exp(la)).astype(BF16)
    kb = k_t.astype(BF16)
    att = []
    for n, r in enumerate(chunks):
        row = []
        for p, cols in enumerate(pairs):
            x = jnp.dot(cat([qb[r, cols], q1[r, cols]]), key_weights(kb, r, p), preferred_element_type=F32)
            row.append(jnp.where(lvl == 1, x[c:, :], jnp.where(lvl == 0, x[:c, :], 0.0)))
        att.append(row)
        if n == half - 1:
            yield
    yield
    for lv in range(2, N_LEVELS):
        el = cat([en[(lv - 2) * c:(lv - 1) * c, :] for en in e])
        ql = (q * jnp.exp(jnp.minimum(el, 0.0))).astype(BF16)
        kl = (k_t * jnp.exp(jnp.minimum(-el.T, 0.0))).astype(BF16)
        for n, r in enumerate(chunks):
            for p, cols in enumerate(pairs):
                x = jnp.dot(ql[r, cols], key_weights(kl, r, p), preferred_element_type=F32)
                att[n][p] = jnp.where(lvl == lv, x, att[n][p])
        yield

    q_in = (q * jnp.exp(bc)).astype(BF16)
    o = []
    for n, r in enumerate(chunks):
        sb = states[n].astype(BF16)
        row = []
        for p, cols in enumerate(pairs):
            h0 = 2 * p
            s_w = _pair_diag(sb[h0 * GLA_DK:(h0 + 1) * GLA_DK, :], sb[(h0 + 1) * GLA_DK:(h0 + 2) * GLA_DK, :])
            v_w = _pair_diag(vb[r, h0 * GLA_DV:(h0 + 1) * GLA_DV], vb[r, (h0 + 1) * GLA_DV:(h0 + 2) * GLA_DV])
            row.append(jnp.dot(q_in[r, cols], s_w, preferred_element_type=F32)
                       + jnp.dot(att[n][p].astype(BF16), v_w, preferred_element_type=F32))
        o.append(jnp.concatenate(row, axis=1))
        if n == half - 1:
            yield
    return cat(o), st


def _mixer_prompt_body(tm, h_ref, ng_ref, wa_ref, wr_ref, wgu_ref, bg_ref, wc_ref, gng_ref, cw_ref,
                       wout_ref, mst_ref, lvl_ref, o_ref, sg_ref, sc_ref, st_scr, carry_scr):
    j = pl.program_id(1)

    @pl.when(j == 0)
    def _():
        st_scr[...] = jnp.zeros_like(st_scr)
        carry_scr[...] = jnp.zeros_like(carry_scr)

    def inputs(h):
        return _mixer_inputs(h, ng_ref, wa_ref, wr_ref, wgu_ref, bg_ref, wc_ref)

    def gla(q, k, v, la, st):
        return _gla_tile(q, k, v, la, st, mst_ref[...], lvl_ref[...])

    def output(h, og, g, cb, uc, tail):
        return _mixer_output(h, og, g, _short_conv(cb, uc, tail, cw_ref), ng_ref, gng_ref, wout_ref)

    h_a = h_ref[0:tm, :]
    h_b = h_ref[tm:2 * tm, :]
    q, k, v, g_a, la, cb_a, cc, ch = _finish(inputs(h_a))
    uc_a = cc * ch
    (q, k, v, g_b, la, cb_b, cc, ch), (og_a, st) = _run_together(inputs(h_b), gla(q, k, v, la, st_scr[...]))
    uc_b = cc * ch
    (og_b, st), out_a = _run_together(gla(q, k, v, la, st), output(h_a, og_a, g_a, cb_a, uc_a, carry_scr[...]))
    o_ref[0:tm, :] = out_a
    o_ref[tm:2 * tm, :] = _finish(output(h_b, og_b, g_b, cb_b, uc_b, uc_a[tm - 8:tm, :]))
    st_scr[...] = st
    carry_scr[...] = uc_b[tm - 8:tm, :]

    @pl.when(j == pl.num_programs(1) - 1)
    def _():
        sg_ref[0] = st
        sc_ref[0] = carry_scr[6:8, :]


def _mixer_prompt(h, batch, seq, tm, ng, wa, wr, wgu, bg, wc, gng, cw, wout):
    assert seq % (2 * tm) == 0 and tm % (2 * CHUNK) == 0
    nt = seq // (2 * tm)
    mst, lvl = _chunk_constants()
    consts = [ng, wa, wr, wgu, bg, wc, gng, cw, wout, mst, lvl]
    row_spec = pl.BlockSpec((2 * tm, D_MODEL), lambda b, j: (b * nt + j, 0))
    est = (8 * _nbytes((tm, D_MODEL), F32) + sum(_nbytes(c.shape, c.dtype) for c in consts)
           + 4 * _nbytes((tm, 3 * HV), F32) + 2 * _nbytes((tm, 3 * HK + 2 * HV), F32)
           + 8 * _nbytes((tm, D_MODEL), F32))
    return pl.pallas_call(
        functools.partial(_mixer_prompt_body, tm),
        grid=(batch, nt),
        in_specs=[row_spec] + [_resident(c.shape) for c in consts],
        out_specs=[row_spec,
                   pl.BlockSpec((1, HK, GLA_DV), lambda b, j: (b, 0, 0)),
                   pl.BlockSpec((1, CONV_K - 1, CONV_WIDTH), lambda b, j: (b, 0, 0))],
        out_shape=[jax.ShapeDtypeStruct((batch * seq, D_MODEL), F32),
                   jax.ShapeDtypeStruct((batch, HK, GLA_DV), F32),
                   jax.ShapeDtypeStruct((batch, CONV_K - 1, CONV_WIDTH), F32)],
        scratch_shapes=[pltpu.VMEM((HK, GLA_DV), F32), pltpu.VMEM((8, CONV_WIDTH), F32)],
        compiler_params=pltpu.CompilerParams(
            dimension_semantics=("arbitrary", "arbitrary"), vmem_limit_bytes=_vmem_limit(est)),
        name="mixer_prompt",
    )(h, *consts)


def _mixer_sample_body(nb, bs, h_ref, s_ref, c0_ref, ng_ref, wa_ref, wr_ref, wgu_ref, bg_ref, wc_ref,
                       gng_ref, cw_ref, wout_ref, o_ref, so_ref, co_ref,
                       q_scr, k_scr, la3_scr, v_scr, g_scr, oc_scr, og_scr):
    i = pl.program_id(0)

    @pl.when(i == 0)
    def _():
        q, k, v, g, la, cb, cc, ch = _finish(
            _mixer_inputs(h_ref[...], ng_ref, wa_ref, wr_ref, wgu_ref, bg_ref, wc_ref))
        q_scr[...] = q.astype(BF16)
        k_scr[...] = k.astype(BF16)
        la3_scr[...] = jnp.concatenate(_split3(la), axis=0)
        v_scr[...] = v
        g_scr[...] = g
        uc = cc * ch
        c0 = c0_ref[...]
        old2 = c0[:, 0:CONV_WIDTH]
        old1 = c0[:, CONV_WIDTH:2 * CONV_WIDTH]
        y = cw_ref[0:1, :] * old2 + cw_ref[1:2, :] * old1 + cw_ref[2:3, :] * uc
        oc_scr[...] = cb * y
        co_ref[...] = jnp.concatenate([old1, uc], axis=1)

    first = pl.multiple_of(i * bs, bs)
    seq_row = lax.broadcasted_iota(jnp.int32, (nb, bs * LANES), 0)
    seq_col = lax.broadcasted_iota(jnp.int32, (nb, bs * LANES), 1) // LANES + first
    pick = jnp.where(seq_row == seq_col, 1.0, 0.0).astype(BF16)
    pick3 = jnp.concatenate([pick, pick, pick], axis=0)
    decay_t = jnp.exp(lax.dot_general(la3_scr[...], pick3, TN_DIMS, preferred_element_type=F32))
    k_t = lax.dot_general(k_scr[...], pick, TN_DIMS, preferred_element_type=F32)
    q_t = lax.dot_general(q_scr[...], pick, TN_DIMS, preferred_element_type=F32)
    v_blk = v_scr[pl.ds(first, bs), :]
    o_rows = []
    for t in range(bs):
        cols = slice(t * LANES, (t + 1) * LANES)
        v_b = jnp.concatenate(
            [jnp.broadcast_to(v_blk[t:t + 1, hd * GLA_DV:(hd + 1) * GLA_DV], (GLA_DK, GLA_DV))
             for hd in range(GLA_HEADS)], axis=0)
        s_new = decay_t[:, cols] * s_ref[t] + k_t[:, cols] * v_b
        so_ref[t] = s_new
        qs = q_t[:, cols] * s_new
        o_rows.append(jnp.concatenate(
            [jnp.sum(qs[hd * GLA_DK:(hd + 1) * GLA_DK, :], axis=0, keepdims=True)
             for hd in range(GLA_HEADS)], axis=1))
    og_scr[pl.ds(first, bs), :] = jnp.concatenate(o_rows, axis=0)

    @pl.when(i == pl.num_programs(0) - 1)
    def _():
        o_ref[...] = _finish(
            _mixer_output(h_ref[...], og_scr[...], g_scr[...], oc_scr[...], ng_ref, gng_ref, wout_ref))


def _mixer_sample(h, s0, c0, bs, ng, wa, wr, wgu, bg, wc, gng, cw, wout):
    nb = h.shape[0]
    assert nb == LANES and nb % bs == 0
    consts = [ng, wa, wr, wgu, bg, wc, gng, cw, wout]
    state_spec = pl.BlockSpec((bs, HK, GLA_DV), lambda i: (i, 0, 0))
    est = (sum(_nbytes(c.shape, c.dtype) for c in consts) + 4 * _nbytes((bs, HK, GLA_DV), F32)
           + 12 * _nbytes((nb, D_MODEL), F32) + 2 * _nbytes((nb, 3 * HV), F32))
    return pl.pallas_call(
        functools.partial(_mixer_sample_body, nb, bs),
        grid=(nb // bs,),
        in_specs=[_resident(h.shape), state_spec, _resident(c0.shape)] + [_resident(c.shape) for c in consts],
        out_specs=[pl.BlockSpec((nb, D_MODEL), lambda i: (0, 0)), state_spec,
                   pl.BlockSpec(c0.shape, lambda i: (0, 0))],
        out_shape=[jax.ShapeDtypeStruct((nb, D_MODEL), F32),
                   jax.ShapeDtypeStruct(s0.shape, F32),
                   jax.ShapeDtypeStruct(c0.shape, F32)],
        scratch_shapes=[pltpu.VMEM((nb, HK), BF16), pltpu.VMEM((nb, HK), BF16), pltpu.VMEM((3 * nb, HK), BF16),
                        pltpu.VMEM((nb, HV), F32), pltpu.VMEM((nb, HV), F32), pltpu.VMEM((nb, CONV_WIDTH), F32),
                        pltpu.VMEM((nb, HV), F32)],
        compiler_params=pltpu.CompilerParams(
            dimension_semantics=("arbitrary",), vmem_limit_bytes=_vmem_limit(est)),
        name="mixer_sample",
    )(h, s0, c0, *consts)


TM_PROMPT = 512
FFN_SIZES = (768, 256)
FFN_PLE_SIZES = (256, 256, 256, 256)
BS_SAMPLE = 32


def kernel(x_prompt, x_sample, state_gla, state_conv, p_prompt, p_sample, norm_g, w_in, w_gate_up, b_gate,
           gla_norm_g, conv_w, w_out, ffn1_gate, ffn1_up, ffn1_down, ffn2_gate, ffn2_up, ffn2_down,
           w_ple_proj, w_ple_gate):
    batch, seq, _ = x_prompt.shape
    nb = x_sample.shape[0]
    assert norm_g.shape[0] == 1 and x_sample.shape[1] == 1

    ng = norm_g[0]
    f1 = (_gain_into_rows(ffn1_gate[0], ng[0]), _gain_into_rows(ffn1_up[0], ng[0]), ffn1_down[0].astype(BF16))
    wgu = jnp.pad(w_gate_up[0], ((0, LANES - GLA_RANK), (0, 0))).astype(BF16)
    bg = b_gate[0].reshape(1, HK)
    gng = gla_norm_g[0].reshape(1, GLA_DV)
    cw = conv_w[0]
    r0 = 2 * HK + 2 * HV
    r1 = r0 + GLA_RANK
    jobs = [
        _PrepJob(ffn2_gate[0], gain=4), _PrepJob(ffn2_up[0], gain=4), _PrepJob(ffn2_down[0]),
        _PrepJob(jnp.transpose(w_in[0]), gain=2, transposed=True,
                 pieces=[(0, r0, r0), (r0, GLA_RANK, LANES), (r1, w_in.shape[2] - r1, w_in.shape[2] - r1)]),
        _PrepJob(w_out[0]), _PrepJob(w_ple_gate[0], gain=6), _PrepJob(w_ple_proj[0]),
    ]

    xp = x_prompt.reshape(batch * seq, D_MODEL)
    pp = p_prompt[0].reshape(batch * seq, PLE_DIM)
    hp, f2g, f2u, f2d, wa, wr, wc, wout, wpg, wpp = _ffn(xp, ng, *f1, 1, FFN_SIZES, prep=(ng.T, jobs))
    f2 = (f2g, f2u, f2d)
    mix_w = (ng, wa, wr, wgu, bg, wc, gng, cw, wout)
    hp, sg_p, sc_p = _mixer_prompt(hp, batch, seq, TM_PROMPT, *mix_w)
    hp = _ffn(hp, ng, *f2, 5, FFN_PLE_SIZES, ple_args=(pp, wpp, wpg))

    xs = x_sample.reshape(nb, D_MODEL)
    ps = p_sample[0].reshape(nb, PLE_DIM)
    s0 = state_gla[0].reshape(nb, HK, GLA_DV)
    c0 = state_conv[0].reshape(nb, (CONV_K - 1) * CONV_WIDTH)
    hs = _ffn(xs, ng, *f1, 1, (nb,))
    hs, sg_s, sc_s = _mixer_sample(hs, s0, c0, BS_SAMPLE, *mix_w)
    hs = _ffn(hs, ng, *f2, 5, (nb,), ple_args=(ps, wpp, wpg))

    return (hp.reshape(batch, seq, D_MODEL),
            hs.reshape(nb, 1, D_MODEL),
            sg_p.reshape(1, batch, GLA_HEADS, GLA_DK, GLA_DV),
            sc_p.reshape(1, batch, CONV_K - 1, CONV_WIDTH),
            sg_s.reshape(1, nb, GLA_HEADS, GLA_DK, GLA_DV),
            sc_s.reshape(1, nb, CONV_K - 1, CONV_WIDTH))
```

```python
import functools

import numpy as np
import jax
import jax.numpy as jnp
from jax import lax
from jax.experimental import pallas as pl
from jax.experimental.pallas import tpu as pltpu

F32 = jnp.float32
BF16 = jnp.bfloat16

D_MODEL = 1024
PLE_DIM = 256
GLA_HEADS = 4
GLA_DK = 64
GLA_DV = 128
HK = GLA_HEADS * GLA_DK
HV = GLA_HEADS * GLA_DV
GLA_RANK = 16
GLA_GATE_NORM = 16.0
CONV_WIDTH = 512
CONV_K = 3
D_FF = 2816
EPS = 1e-6

LANES = 128
MXU_COLS = 256
CHUNK = 128
N_LEVELS = 8
VMEM_CAP = 56 * 1024 * 1024

TN_DIMS = (((0,), (0,)), ((), ()))


def _inv_rms(x):
    return lax.rsqrt(jnp.mean(x * x, axis=-1, keepdims=True) + EPS)


def _rms(x, g):
    return x * _inv_rms(x) * g


def _gain_into_rows(w, g):
    return (g[:, None] * w).astype(BF16)


def _sigmoid(x):
    return 1.0 / (1.0 + jnp.exp(-x))


def _log_sigmoid(x):
    return jnp.minimum(x, 0.0) - jnp.log(1.0 + jnp.exp(-jnp.abs(x)))


def _zero_from(x):
    bits = x.astype(jnp.int32)
    sixteen = jnp.full(x.shape, 16, jnp.int32)
    return lax.shift_right_logical(lax.shift_right_logical(bits, sixteen), sixteen).astype(F32)


def _split2(x):
    hi = x.astype(BF16)
    return hi, (x - hi.astype(F32)).astype(BF16)


def _split3(x):
    hi = x.astype(BF16)
    r1 = x - hi.astype(F32)
    mid = r1.astype(BF16)
    lo = (r1 - mid.astype(F32)).astype(BF16)
    return hi, mid, lo


def _resident(shape):
    nd = len(shape)
    return pl.BlockSpec(shape, lambda *_: (0,) * nd, pipeline_mode=pl.Buffered(1))


def _nbytes(shape, dtype):
    return int(np.prod(shape)) * jnp.dtype(dtype).itemsize


def _vmem_limit(est_bytes):
    return int(min(VMEM_CAP, max(est_bytes * 5 // 4, 16 * 1024 * 1024)))


class _PrepJob:
    def __init__(self, w, gain=None, pieces=None, transposed=False):
        self.w = w
        self.gain = gain
        self.transposed = transposed
        self.rows, self.cols = (w.shape[1], w.shape[0]) if transposed else w.shape
        self.pieces = pieces or [(0, self.cols, self.cols)]


def _rows_of_transposed(w_t, rb):
    cols = w_t.shape[0]
    whole = cols // LANES * LANES
    parts = [w_t[j:j + LANES, :].T for j in range(0, whole, LANES)]
    if cols > whole:
        parts.append(w_t[cols - LANES:cols, :].T[:, LANES - (cols - whole):])
    t = jnp.concatenate(parts, axis=1)
    which = pl.program_id(0) % (LANES // rb)
    w = t[0:rb, :]
    for j in range(1, LANES // rb):
        w = jnp.where(which == j, t[j * rb:(j + 1) * rb, :], w)
    return w


def _prep_rows(job, g_ref, w_ref, out_refs):
    w = w_ref[...]
    if job.transposed:
        w = _rows_of_transposed(w, job.rb)
    if job.gain is not None:
        w = w * g_ref[:, job.gain:job.gain + 1]
    for (c0, keep, width), out_ref in zip(job.pieces, out_refs):
        piece = w[:, c0:c0 + width]
        if keep < width:
            lane = lax.broadcasted_iota(jnp.int32, piece.shape, 1)
            piece = jnp.where(lane < keep, piece, 0.0)
        out_ref[...] = piece.astype(BF16)


def _ffn_body(post, ple, sizes, jobs, *refs):
    refs = list(refs)
    x_ref, ng_ref, wg_ref, wu_ref, wd_ref = refs[:5]
    del refs[:5]
    if ple:
        p_ref, wpp_ref, wpg_ref = refs[:3]
        del refs[:3]
    if jobs:
        g_ref = refs.pop(0)
        job_in = refs[:len(jobs)]
        del refs[:len(jobs)]
    o_ref = refs.pop(0)
    h_scr = refs.pop()
    for job, w_ref in zip(jobs, job_in if jobs else ()):
        outs = refs[:len(job.pieces)]
        del refs[:len(job.pieces)]
        _prep_rows(job, g_ref, w_ref, outs)

    starts = [sum(sizes[:s]) for s in range(len(sizes))]

    n_chunks = D_FF // MXU_COLS

    def gate_up(s, beside):
        x = x_ref[starts[s]:starts[s] + sizes[s], :]
        xb = x.astype(BF16)
        inv = _inv_rms(x)
        for c in range(n_chunks):
            sl = slice(c * MXU_COLS, (c + 1) * MXU_COLS)
            if c == n_chunks // 2 and beside is not None:
                inv = inv + _zero_from(beside()[0:sizes[s], 0:1])
            g = jnp.dot(xb, wg_ref[:, sl], preferred_element_type=F32) * inv
            u = jnp.dot(xb, wu_ref[:, sl], preferred_element_type=F32) * inv
            h_scr[s % 2, 0:sizes[s], sl] = (g * _sigmoid(g) * u).astype(BF16)

    def down(s):
        return jnp.dot(h_scr[s % 2, 0:sizes[s], :], wd_ref[...], preferred_element_type=F32)

    def epilogue(s, y):
        rows = slice(starts[s], starts[s] + sizes[s])
        h = x_ref[rows, :] + 0.5 * _rms(y, ng_ref[post:post + 1, :])
        if ple:
            gate = _sigmoid(jnp.dot(h.astype(BF16), wpg_ref[...], preferred_element_type=F32) * _inv_rms(h))
            proj = jnp.dot(p_ref[rows, :].astype(BF16), wpp_ref[...], preferred_element_type=F32)
            h = h + _rms(proj * gate, ng_ref[7:8, :])
        o_ref[rows, :] = h
        return h

    pending = None
    for s in range(len(sizes)):
        gate_up(s, pending)
        y = down(s)
        pending = functools.partial(epilogue, s, y)
    pending()


def _ffn(x, ng, wg, wu, wd, post, sizes, ple_args=None, prep=None):
    rows = x.shape[0]
    tm = sum(sizes)
    ts = max(sizes)
    assert rows % tm == 0 and all(sz % 16 == 0 for sz in sizes)
    assert all(a >= b for a, b in zip(sizes, sizes[1:]))
    steps = rows // tm
    ple = ple_args is not None
    row_spec = pl.BlockSpec((tm, D_MODEL), lambda i: (i, 0))
    in_specs = [row_spec, _resident(ng.shape), _resident(wg.shape), _resident(wu.shape), _resident(wd.shape)]
    args = [x, ng, wg, wu, wd]
    est = (4 * _nbytes((tm, D_MODEL), F32) + 3 * _nbytes(wg.shape, BF16)
           + 2 * _nbytes((ts, D_FF), BF16) + _nbytes((ts, D_MODEL), BF16)
           + 6 * _nbytes((ts, MXU_COLS), F32) + 6 * _nbytes((ts, D_MODEL), F32))
    if ple:
        p, wpp, wpg = ple_args
        in_specs += [pl.BlockSpec((tm, PLE_DIM), lambda i: (i, 0)), _resident(wpp.shape), _resident(wpg.shape)]
        args += [p, wpp, wpg]
        est += 2 * _nbytes((tm, PLE_DIM), F32) + _nbytes(wpp.shape, BF16) + _nbytes(wpg.shape, BF16)
        est += 3 * _nbytes((ts, D_MODEL), F32)
    jobs = []
    out_specs = [row_spec]
    out_shape = [jax.ShapeDtypeStruct((rows, D_MODEL), F32)]
    if prep is not None:
        gains, jobs = prep
        in_specs.append(pl.BlockSpec((gains.shape[0] // steps, gains.shape[1]), lambda i: (i, 0)))
        args.append(gains)
        for job in jobs:
            r, c = job.rows, job.cols
            rb = job.rb = r // steps
            assert r % steps == 0 and rb % 16 == 0
            assert job.gain is None or r == gains.shape[0]
            if job.transposed:
                assert LANES % rb == 0
                in_specs.append(pl.BlockSpec((c, LANES), lambda i, k=LANES // rb: (0, i // k)))
            else:
                in_specs.append(pl.BlockSpec((rb, c), lambda i: (i, 0)))
            args.append(job.w)
            est += 2 * _nbytes((rb, c), F32)
            for _, _, width in job.pieces:
                out_specs.append(pl.BlockSpec((rb, width), lambda i: (i, 0)))
                out_shape.append(jax.ShapeDtypeStruct((r, width), BF16))
                est += 2 * _nbytes((rb, width), BF16)
    res = pl.pallas_call(
        functools.partial(_ffn_body, post, ple, tuple(sizes), jobs),
        grid=(steps,),
        in_specs=in_specs,
        out_specs=out_specs,
        out_shape=out_shape,
        scratch_shapes=[pltpu.VMEM((2, ts, D_FF), BF16)],
        compiler_params=pltpu.CompilerParams(
            dimension_semantics=("arbitrary",), vmem_limit_bytes=_vmem_limit(est)),
        name="ffn_ple" if ple else "ffn",
    )(*args)
    return res if prep is not None else res[0]


def _run_together(*stages):
    results = [None] * len(stages)
    live = list(enumerate(stages))
    while live:
        still = []
        for idx, gen in live:
            try:
                next(gen)
                still.append((idx, gen))
            except StopIteration as stop:
                results[idx] = stop.value
        live = still
    return results


def _finish(stages):
    return _run_together(stages)[0]


def _mixer_inputs(h, ng_ref, wa_ref, wr_ref, wgu_ref, bg_ref, wc_ref):
    hb = h.astype(BF16)
    inv = _inv_rms(h)
    yield
    r = jnp.dot(hb, wr_ref[...], preferred_element_type=F32) * inv
    logit = jnp.dot(r.astype(BF16), wgu_ref[...], preferred_element_type=F32) + bg_ref[...]
    la = _log_sigmoid(logit) * (1.0 / GLA_GATE_NORM)
    yield
    cols = []
    for w_ref in (wa_ref, wc_ref):
        for c in range(0, w_ref.shape[1], MXU_COLS):
            cols.append(jnp.dot(hb, w_ref[:, c:c + MXU_COLS], preferred_element_type=F32) * inv)
            yield
    z = jnp.concatenate(cols, axis=1)
    q = z[:, 0:HK] * (GLA_DK ** -0.5)
    k = z[:, HK:2 * HK]
    v = z[:, 2 * HK:2 * HK + HV]
    g = z[:, 2 * HK + HV:2 * HK + 2 * HV]
    zc = z[:, 2 * HK + 2 * HV:]
    cb = zc[:, 0:CONV_WIDTH]
    cc = zc[:, CONV_WIDTH:2 * CONV_WIDTH]
    ch = zc[:, 2 * CONV_WIDTH:3 * CONV_WIDTH]
    return q, k, v, g, la, cb, cc, ch


def _short_conv(cb, uc, tail, cw_ref):
    row = lax.broadcasted_iota(jnp.int32, uc.shape, 0)
    c1 = tail[7:8, :]
    c2 = tail[6:7, :]
    prev1 = jnp.where(row == 0, c1, pltpu.roll(uc, 1, axis=0))
    prev2 = jnp.where(row == 0, c2, jnp.where(row == 1, c1, pltpu.roll(uc, 2, axis=0)))
    return cb * (cw_ref[0:1, :] * prev2 + cw_ref[1:2, :] * prev1 + cw_ref[2:3, :] * uc)


def _mixer_output(h, o, g, oc, ng_ref, gng_ref, wout_ref):
    heads = []
    for hd in range(GLA_HEADS):
        sl = slice(hd * GLA_DV, (hd + 1) * GLA_DV)
        heads.append(_rms(o[:, sl], gng_ref[...]))
    on = jnp.concatenate(heads, axis=1) * (g * _sigmoid(g))
    cat = jnp.concatenate([on.astype(BF16), oc.astype(BF16)], axis=1)
    yield
    cols = []
    for c in range(0, wout_ref.shape[1], MXU_COLS):
        cols.append(jnp.dot(cat, wout_ref[:, c:c + MXU_COLS], preferred_element_type=F32))
        yield
    return h + _rms(jnp.concatenate(cols, axis=1), ng_ref[3:4, :])


def _chunk_constants():
    t = np.arange(CHUNK)[:, None]
    s = np.arange(CHUNK)[None, :]
    blocks = []
    for j in range(1, N_LEVELS - 1):
        half = 1 << j
        mid = (t // (2 * half)) * (2 * half) + half
        blocks.append((s <= t).astype(np.float32) - (s < mid).astype(np.float32))
    blocks.append((s <= t).astype(np.float32))
    m = np.concatenate(blocks, axis=0)
    mst = np.concatenate([m, m], axis=1)
    x = np.bitwise_xor(t, s)
    top = np.floor(np.log2(np.maximum(x, 1))).astype(np.int32) + 1
    lvl = np.where(s > t, -1, np.where(s == t, 0, top)).astype(np.int32)
    return jnp.asarray(mst, BF16), jnp.asarray(np.tile(lvl, (1, 2)), jnp.int32)


def _pair_diag(a, b):
    z = jnp.zeros_like(a)
    return jnp.concatenate([jnp.concatenate([a, z], axis=1), jnp.concatenate([z, b], axis=1)], axis=0)


def _gla_tile(q, k, v, la, st, mst, lvl):
    c = CHUNK
    chunks = [slice(r, r + c) for r in range(0, q.shape[0], c)]
    pairs = [slice(p * 2 * GLA_DK, (p + 1) * 2 * GLA_DK) for p in range(GLA_HEADS // 2)]
    cat = lambda xs: jnp.concatenate(xs, axis=0)

    def key_weights(kt, r, p):
        lo = 2 * p * GLA_DK
        return _pair_diag(kt[lo:lo + GLA_DK, r], kt[lo + GLA_DK:lo + 2 * GLA_DK, r])

    hi, lo = _split2(la)
    e = []
    for n, r in enumerate(chunks):
        e.append(jnp.dot(mst, cat([hi[r, :], lo[r, :]]), preferred_element_type=F32))
    yield
    bc = cat([en[(N_LEVELS - 2) * c:(N_LEVELS - 1) * c, :] for en in e])
    k_t = k.T
    bc_t = bc.T
    b_last = [bc_t[:, r.stop - 1:r.stop] for r in chunks]
    b_last_cols = jnp.concatenate([jnp.broadcast_to(b, (HK, c)) for b in b_last], axis=1)

    qb = q.astype(BF16)
    q1 = (q * jnp.exp(la)).astype(BF16)
    kb = k_t.astype(BF16)
    att = []
    for n, r in enumerate(chunks):
        row = []
        for p, cols in enumerate(pairs):
            x = jnp.dot(cat([qb[r, cols], q1[r, cols]]), key_weights(kb, r, p), preferred_element_type=F32)
            row.append(jnp.where(lvl == 1, x[c:, :], jnp.where(lvl == 0, x[:c, :], 0.0)))
        att.append(row)
    yield
    for lv in range(2, N_LEVELS):
        el = cat([en[(lv - 2) * c:(lv - 1) * c, :] for en in e])
        ql = (q * jnp.exp(jnp.minimum(el, 0.0))).astype(BF16)
        kl = (k_t * jnp.exp(jnp.minimum(-el.T, 0.0))).astype(BF16)
        for n, r in enumerate(chunks):
            for p, cols in enumerate(pairs):
                x = jnp.dot(ql[r, cols], key_weights(kl, r, p), preferred_element_type=F32)
                att[n][p] = jnp.where(lvl == lv, x, att[n][p])
        yield

    vb = v.astype(BF16)
    k_dec = (k_t * jnp.exp(b_last_cols - bc_t)).astype(BF16)
    upd_all = []
    for n, r in enumerate(chunks):
        upd_all.append(jnp.dot(k_dec[:, r], vb[r, :], preferred_element_type=F32))
    yield
    states = []
    for n in range(len(chunks)):
        states.append(st)
        upd = cat([upd_all[n][hd * GLA_DK:(hd + 1) * GLA_DK, hd * GLA_DV:(hd + 1) * GLA_DV]
                   for hd in range(GLA_HEADS)])
        st = st * jnp.exp(b_last[n]) + upd

    q_in = (q * jnp.exp(bc)).astype(BF16)
    o = []
    for n, r in enumerate(chunks):
        sb = states[n].astype(BF16)
        row = []
        for p, cols in enumerate(pairs):
            h0 = 2 * p
            s_w = _pair_diag(sb[h0 * GLA_DK:(h0 + 1) * GLA_DK, :], sb[(h0 + 1) * GLA_DK:(h0 + 2) * GLA_DK, :])
            v_w = _pair_diag(vb[r, h0 * GLA_DV:(h0 + 1) * GLA_DV], vb[r, (h0 + 1) * GLA_DV:(h0 + 2) * GLA_DV])
            row.append(jnp.dot(q_in[r, cols], s_w, preferred_element_type=F32)
                       + jnp.dot(att[n][p].astype(BF16), v_w, preferred_element_type=F32))
        o.append(jnp.concatenate(row, axis=1))
    return cat(o), st


def _mixer_prompt_body(tm, h_ref, ng_ref, wa_ref, wr_ref, wgu_ref, bg_ref, wc_ref, gng_ref, cw_ref,
                       wout_ref, mst_ref, lvl_ref, o_ref, sg_ref, sc_ref, st_scr, carry_scr):
    j = pl.program_id(1)

    @pl.when(j == 0)
    def _():
        st_scr[...] = jnp.zeros_like(st_scr)
        carry_scr[...] = jnp.zeros_like(carry_scr)

    def inputs(h):
        return _mixer_inputs(h, ng_ref, wa_ref, wr_ref, wgu_ref, bg_ref, wc_ref)

    def gla(q, k, v, la, st):
        return _gla_tile(q, k, v, la, st, mst_ref[...], lvl_ref[...])

    def output(h, og, g, cb, uc, tail):
        return _mixer_output(h, og, g, _short_conv(cb, uc, tail, cw_ref), ng_ref, gng_ref, wout_ref)

    h_a = h_ref[0:tm, :]
    h_b = h_ref[tm:2 * tm, :]
    q, k, v, g_a, la, cb_a, cc, ch = _finish(inputs(h_a))
    uc_a = cc * ch
    (q, k, v, g_b, la, cb_b, cc, ch), (og_a, st) = _run_together(inputs(h_b), gla(q, k, v, la, st_scr[...]))
    uc_b = cc * ch
    (og_b, st), out_a = _run_together(gla(q, k, v, la, st), output(h_a, og_a, g_a, cb_a, uc_a, carry_scr[...]))
    o_ref[0:tm, :] = out_a
    o_ref[tm:2 * tm, :] = _finish(output(h_b, og_b, g_b, cb_b, uc_b, uc_a[tm - 8:tm, :]))
    st_scr[...] = st
    carry_scr[...] = uc_b[tm - 8:tm, :]

    @pl.when(j == pl.num_programs(1) - 1)
    def _():
        sg_ref[0] = st
        sc_ref[0] = carry_scr[6:8, :]


def _mixer_prompt(h, batch, seq, tm, ng, wa, wr, wgu, bg, wc, gng, cw, wout):
    assert seq % (2 * tm) == 0 and tm % (2 * CHUNK) == 0
    nt = seq // (2 * tm)
    mst, lvl = _chunk_constants()
    consts = [ng, wa, wr, wgu, bg, wc, gng, cw, wout, mst, lvl]
    row_spec = pl.BlockSpec((2 * tm, D_MODEL), lambda b, j: (b * nt + j, 0))
    est = (8 * _nbytes((tm, D_MODEL), F32) + sum(_nbytes(c.shape, c.dtype) for c in consts)
           + 4 * _nbytes((tm, 3 * HV), F32) + 2 * _nbytes((tm, 3 * HK + 2 * HV), F32)
           + 8 * _nbytes((tm, D_MODEL), F32))
    return pl.pallas_call(
        functools.partial(_mixer_prompt_body, tm),
        grid=(batch, nt),
        in_specs=[row_spec] + [_resident(c.shape) for c in consts],
        out_specs=[row_spec,
                   pl.BlockSpec((1, HK, GLA_DV), lambda b, j: (b, 0, 0)),
                   pl.BlockSpec((1, CONV_K - 1, CONV_WIDTH), lambda b, j: (b, 0, 0))],
        out_shape=[jax.ShapeDtypeStruct((batch * seq, D_MODEL), F32),
                   jax.ShapeDtypeStruct((batch, HK, GLA_DV), F32),
                   jax.ShapeDtypeStruct((batch, CONV_K - 1, CONV_WIDTH), F32)],
        scratch_shapes=[pltpu.VMEM((HK, GLA_DV), F32), pltpu.VMEM((8, CONV_WIDTH), F32)],
        compiler_params=pltpu.CompilerParams(
            dimension_semantics=("arbitrary", "arbitrary"), vmem_limit_bytes=_vmem_limit(est)),
        name="mixer_prompt",
    )(h, *consts)


def _mixer_sample_body(nb, bs, h_ref, s_ref, c0_ref, ng_ref, wa_ref, wr_ref, wgu_ref, bg_ref, wc_ref,
                       gng_ref, cw_ref, wout_ref, o_ref, so_ref, co_ref,
                       q_scr, k_scr, la3_scr, v_scr, g_scr, oc_scr, og_scr):
    i = pl.program_id(0)

    @pl.when(i == 0)
    def _():
        q, k, v, g, la, cb, cc, ch = _finish(
            _mixer_inputs(h_ref[...], ng_ref, wa_ref, wr_ref, wgu_ref, bg_ref, wc_ref))
        q_scr[...] = q.astype(BF16)
        k_scr[...] = k.astype(BF16)
        la3_scr[...] = jnp.concatenate(_split3(la), axis=0)
        v_scr[...] = v
        g_scr[...] = g
        uc = cc * ch
        c0 = c0_ref[...]
        old2 = c0[:, 0:CONV_WIDTH]
        old1 = c0[:, CONV_WIDTH:2 * CONV_WIDTH]
        y = cw_ref[0:1, :] * old2 + cw_ref[1:2, :] * old1 + cw_ref[2:3, :] * uc
        oc_scr[...] = cb * y
        co_ref[...] = jnp.concatenate([old1, uc], axis=1)

    first = pl.multiple_of(i * bs, bs)
    seq_row = lax.broadcasted_iota(jnp.int32, (nb, bs * LANES), 0)
    seq_col = lax.broadcasted_iota(jnp.int32, (nb, bs * LANES), 1) // LANES + first
    pick = jnp.where(seq_row == seq_col, 1.0, 0.0).astype(BF16)
    pick3 = jnp.concatenate([pick, pick, pick], axis=0)
    decay_t = jnp.exp(lax.dot_general(la3_scr[...], pick3, TN_DIMS, preferred_element_type=F32))
    k_t = lax.dot_general(k_scr[...], pick, TN_DIMS, preferred_element_type=F32)
    q_t = lax.dot_general(q_scr[...], pick, TN_DIMS, preferred_element_type=F32)
    v_blk = v_scr[pl.ds(first, bs), :]
    o_rows = []
    for t in range(bs):
        cols = slice(t * LANES, (t + 1) * LANES)
        v_b = jnp.concatenate(
            [jnp.broadcast_to(v_blk[t:t + 1, hd * GLA_DV:(hd + 1) * GLA_DV], (GLA_DK, GLA_DV))
             for hd in range(GLA_HEADS)], axis=0)
        s_new = decay_t[:, cols] * s_ref[t] + k_t[:, cols] * v_b
        so_ref[t] = s_new
        qs = q_t[:, cols] * s_new
        o_rows.append(jnp.concatenate(
            [jnp.sum(qs[hd * GLA_DK:(hd + 1) * GLA_DK, :], axis=0, keepdims=True)
             for hd in range(GLA_HEADS)], axis=1))
    og_scr[pl.ds(first, bs), :] = jnp.concatenate(o_rows, axis=0)

    @pl.when(i == pl.num_programs(0) - 1)
    def _():
        o_ref[...] = _finish(
            _mixer_output(h_ref[...], og_scr[...], g_scr[...], oc_scr[...], ng_ref, gng_ref, wout_ref))


def _mixer_sample(h, s0, c0, bs, ng, wa, wr, wgu, bg, wc, gng, cw, wout):
    nb = h.shape[0]
    assert nb == LANES and nb % bs == 0
    consts = [ng, wa, wr, wgu, bg, wc, gng, cw, wout]
    state_spec = pl.BlockSpec((bs, HK, GLA_DV), lambda i: (i, 0, 0))
    est = (sum(_nbytes(c.shape, c.dtype) for c in consts) + 4 * _nbytes((bs, HK, GLA_DV), F32)
           + 12 * _nbytes((nb, D_MODEL), F32) + 2 * _nbytes((nb, 3 * HV), F32))
    return pl.pallas_call(
        functools.partial(_mixer_sample_body, nb, bs),
        grid=(nb // bs,),
        in_specs=[_resident(h.shape), state_spec, _resident(c0.shape)] + [_resident(c.shape) for c in consts],
        out_specs=[pl.BlockSpec((nb, D_MODEL), lambda i: (0, 0)), state_spec,
                   pl.BlockSpec(c0.shape, lambda i: (0, 0))],
        out_shape=[jax.ShapeDtypeStruct((nb, D_MODEL), F32),
                   jax.ShapeDtypeStruct(s0.shape, F32),
                   jax.ShapeDtypeStruct(c0.shape, F32)],
        scratch_shapes=[pltpu.VMEM((nb, HK), BF16), pltpu.VMEM((nb, HK), BF16), pltpu.VMEM((3 * nb, HK), BF16),
                        pltpu.VMEM((nb, HV), F32), pltpu.VMEM((nb, HV), F32), pltpu.VMEM((nb, CONV_WIDTH), F32),
                        pltpu.VMEM((nb, HV), F32)],
        compiler_params=pltpu.CompilerParams(
            dimension_semantics=("arbitrary",), vmem_limit_bytes=_vmem_limit(est)),
        name="mixer_sample",
    )(h, s0, c0, *consts)


TM_PROMPT = 512
FFN_SIZES = (768, 256)
FFN_PLE_SIZES = (256, 256, 256, 256)
BS_SAMPLE = 32


def kernel(x_prompt, x_sample, state_gla, state_conv, p_prompt, p_sample, norm_g, w_in, w_gate_up, b_gate,
           gla_norm_g, conv_w, w_out, ffn1_gate, ffn1_up, ffn1_down, ffn2_gate, ffn2_up, ffn2_down,
           w_ple_proj, w_ple_gate):
    batch, seq, _ = x_prompt.shape
    nb = x_sample.shape[0]
    assert norm_g.shape[0] == 1 and x_sample.shape[1] == 1

    ng = norm_g[0]
    f1 = (_gain_into_rows(ffn1_gate[0], ng[0]), _gain_into_rows(ffn1_up[0], ng[0]), ffn1_down[0].astype(BF16))
    wgu = jnp.pad(w_gate_up[0], ((0, LANES - GLA_RANK), (0, 0))).astype(BF16)
    bg = b_gate[0].reshape(1, HK)
    gng = gla_norm_g[0].reshape(1, GLA_DV)
    cw = conv_w[0]
    r0 = 2 * HK + 2 * HV
    r1 = r0 + GLA_RANK
    jobs = [
        _PrepJob(ffn2_gate[0], gain=4), _PrepJob(ffn2_up[0], gain=4), _PrepJob(ffn2_down[0]),
        _PrepJob(jnp.transpose(w_in[0]), gain=2, transposed=True,
                 pieces=[(0, r0, r0), (r0, GLA_RANK, LANES), (r1, w_in.shape[2] - r1, w_in.shape[2] - r1)]),
        _PrepJob(w_out[0]), _PrepJob(w_ple_gate[0], gain=6), _PrepJob(w_ple_proj[0]),
    ]

    xp = x_prompt.reshape(batch * seq, D_MODEL)
    pp = p_prompt[0].reshape(batch * seq, PLE_DIM)
    hp, f2g, f2u, f2d, wa, wr, wc, wout, wpg, wpp = _ffn(xp, ng, *f1, 1, FFN_SIZES, prep=(ng.T, jobs))
    f2 = (f2g, f2u, f2d)
    mix_w = (ng, wa, wr, wgu, bg, wc, gng, cw, wout)
    hp, sg_p, sc_p = _mixer_prompt(hp, batch, seq, TM_PROMPT, *mix_w)
    hp = _ffn(hp, ng, *f2, 5, FFN_PLE_SIZES, ple_args=(pp, wpp, wpg))

    xs = x_sample.reshape(nb, D_MODEL)
    ps = p_sample[0].reshape(nb, PLE_DIM)
    s0 = state_gla[0].reshape(nb, HK, GLA_DV)
    c0 = state_conv[0].reshape(nb, (CONV_K - 1) * CONV_WIDTH)
    hs = _ffn(xs, ng, *f1, 1, (nb,))
    hs, sg_s, sc_s = _mixer_sample(hs, s0, c0, BS_SAMPLE, *mix_w)
    hs = _ffn(hs, ng, *f2, 5, (nb,), ple_args=(ps, wpp, wpg))

    return (hp.reshape(batch, seq, D_MODEL),
            hs.reshape(nb, 1, D_MODEL),
            sg_p.reshape(1, batch, GLA_HEADS, GLA_DK, GLA_DV),
            sc_p.reshape(1, batch, CONV_K - 1, CONV_WIDTH),
            sg_s.reshape(1, nb, GLA_HEADS, GLA_DK, GLA_DV),
            sc_s.reshape(1, nb, CONV_K - 1, CONV_WIDTH))
```

```python
import functools

import numpy as np
import jax
import jax.numpy as jnp
from jax import lax
from jax.experimental import pallas as pl
from jax.experimental.pallas import tpu as pltpu

F32 = jnp.float32
BF16 = jnp.bfloat16

D_MODEL = 1024
PLE_DIM = 256
GLA_HEADS = 4
GLA_DK = 64
GLA_DV = 128
HK = GLA_HEADS * GLA_DK
HV = GLA_HEADS * GLA_DV
GLA_RANK = 16
GLA_GATE_NORM = 16.0
CONV_WIDTH = 512
CONV_K = 3
D_FF = 2816
EPS = 1e-6

LANES = 128
SUBLANES = 8
BF16_ROWS = 16
MXU_COLS = 256
CHUNK = 128
N_LEVELS = 8
VMEM_CAP = 56 * 1024 * 1024
VMEM_FLOOR = 16 * 1024 * 1024

TN_DIMS = (((0,), (0,)), ((), ()))


def _inv_rms(x):
    return lax.rsqrt(jnp.mean(x * x, axis=-1, keepdims=True) + EPS)


def _rms(x, g):
    return x * _inv_rms(x) * g


def _gain_into_rows(w, g):
    return (g[:, None] * w).astype(BF16)


def _sigmoid(x):
    return 1.0 / (1.0 + jnp.exp(-x))


def _log_sigmoid(x):
    return jnp.minimum(x, 0.0) - jnp.log(1.0 + jnp.exp(-jnp.abs(x)))


def _zero_from(x):
    bits = x.astype(jnp.int32)
    sixteen = jnp.full(x.shape, 16, jnp.int32)
    return lax.shift_right_logical(lax.shift_right_logical(bits, sixteen), sixteen).astype(F32)


def _split2(x):
    hi = x.astype(BF16)
    return hi, (x - hi.astype(F32)).astype(BF16)


def _split3(x):
    hi = x.astype(BF16)
    r1 = x - hi.astype(F32)
    mid = r1.astype(BF16)
    lo = (r1 - mid.astype(F32)).astype(BF16)
    return hi, mid, lo


def _resident(shape):
    nd = len(shape)
    return pl.BlockSpec(shape, lambda *_: (0,) * nd, pipeline_mode=pl.Buffered(1))


def _nbytes(shape, dtype):
    return int(np.prod(shape)) * jnp.dtype(dtype).itemsize


def _vmem_limit(est_bytes):
    return int(min(VMEM_CAP, max(est_bytes * 5 // 4, VMEM_FLOOR)))


class _PrepJob:
    def __init__(self, w, gain=None, pieces=None, transposed=False):
        self.w = w
        self.gain = gain
        self.transposed = transposed
        self.rows, self.cols = (w.shape[1], w.shape[0]) if transposed else w.shape
        self.pieces = pieces or [(0, self.cols, self.cols)]


def _rows_of_transposed(w_t, rb):
    cols = w_t.shape[0]
    whole = cols // LANES * LANES
    parts = [w_t[j:j + LANES, :].T for j in range(0, whole, LANES)]
    if cols > whole:
        parts.append(w_t[cols - LANES:cols, :].T[:, LANES - (cols - whole):])
    t = jnp.concatenate(parts, axis=1)
    which = pl.program_id(0) % (LANES // rb)
    w = t[0:rb, :]
    for j in range(1, LANES // rb):
        w = jnp.where(which == j, t[j * rb:(j + 1) * rb, :], w)
    return w


def _prep_rows(job, g_ref, w_ref, out_refs):
    w = w_ref[...]
    if job.transposed:
        w = _rows_of_transposed(w, job.rb)
    if job.gain is not None:
        w = w * g_ref[:, job.gain:job.gain + 1]
    for (c0, keep, width), out_ref in zip(job.pieces, out_refs):
        piece = w[:, c0:c0 + width]
        if keep < width:
            lane = lax.broadcasted_iota(jnp.int32, piece.shape, 1)
            piece = jnp.where(lane < keep, piece, 0.0)
        out_ref[...] = piece.astype(BF16)


def _ffn_body(post, ple, sizes, jobs, *refs):
    refs = list(refs)
    x_ref, ng_ref, wg_ref, wu_ref, wd_ref = refs[:5]
    del refs[:5]
    if ple:
        p_ref, wpp_ref, wpg_ref = refs[:3]
        del refs[:3]
    if jobs:
        g_ref = refs.pop(0)
        job_in = refs[:len(jobs)]
        del refs[:len(jobs)]
    o_ref = refs.pop(0)
    h_scr = refs.pop()
    for job, w_ref in zip(jobs, job_in if jobs else ()):
        outs = refs[:len(job.pieces)]
        del refs[:len(job.pieces)]
        _prep_rows(job, g_ref, w_ref, outs)

    starts = [sum(sizes[:s]) for s in range(len(sizes))]

    n_chunks = D_FF // MXU_COLS

    def gate_up(s, beside):
        x = x_ref[starts[s]:starts[s] + sizes[s], :]
        xb = x.astype(BF16)
        inv = _inv_rms(x)
        for c in range(n_chunks):
            sl = slice(c * MXU_COLS, (c + 1) * MXU_COLS)
            if c == n_chunks // 2 and beside is not None:
                inv = inv + _zero_from(beside()[0:sizes[s], 0:1])
            g = jnp.dot(xb, wg_ref[:, sl], preferred_element_type=F32) * inv
            u = jnp.dot(xb, wu_ref[:, sl], preferred_element_type=F32) * inv
            h_scr[s % 2, 0:sizes[s], sl] = (g * _sigmoid(g) * u).astype(BF16)

    def down(s):
        return jnp.dot(h_scr[s % 2, 0:sizes[s], :], wd_ref[...], preferred_element_type=F32)

    def epilogue(s, y):
        rows = slice(starts[s], starts[s] + sizes[s])
        h = x_ref[rows, :] + 0.5 * _rms(y, ng_ref[post:post + 1, :])
        if ple:
            gate = _sigmoid(jnp.dot(h.astype(BF16), wpg_ref[...], preferred_element_type=F32) * _inv_rms(h))
            proj = jnp.dot(p_ref[rows, :].astype(BF16), wpp_ref[...], preferred_element_type=F32)
            h = h + _rms(proj * gate, ng_ref[7:8, :])
        o_ref[rows, :] = h
        return h

    pending = None
    for s in range(len(sizes)):
        gate_up(s, pending)
        y = down(s)
        pending = functools.partial(epilogue, s, y)
    pending()


def _ffn(x, ng, wg, wu, wd, post, sizes, ple_args=None, prep=None):
    rows = x.shape[0]
    tm = sum(sizes)
    ts = max(sizes)
    assert rows % tm == 0 and all(sz % BF16_ROWS == 0 for sz in sizes)
    assert all(a >= b for a, b in zip(sizes, sizes[1:]))
    steps = rows // tm
    ple = ple_args is not None
    row_spec = pl.BlockSpec((tm, D_MODEL), lambda i: (i, 0))
    in_specs = [row_spec, _resident(ng.shape), _resident(wg.shape), _resident(wu.shape), _resident(wd.shape)]
    args = [x, ng, wg, wu, wd]
    est = (4 * _nbytes((tm, D_MODEL), F32) + 3 * _nbytes(wg.shape, BF16)
           + 2 * _nbytes((ts, D_FF), BF16) + _nbytes((ts, D_MODEL), BF16)
           + 6 * _nbytes((ts, MXU_COLS), F32) + 6 * _nbytes((ts, D_MODEL), F32))
    if ple:
        p, wpp, wpg = ple_args
        in_specs += [pl.BlockSpec((tm, PLE_DIM), lambda i: (i, 0)), _resident(wpp.shape), _resident(wpg.shape)]
        args += [p, wpp, wpg]
        est += 2 * _nbytes((tm, PLE_DIM), F32) + _nbytes(wpp.shape, BF16) + _nbytes(wpg.shape, BF16)
        est += 3 * _nbytes((ts, D_MODEL), F32)
    jobs = []
    out_specs = [row_spec]
    out_shape = [jax.ShapeDtypeStruct((rows, D_MODEL), F32)]
    if prep is not None:
        gains, jobs = prep
        in_specs.append(pl.BlockSpec((gains.shape[0] // steps, gains.shape[1]), lambda i: (i, 0)))
        args.append(gains)
        for job in jobs:
            r, c = job.rows, job.cols
            rb = job.rb = r // steps
            assert r % steps == 0 and rb % BF16_ROWS == 0
            assert job.gain is None or r == gains.shape[0]
            if job.transposed:
                assert LANES % rb == 0
                in_specs.append(pl.BlockSpec((c, LANES), lambda i, k=LANES // rb: (0, i // k)))
            else:
                in_specs.append(pl.BlockSpec((rb, c), lambda i: (i, 0)))
            args.append(job.w)
            est += 2 * _nbytes((rb, c), F32)
            for _, _, width in job.pieces:
                out_specs.append(pl.BlockSpec((rb, width), lambda i: (i, 0)))
                out_shape.append(jax.ShapeDtypeStruct((r, width), BF16))
                est += 2 * _nbytes((rb, width), BF16)
    res = pl.pallas_call(
        functools.partial(_ffn_body, post, ple, tuple(sizes), jobs),
        grid=(steps,),
        in_specs=in_specs,
        out_specs=out_specs,
        out_shape=out_shape,
        scratch_shapes=[pltpu.VMEM((2, ts, D_FF), BF16)],
        compiler_params=pltpu.CompilerParams(
            dimension_semantics=("arbitrary",), vmem_limit_bytes=_vmem_limit(est)),
        name="ffn_ple" if ple else "ffn",
    )(*args)
    return res if prep is not None else res[0]


def _run_together(*stages):
    results = [None] * len(stages)
    live = list(enumerate(stages))
    while live:
        still = []
        for idx, gen in live:
            try:
                next(gen)
                still.append((idx, gen))
            except StopIteration as stop:
                results[idx] = stop.value
        live = still
    return results


def _finish(stages):
    return _run_together(stages)[0]


def _mixer_inputs(h, ng_ref, wa_ref, wr_ref, wgu_ref, bg_ref, wc_ref):
    hb = h.astype(BF16)
    inv = _inv_rms(h)
    yield
    r = jnp.dot(hb, wr_ref[...], preferred_element_type=F32) * inv
    logit = jnp.dot(r.astype(BF16), wgu_ref[...], preferred_element_type=F32) + bg_ref[...]
    la = _log_sigmoid(logit) * (1.0 / GLA_GATE_NORM)
    yield
    cols = []
    for w_ref in (wa_ref, wc_ref):
        for c in range(0, w_ref.shape[1], MXU_COLS):
            cols.append(jnp.dot(hb, w_ref[:, c:c + MXU_COLS], preferred_element_type=F32) * inv)
            yield
    z = jnp.concatenate(cols, axis=1)
    q = z[:, 0:HK] * (GLA_DK ** -0.5)
    k = z[:, HK:2 * HK]
    v = z[:, 2 * HK:2 * HK + HV]
    g = z[:, 2 * HK + HV:2 * HK + 2 * HV]
    zc = z[:, 2 * HK + 2 * HV:]
    cb = zc[:, 0:CONV_WIDTH]
    cc = zc[:, CONV_WIDTH:2 * CONV_WIDTH]
    ch = zc[:, 2 * CONV_WIDTH:3 * CONV_WIDTH]
    return q, k, v, g, la, cb, cc, ch


def _short_conv(cb, uc, tail, cw_ref):
    row = lax.broadcasted_iota(jnp.int32, uc.shape, 0)
    c1 = tail[SUBLANES - 1:SUBLANES, :]
    c2 = tail[SUBLANES - 2:SUBLANES - 1, :]
    prev1 = jnp.where(row == 0, c1, pltpu.roll(uc, 1, axis=0))
    prev2 = jnp.where(row == 0, c2, jnp.where(row == 1, c1, pltpu.roll(uc, 2, axis=0)))
    return cb * (cw_ref[0:1, :] * prev2 + cw_ref[1:2, :] * prev1 + cw_ref[2:3, :] * uc)


def _mixer_output(h, o, g, oc, ng_ref, gng_ref, wout_ref):
    heads = []
    for hd in range(GLA_HEADS):
        sl = slice(hd * GLA_DV, (hd + 1) * GLA_DV)
        heads.append(_rms(o[:, sl], gng_ref[...]))
    on = jnp.concatenate(heads, axis=1) * (g * _sigmoid(g))
    cat = jnp.concatenate([on.astype(BF16), oc.astype(BF16)], axis=1)
    yield
    cols = []
    for c in range(0, wout_ref.shape[1], MXU_COLS):
        cols.append(jnp.dot(cat, wout_ref[:, c:c + MXU_COLS], preferred_element_type=F32))
        yield
    return h + _rms(jnp.concatenate(cols, axis=1), ng_ref[3:4, :])


def _chunk_constants():
    t = np.arange(CHUNK)[:, None]
    s = np.arange(CHUNK)[None, :]
    blocks = []
    for j in range(1, N_LEVELS - 1):
        half = 1 << j
        mid = (t // (2 * half)) * (2 * half) + half
        blocks.append((s <= t).astype(np.float32) - (s < mid).astype(np.float32))
    blocks.append((s <= t).astype(np.float32))
    m = np.concatenate(blocks, axis=0)
    mst = np.concatenate([m, m], axis=1)
    x = np.bitwise_xor(t, s)
    top = np.floor(np.log2(np.maximum(x, 1))).astype(np.int32) + 1
    lvl = np.where(s > t, -1, np.where(s == t, 0, top)).astype(np.int32)
    return jnp.asarray(mst, BF16), jnp.asarray(np.tile(lvl, (1, 2)), jnp.int32)


def _pair_diag(a, b):
    z = jnp.zeros_like(a)
    return jnp.concatenate([jnp.concatenate([a, z], axis=1), jnp.concatenate([z, b], axis=1)], axis=0)


def _gla_tile(q, k, v, la, st, mst, lvl):
    c = CHUNK
    chunks = [slice(r, r + c) for r in range(0, q.shape[0], c)]
    pairs = [slice(p * 2 * GLA_DK, (p + 1) * 2 * GLA_DK) for p in range(GLA_HEADS // 2)]
    cat = lambda xs: jnp.concatenate(xs, axis=0)

    def key_weights(kt, r, p):
        lo = 2 * p * GLA_DK
        return _pair_diag(kt[lo:lo + GLA_DK, r], kt[lo + GLA_DK:lo + 2 * GLA_DK, r])

    hi, lo = _split2(la)
    e = []
    for n, r in enumerate(chunks):
        e.append(jnp.dot(mst, cat([hi[r, :], lo[r, :]]), preferred_element_type=F32))
    yield
    bc = cat([en[(N_LEVELS - 2) * c:(N_LEVELS - 1) * c, :] for en in e])
    k_t = k.T
    bc_t = bc.T
    b_last = [bc_t[:, r.stop - 1:r.stop] for r in chunks]
    b_last_cols = jnp.concatenate([jnp.broadcast_to(b, (HK, c)) for b in b_last], axis=1)

    qb = q.astype(BF16)
    q1 = (q * jnp.exp(la)).astype(BF16)
    kb = k_t.astype(BF16)
    att = []
    for n, r in enumerate(chunks):
        row = []
        for p, cols in enumerate(pairs):
            x = jnp.dot(cat([qb[r, cols], q1[r, cols]]), key_weights(kb, r, p), preferred_element_type=F32)
            row.append(jnp.where(lvl == 1, x[c:, :], jnp.where(lvl == 0, x[:c, :], 0.0)))
        att.append(row)
    yield
    for lv in range(2, N_LEVELS):
        el = cat([en[(lv - 2) * c:(lv - 1) * c, :] for en in e])
        ql = (q * jnp.exp(jnp.minimum(el, 0.0))).astype(BF16)
        kl = (k_t * jnp.exp(jnp.minimum(-el.T, 0.0))).astype(BF16)
        for n, r in enumerate(chunks):
            for p, cols in enumerate(pairs):
                x = jnp.dot(ql[r, cols], key_weights(kl, r, p), preferred_element_type=F32)
                att[n][p] = jnp.where(lvl == lv, x, att[n][p])
        yield

    vb = v.astype(BF16)
    k_dec = (k_t * jnp.exp(b_last_cols - bc_t)).astype(BF16)
    upd_all = []
    for n, r in enumerate(chunks):
        upd_all.append(jnp.dot(k_dec[:, r], vb[r, :], preferred_element_type=F32))
    yield
    states = []
    for n in range(len(chunks)):
        states.append(st)
        upd = cat([upd_all[n][hd * GLA_DK:(hd + 1) * GLA_DK, hd * GLA_DV:(hd + 1) * GLA_DV]
                   for hd in range(GLA_HEADS)])
        st = st * jnp.exp(b_last[n]) + upd

    q_in = (q * jnp.exp(bc)).astype(BF16)
    o = []
    for n, r in enumerate(chunks):
        sb = states[n].astype(BF16)
        row = []
        for p, cols in enumerate(pairs):
            h0 = 2 * p
            s_w = _pair_diag(sb[h0 * GLA_DK:(h0 + 1) * GLA_DK, :], sb[(h0 + 1) * GLA_DK:(h0 + 2) * GLA_DK, :])
            v_w = _pair_diag(vb[r, h0 * GLA_DV:(h0 + 1) * GLA_DV], vb[r, (h0 + 1) * GLA_DV:(h0 + 2) * GLA_DV])
            row.append(jnp.dot(q_in[r, cols], s_w, preferred_element_type=F32)
                       + jnp.dot(att[n][p].astype(BF16), v_w, preferred_element_type=F32))
        o.append(jnp.concatenate(row, axis=1))
    return cat(o), st


def _mixer_prompt_body(tm, h_ref, ng_ref, wa_ref, wr_ref, wgu_ref, bg_ref, wc_ref, gng_ref, cw_ref,
                       wout_ref, mst_ref, lvl_ref, o_ref, sg_ref, sc_ref, st_scr, carry_scr):
    j = pl.program_id(1)

    @pl.when(j == 0)
    def _():
        st_scr[...] = jnp.zeros_like(st_scr)
        carry_scr[...] = jnp.zeros_like(carry_scr)

    def inputs(h):
        return _mixer_inputs(h, ng_ref, wa_ref, wr_ref, wgu_ref, bg_ref, wc_ref)

    def gla(q, k, v, la, st):
        return _gla_tile(q, k, v, la, st, mst_ref[...], lvl_ref[...])

    def output(h, og, g, cb, uc, tail):
        return _mixer_output(h, og, g, _short_conv(cb, uc, tail, cw_ref), ng_ref, gng_ref, wout_ref)

    h_a = h_ref[0:tm, :]
    h_b = h_ref[tm:2 * tm, :]
    q, k, v, g_a, la, cb_a, cc, ch = _finish(inputs(h_a))
    uc_a = cc * ch
    (q, k, v, g_b, la, cb_b, cc, ch), (og_a, st) = _run_together(inputs(h_b), gla(q, k, v, la, st_scr[...]))
    uc_b = cc * ch
    (og_b, st), out_a = _run_together(gla(q, k, v, la, st), output(h_a, og_a, g_a, cb_a, uc_a, carry_scr[...]))
    o_ref[0:tm, :] = out_a
    o_ref[tm:2 * tm, :] = _finish(output(h_b, og_b, g_b, cb_b, uc_b, uc_a[tm - SUBLANES:tm, :]))
    st_scr[...] = st
    carry_scr[...] = uc_b[tm - SUBLANES:tm, :]

    @pl.when(j == pl.num_programs(1) - 1)
    def _():
        sg_ref[0] = st
        sc_ref[0] = carry_scr[SUBLANES - (CONV_K - 1):SUBLANES, :]


def _mixer_prompt(h, batch, seq, tm, ng, wa, wr, wgu, bg, wc, gng, cw, wout):
    assert seq % (2 * tm) == 0 and tm % (2 * CHUNK) == 0
    nt = seq // (2 * tm)
    mst, lvl = _chunk_constants()
    consts = [ng, wa, wr, wgu, bg, wc, gng, cw, wout, mst, lvl]
    row_spec = pl.BlockSpec((2 * tm, D_MODEL), lambda b, j: (b * nt + j, 0))
    est = (8 * _nbytes((tm, D_MODEL), F32) + sum(_nbytes(c.shape, c.dtype) for c in consts)
           + 4 * _nbytes((tm, 3 * HV), F32) + 2 * _nbytes((tm, 3 * HK + 2 * HV), F32)
           + 8 * _nbytes((tm, D_MODEL), F32))
    return pl.pallas_call(
        functools.partial(_mixer_prompt_body, tm),
        grid=(batch, nt),
        in_specs=[row_spec] + [_resident(c.shape) for c in consts],
        out_specs=[row_spec,
                   pl.BlockSpec((1, HK, GLA_DV), lambda b, j: (b, 0, 0)),
                   pl.BlockSpec((1, CONV_K - 1, CONV_WIDTH), lambda b, j: (b, 0, 0))],
        out_shape=[jax.ShapeDtypeStruct((batch * seq, D_MODEL), F32),
                   jax.ShapeDtypeStruct((batch, HK, GLA_DV), F32),
                   jax.ShapeDtypeStruct((batch, CONV_K - 1, CONV_WIDTH), F32)],
        scratch_shapes=[pltpu.VMEM((HK, GLA_DV), F32), pltpu.VMEM((SUBLANES, CONV_WIDTH), F32)],
        compiler_params=pltpu.CompilerParams(
            dimension_semantics=("arbitrary", "arbitrary"), vmem_limit_bytes=_vmem_limit(est)),
        name="mixer_prompt",
    )(h, *consts)


def _mixer_sample_body(nb, bs, h_ref, s_ref, c0_ref, ng_ref, wa_ref, wr_ref, wgu_ref, bg_ref, wc_ref,
                       gng_ref, cw_ref, wout_ref, o_ref, so_ref, co_ref,
                       q_scr, k_scr, la3_scr, v_scr, g_scr, oc_scr, og_scr):
    i = pl.program_id(0)

    @pl.when(i == 0)
    def _():
        q, k, v, g, la, cb, cc, ch = _finish(
            _mixer_inputs(h_ref[...], ng_ref, wa_ref, wr_ref, wgu_ref, bg_ref, wc_ref))
        q_scr[...] = q.astype(BF16)
        k_scr[...] = k.astype(BF16)
        la3_scr[...] = jnp.concatenate(_split3(la), axis=0)
        v_scr[...] = v
        g_scr[...] = g
        uc = cc * ch
        c0 = c0_ref[...]
        old2 = c0[:, 0:CONV_WIDTH]
        old1 = c0[:, CONV_WIDTH:2 * CONV_WIDTH]
        y = cw_ref[0:1, :] * old2 + cw_ref[1:2, :] * old1 + cw_ref[2:3, :] * uc
        oc_scr[...] = cb * y
        co_ref[...] = jnp.concatenate([old1, uc], axis=1)

    first = pl.multiple_of(i * bs, bs)
    seq_row = lax.broadcasted_iota(jnp.int32, (nb, bs * LANES), 0)
    seq_col = lax.broadcasted_iota(jnp.int32, (nb, bs * LANES), 1) // LANES + first
    pick = jnp.where(seq_row == seq_col, 1.0, 0.0).astype(BF16)
    pick3 = jnp.concatenate([pick, pick, pick], axis=0)
    decay_t = jnp.exp(lax.dot_general(la3_scr[...], pick3, TN_DIMS, preferred_element_type=F32))
    k_t = lax.dot_general(k_scr[...], pick, TN_DIMS, preferred_element_type=F32)
    q_t = lax.dot_general(q_scr[...], pick, TN_DIMS, preferred_element_type=F32)
    v_blk = v_scr[pl.ds(first, bs), :]
    o_rows = []
    for t in range(bs):
        cols = slice(t * LANES, (t + 1) * LANES)
        v_b = jnp.concatenate(
            [jnp.broadcast_to(v_blk[t:t + 1, hd * GLA_DV:(hd + 1) * GLA_DV], (GLA_DK, GLA_DV))
             for hd in range(GLA_HEADS)], axis=0)
        s_new = decay_t[:, cols] * s_ref[t] + k_t[:, cols] * v_b
        so_ref[t] = s_new
        qs = q_t[:, cols] * s_new
        o_rows.append(jnp.concatenate(
            [jnp.sum(qs[hd * GLA_DK:(hd + 1) * GLA_DK, :], axis=0, keepdims=True)
             for hd in range(GLA_HEADS)], axis=1))
    og_scr[pl.ds(first, bs), :] = jnp.concatenate(o_rows, axis=0)

    @pl.when(i == pl.num_programs(0) - 1)
    def _():
        o_ref[...] = _finish(
            _mixer_output(h_ref[...], og_scr[...], g_scr[...], oc_scr[...], ng_ref, gng_ref, wout_ref))


def _mixer_sample(h, s0, c0, bs, ng, wa, wr, wgu, bg, wc, gng, cw, wout):
    nb = h.shape[0]
    assert nb == LANES and nb % bs == 0
    consts = [ng, wa, wr, wgu, bg, wc, gng, cw, wout]
    state_spec = pl.BlockSpec((bs, HK, GLA_DV), lambda i: (i, 0, 0))
    est = (sum(_nbytes(c.shape, c.dtype) for c in consts) + 4 * _nbytes((bs, HK, GLA_DV), F32)
           + 12 * _nbytes((nb, D_MODEL), F32) + 2 * _nbytes((nb, 3 * HV), F32))
    return pl.pallas_call(
        functools.partial(_mixer_sample_body, nb, bs),
        grid=(nb // bs,),
        in_specs=[_resident(h.shape), state_spec, _resident(c0.shape)] + [_resident(c.shape) for c in consts],
        out_specs=[pl.BlockSpec((nb, D_MODEL), lambda i: (0, 0)), state_spec,
                   pl.BlockSpec(c0.shape, lambda i: (0, 0))],
        out_shape=[jax.ShapeDtypeStruct((nb, D_MODEL), F32),
                   jax.ShapeDtypeStruct(s0.shape, F32),
                   jax.ShapeDtypeStruct(c0.shape, F32)],
        scratch_shapes=[pltpu.VMEM((nb, HK), BF16), pltpu.VMEM((nb, HK), BF16), pltpu.VMEM((3 * nb, HK), BF16),
                        pltpu.VMEM((nb, HV), F32), pltpu.VMEM((nb, HV), F32), pltpu.VMEM((nb, CONV_WIDTH), F32),
                        pltpu.VMEM((nb, HV), F32)],
        compiler_params=pltpu.CompilerParams(
            dimension_semantics=("arbitrary",), vmem_limit_bytes=_vmem_limit(est)),
        name="mixer_sample",
    )(h, s0, c0, *consts)


TM_PROMPT = 512
FFN_SIZES = (768, 256)
FFN_PLE_SIZES = (256, 256, 256, 256)
BS_SAMPLE = 32


def kernel(x_prompt, x_sample, state_gla, state_conv, p_prompt, p_sample, norm_g, w_in, w_gate_up, b_gate,
           gla_norm_g, conv_w, w_out, ffn1_gate, ffn1_up, ffn1_down, ffn2_gate, ffn2_up, ffn2_down,
           w_ple_proj, w_ple_gate):
    batch, seq, _ = x_prompt.shape
    nb = x_sample.shape[0]
    assert norm_g.shape[0] == 1 and x_sample.shape[1] == 1

    ng = norm_g[0]
    f1 = (_gain_into_rows(ffn1_gate[0], ng[0]), _gain_into_rows(ffn1_up[0], ng[0]), ffn1_down[0].astype(BF16))
    wgu = jnp.pad(w_gate_up[0], ((0, LANES - GLA_RANK), (0, 0))).astype(BF16)
    bg = b_gate[0].reshape(1, HK)
    gng = gla_norm_g[0].reshape(1, GLA_DV)
    cw = conv_w[0]
    r0 = 2 * HK + 2 * HV
    r1 = r0 + GLA_RANK
    jobs = [
        _PrepJob(ffn2_gate[0], gain=4), _PrepJob(ffn2_up[0], gain=4), _PrepJob(ffn2_down[0]),
        _PrepJob(jnp.transpose(w_in[0]), gain=2, transposed=True,
                 pieces=[(0, r0, r0), (r0, GLA_RANK, LANES), (r1, w_in.shape[2] - r1, w_in.shape[2] - r1)]),
        _PrepJob(w_out[0]), _PrepJob(w_ple_gate[0], gain=6), _PrepJob(w_ple_proj[0]),
    ]

    xp = x_prompt.reshape(batch * seq, D_MODEL)
    pp = p_prompt[0].reshape(batch * seq, PLE_DIM)
    hp, f2g, f2u, f2d, wa, wr, wc, wout, wpg, wpp = _ffn(xp, ng, *f1, 1, FFN_SIZES, prep=(ng.T, jobs))
    f2 = (f2g, f2u, f2d)
    mix_w = (ng, wa, wr, wgu, bg, wc, gng, cw, wout)
    hp, sg_p, sc_p = _mixer_prompt(hp, batch, seq, TM_PROMPT, *mix_w)
    hp = _ffn(hp, ng, *f2, 5, FFN_PLE_SIZES, ple_args=(pp, wpp, wpg))

    xs = x_sample.reshape(nb, D_MODEL)
    ps = p_sample[0].reshape(nb, PLE_DIM)
    s0 = state_gla[0].reshape(nb, HK, GLA_DV)
    c0 = state_conv[0].reshape(nb, (CONV_K - 1) * CONV_WIDTH)
    hs = _ffn(xs, ng, *f1, 1, (nb,))
    hs, sg_s, sc_s = _mixer_sample(hs, s0, c0, BS_SAMPLE, *mix_w)
    hs = _ffn(hs, ng, *f2, 5, (nb,), ple_args=(ps, wpp, wpg))

    return (hp.reshape(batch, seq, D_MODEL),
            hs.reshape(nb, 1, D_MODEL),
            sg_p.reshape(1, batch, GLA_HEADS, GLA_DK, GLA_DV),
            sc_p.reshape(1, batch, CONV_K - 1, CONV_WIDTH),
            sg_s.reshape(1, nb, GLA_HEADS, GLA_DK, GLA_DV),
            sc_s.reshape(1, nb, CONV_K - 1, CONV_WIDTH))
```

```python
import functools

import numpy as np
import jax
import jax.numpy as jnp
from jax import lax
from jax.experimental import pallas as pl
from jax.experimental.pallas import tpu as pltpu

F32 = jnp.float32
BF16 = jnp.bfloat16

D_MODEL = 1024
PLE_DIM = 256
GLA_HEADS = 4
GLA_DK = 64
GLA_DV = 128
HK = GLA_HEADS * GLA_DK
HV = GLA_HEADS * GLA_DV
GLA_RANK = 16
GLA_GATE_NORM = 16.0
CONV_WIDTH = 512
CONV_K = 3
D_FF = 2816
EPS = 1e-6

LANES = 128
SUBLANES = 8
BF16_ROWS = 16
MXU_COLS = 256
CHUNK = 128
N_LEVELS = 8
VMEM_CAP = 56 * 1024 * 1024
VMEM_FLOOR = 16 * 1024 * 1024

TN_DIMS = (((0,), (0,)), ((), ()))


def _inv_rms(x):
    return lax.rsqrt(jnp.mean(x * x, axis=-1, keepdims=True) + EPS)


def _rms(x, g):
    return x * _inv_rms(x) * g


def _gain_into_rows(w, g):
    return (g[:, None] * w).astype(BF16)


def _sigmoid(x):
    return 1.0 / (1.0 + jnp.exp(-x))


def _log_sigmoid(x):
    return jnp.minimum(x, 0.0) - jnp.log(1.0 + jnp.exp(-jnp.abs(x)))


def _zero_from(x):
    bits = x.astype(jnp.int32)
    sixteen = jnp.full(x.shape, 16, jnp.int32)
    return lax.shift_right_logical(lax.shift_right_logical(bits, sixteen), sixteen).astype(F32)


def _split2(x):
    hi = x.astype(BF16)
    return hi, (x - hi.astype(F32)).astype(BF16)


def _split3(x):
    hi = x.astype(BF16)
    r1 = x - hi.astype(F32)
    mid = r1.astype(BF16)
    lo = (r1 - mid.astype(F32)).astype(BF16)
    return hi, mid, lo


def _resident(shape):
    nd = len(shape)
    return pl.BlockSpec(shape, lambda *_: (0,) * nd, pipeline_mode=pl.Buffered(1))


def _nbytes(shape, dtype):
    return int(np.prod(shape)) * jnp.dtype(dtype).itemsize


def _vmem_limit(est_bytes):
    return int(min(VMEM_CAP, max(est_bytes * 5 // 4, VMEM_FLOOR)))


class _PrepJob:
    def __init__(self, w, gain=None, pieces=None, transposed=False):
        self.w = w
        self.gain = gain
        self.transposed = transposed
        self.rows, self.cols = (w.shape[1], w.shape[0]) if transposed else w.shape
        self.pieces = pieces or [(0, self.cols, self.cols)]


def _rows_of_transposed(w_t, rb):
    cols = w_t.shape[0]
    whole = cols // LANES * LANES
    parts = [w_t[j:j + LANES, :].T for j in range(0, whole, LANES)]
    if cols > whole:
        parts.append(w_t[cols - LANES:cols, :].T[:, LANES - (cols - whole):])
    t = jnp.concatenate(parts, axis=1)
    which = pl.program_id(0) % (LANES // rb)
    w = t[0:rb, :]
    for j in range(1, LANES // rb):
        w = jnp.where(which == j, t[j * rb:(j + 1) * rb, :], w)
    return w


def _prep_rows(job, g_ref, w_ref, out_refs):
    w = w_ref[...]
    if job.transposed:
        w = _rows_of_transposed(w, job.rb)
    if job.gain is not None:
        w = w * g_ref[:, job.gain:job.gain + 1]
    for (c0, keep, width), out_ref in zip(job.pieces, out_refs):
        piece = w[:, c0:c0 + width]
        if keep < width:
            lane = lax.broadcasted_iota(jnp.int32, piece.shape, 1)
            piece = jnp.where(lane < keep, piece, 0.0)
        out_ref[...] = piece.astype(BF16)


def _ffn_body(post, ple, sizes, jobs, *refs):
    refs = list(refs)
    x_ref, ng_ref, wg_ref, wu_ref, wd_ref = refs[:5]
    del refs[:5]
    if ple:
        p_ref, wpp_ref, wpg_ref = refs[:3]
        del refs[:3]
    if jobs:
        g_ref = refs.pop(0)
        job_in = refs[:len(jobs)]
        del refs[:len(jobs)]
    o_ref = refs.pop(0)
    h_scr = refs.pop()
    for job, w_ref in zip(jobs, job_in if jobs else ()):
        outs = refs[:len(job.pieces)]
        del refs[:len(job.pieces)]
        _prep_rows(job, g_ref, w_ref, outs)

    starts = [sum(sizes[:s]) for s in range(len(sizes))]

    n_chunks = D_FF // MXU_COLS

    def gate_up(s, beside):
        x = x_ref[starts[s]:starts[s] + sizes[s], :]
        xb = x.astype(BF16)
        inv = _inv_rms(x)
        for c in range(n_chunks):
            sl = slice(c * MXU_COLS, (c + 1) * MXU_COLS)
            if c == n_chunks // 2 and beside is not None:
                inv = inv + _zero_from(beside()[0:sizes[s], 0:1])
            g = jnp.dot(xb, wg_ref[:, sl], preferred_element_type=F32) * inv
            u = jnp.dot(xb, wu_ref[:, sl], preferred_element_type=F32) * inv
            h_scr[s % 2, 0:sizes[s], sl] = (g * _sigmoid(g) * u).astype(BF16)

    def down(s):
        return jnp.dot(h_scr[s % 2, 0:sizes[s], :], wd_ref[...], preferred_element_type=F32)

    def epilogue(s, y):
        rows = slice(starts[s], starts[s] + sizes[s])
        h = x_ref[rows, :] + 0.5 * _rms(y, ng_ref[post:post + 1, :])
        if ple:
            gate = _sigmoid(jnp.dot(h.astype(BF16), wpg_ref[...], preferred_element_type=F32) * _inv_rms(h))
            proj = jnp.dot(p_ref[rows, :].astype(BF16), wpp_ref[...], preferred_element_type=F32)
            h = h + _rms(proj * gate, ng_ref[7:8, :])
        o_ref[rows, :] = h
        return h

    pending = None
    for s in range(len(sizes)):
        gate_up(s, pending)
        y = down(s)
        pending = functools.partial(epilogue, s, y)
    pending()


def _ffn(x, ng, wg, wu, wd, post, sizes, ple_args=None, prep=None):
    rows = x.shape[0]
    tm = sum(sizes)
    ts = max(sizes)
    assert rows % tm == 0 and all(sz % BF16_ROWS == 0 for sz in sizes)
    assert all(a >= b for a, b in zip(sizes, sizes[1:]))
    steps = rows // tm
    ple = ple_args is not None
    row_spec = pl.BlockSpec((tm, D_MODEL), lambda i: (i, 0))
    in_specs = [row_spec, _resident(ng.shape), _resident(wg.shape), _resident(wu.shape), _resident(wd.shape)]
    args = [x, ng, wg, wu, wd]
    est = (4 * _nbytes((tm, D_MODEL), F32) + 3 * _nbytes(wg.shape, BF16)
           + 2 * _nbytes((ts, D_FF), BF16) + _nbytes((ts, D_MODEL), BF16)
           + 6 * _nbytes((ts, MXU_COLS), F32) + 6 * _nbytes((ts, D_MODEL), F32))
    if ple:
        p, wpp, wpg = ple_args
        in_specs += [pl.BlockSpec((tm, PLE_DIM), lambda i: (i, 0)), _resident(wpp.shape), _resident(wpg.shape)]
        args += [p, wpp, wpg]
        est += 2 * _nbytes((tm, PLE_DIM), F32) + _nbytes(wpp.shape, BF16) + _nbytes(wpg.shape, BF16)
        est += 3 * _nbytes((ts, D_MODEL), F32)
    jobs = []
    out_specs = [row_spec]
    out_shape = [jax.ShapeDtypeStruct((rows, D_MODEL), F32)]
    if prep is not None:
        gains, jobs = prep
        in_specs.append(pl.BlockSpec((gains.shape[0] // steps, gains.shape[1]), lambda i: (i, 0)))
        args.append(gains)
        for job in jobs:
            r, c = job.rows, job.cols
            rb = job.rb = r // steps
            assert r % steps == 0 and rb % BF16_ROWS == 0
            assert job.gain is None or r == gains.shape[0]
            if job.transposed:
                assert LANES % rb == 0
                in_specs.append(pl.BlockSpec((c, LANES), lambda i, k=LANES // rb: (0, i // k)))
            else:
                in_specs.append(pl.BlockSpec((rb, c), lambda i: (i, 0)))
            args.append(job.w)
            est += 2 * _nbytes((rb, c), F32)
            for _, _, width in job.pieces:
                out_specs.append(pl.BlockSpec((rb, width), lambda i: (i, 0)))
                out_shape.append(jax.ShapeDtypeStruct((r, width), BF16))
                est += 2 * _nbytes((rb, width), BF16)
    res = pl.pallas_call(
        functools.partial(_ffn_body, post, ple, tuple(sizes), jobs),
        grid=(steps,),
        in_specs=in_specs,
        out_specs=out_specs,
        out_shape=out_shape,
        scratch_shapes=[pltpu.VMEM((2, ts, D_FF), BF16)],
        compiler_params=pltpu.CompilerParams(
            dimension_semantics=("parallel",), vmem_limit_bytes=_vmem_limit(est)),
        name="ffn_ple" if ple else "ffn",
    )(*args)
    return res if prep is not None else res[0]


def _run_together(*stages):
    results = [None] * len(stages)
    live = list(enumerate(stages))
    while live:
        still = []
        for idx, gen in live:
            try:
                next(gen)
                still.append((idx, gen))
            except StopIteration as stop:
                results[idx] = stop.value
        live = still
    return results


def _finish(stages):
    return _run_together(stages)[0]


def _mixer_inputs(h, ng_ref, wa_ref, wr_ref, wgu_ref, bg_ref, wc_ref):
    hb = h.astype(BF16)
    inv = _inv_rms(h)
    yield
    r = jnp.dot(hb, wr_ref[...], preferred_element_type=F32) * inv
    logit = jnp.dot(r.astype(BF16), wgu_ref[...], preferred_element_type=F32) + bg_ref[...]
    la = _log_sigmoid(logit) * (1.0 / GLA_GATE_NORM)
    yield
    cols = []
    for w_ref in (wa_ref, wc_ref):
        for c in range(0, w_ref.shape[1], MXU_COLS):
            cols.append(jnp.dot(hb, w_ref[:, c:c + MXU_COLS], preferred_element_type=F32) * inv)
            yield
    z = jnp.concatenate(cols, axis=1)
    q = z[:, 0:HK] * (GLA_DK ** -0.5)
    k = z[:, HK:2 * HK]
    v = z[:, 2 * HK:2 * HK + HV]
    g = z[:, 2 * HK + HV:2 * HK + 2 * HV]
    zc = z[:, 2 * HK + 2 * HV:]
    cb = zc[:, 0:CONV_WIDTH]
    cc = zc[:, CONV_WIDTH:2 * CONV_WIDTH]
    ch = zc[:, 2 * CONV_WIDTH:3 * CONV_WIDTH]
    return q, k, v, g, la, cb, cc, ch


def _short_conv(cb, uc, tail, cw_ref):
    row = lax.broadcasted_iota(jnp.int32, uc.shape, 0)
    c1 = tail[SUBLANES - 1:SUBLANES, :]
    c2 = tail[SUBLANES - 2:SUBLANES - 1, :]
    prev1 = jnp.where(row == 0, c1, pltpu.roll(uc, 1, axis=0))
    prev2 = jnp.where(row == 0, c2, jnp.where(row == 1, c1, pltpu.roll(uc, 2, axis=0)))
    return cb * (cw_ref[0:1, :] * prev2 + cw_ref[1:2, :] * prev1 + cw_ref[2:3, :] * uc)


def _mixer_output(h, o, g, oc, ng_ref, gng_ref, wout_ref):
    heads = []
    for hd in range(GLA_HEADS):
        sl = slice(hd * GLA_DV, (hd + 1) * GLA_DV)
        heads.append(_rms(o[:, sl], gng_ref[...]))
    on = jnp.concatenate(heads, axis=1) * (g * _sigmoid(g))
    cat = jnp.concatenate([on.astype(BF16), oc.astype(BF16)], axis=1)
    yield
    cols = []
    for c in range(0, wout_ref.shape[1], MXU_COLS):
        cols.append(jnp.dot(cat, wout_ref[:, c:c + MXU_COLS], preferred_element_type=F32))
        yield
    return h + _rms(jnp.concatenate(cols, axis=1), ng_ref[3:4, :])


def _chunk_constants():
    t = np.arange(CHUNK)[:, None]
    s = np.arange(CHUNK)[None, :]
    blocks = []
    for j in range(1, N_LEVELS - 1):
        half = 1 << j
        mid = (t // (2 * half)) * (2 * half) + half
        blocks.append((s <= t).astype(np.float32) - (s < mid).astype(np.float32))
    blocks.append((s <= t).astype(np.float32))
    m = np.concatenate(blocks, axis=0)
    mst = np.concatenate([m, m], axis=1)
    x = np.bitwise_xor(t, s)
    top = np.floor(np.log2(np.maximum(x, 1))).astype(np.int32) + 1
    lvl = np.where(s > t, -1, np.where(s == t, 0, top)).astype(np.int32)
    return jnp.asarray(mst, BF16), jnp.asarray(np.tile(lvl, (1, 2)), jnp.int32)


def _pair_diag(a, b):
    z = jnp.zeros_like(a)
    return jnp.concatenate([jnp.concatenate([a, z], axis=1), jnp.concatenate([z, b], axis=1)], axis=0)


def _gla_tile(q, k, v, la, st, mst, lvl):
    c = CHUNK
    chunks = [slice(r, r + c) for r in range(0, q.shape[0], c)]
    pairs = [slice(p * 2 * GLA_DK, (p + 1) * 2 * GLA_DK) for p in range(GLA_HEADS // 2)]
    cat = lambda xs: jnp.concatenate(xs, axis=0)

    def key_weights(kt, r, p):
        lo = 2 * p * GLA_DK
        return _pair_diag(kt[lo:lo + GLA_DK, r], kt[lo + GLA_DK:lo + 2 * GLA_DK, r])

    hi, lo = _split2(la)
    e = []
    for n, r in enumerate(chunks):
        e.append(jnp.dot(mst, cat([hi[r, :], lo[r, :]]), preferred_element_type=F32))
    yield
    bc = cat([en[(N_LEVELS - 2) * c:(N_LEVELS - 1) * c, :] for en in e])
    k_t = k.T
    bc_t = bc.T
    b_last = [bc_t[:, r.stop - 1:r.stop] for r in chunks]
    b_last_cols = jnp.concatenate([jnp.broadcast_to(b, (HK, c)) for b in b_last], axis=1)

    qb = q.astype(BF16)
    q1 = (q * jnp.exp(la)).astype(BF16)
    kb = k_t.astype(BF16)
    att = []
    for n, r in enumerate(chunks):
        row = []
        for p, cols in enumerate(pairs):
            x = jnp.dot(cat([qb[r, cols], q1[r, cols]]), key_weights(kb, r, p), preferred_element_type=F32)
            row.append(jnp.where(lvl == 1, x[c:, :], jnp.where(lvl == 0, x[:c, :], 0.0)))
        att.append(row)
    yield
    for lv in range(2, N_LEVELS):
        el = cat([en[(lv - 2) * c:(lv - 1) * c, :] for en in e])
        ql = (q * jnp.exp(jnp.minimum(el, 0.0))).astype(BF16)
        kl = (k_t * jnp.exp(jnp.minimum(-el.T, 0.0))).astype(BF16)
        for n, r in enumerate(chunks):
            for p, cols in enumerate(pairs):
                x = jnp.dot(ql[r, cols], key_weights(kl, r, p), preferred_element_type=F32)
                att[n][p] = jnp.where(lvl == lv, x, att[n][p])
        yield

    vb = v.astype(BF16)
    k_dec = (k_t * jnp.exp(b_last_cols - bc_t)).astype(BF16)
    upd_all = []
    for n, r in enumerate(chunks):
        upd_all.append(jnp.dot(k_dec[:, r], vb[r, :], preferred_element_type=F32))
    yield
    states = []
    for n in range(len(chunks)):
        states.append(st)
        upd = cat([upd_all[n][hd * GLA_DK:(hd + 1) * GLA_DK, hd * GLA_DV:(hd + 1) * GLA_DV]
                   for hd in range(GLA_HEADS)])
        st = st * jnp.exp(b_last[n]) + upd

    q_in = (q * jnp.exp(bc)).astype(BF16)
    o = []
    for n, r in enumerate(chunks):
        sb = states[n].astype(BF16)
        row = []
        for p, cols in enumerate(pairs):
            h0 = 2 * p
            s_w = _pair_diag(sb[h0 * GLA_DK:(h0 + 1) * GLA_DK, :], sb[(h0 + 1) * GLA_DK:(h0 + 2) * GLA_DK, :])
            v_w = _pair_diag(vb[r, h0 * GLA_DV:(h0 + 1) * GLA_DV], vb[r, (h0 + 1) * GLA_DV:(h0 + 2) * GLA_DV])
            row.append(jnp.dot(q_in[r, cols], s_w, preferred_element_type=F32)
                       + jnp.dot(att[n][p].astype(BF16), v_w, preferred_element_type=F32))
        o.append(jnp.concatenate(row, axis=1))
    return cat(o), st


def _mixer_prompt_body(tm, h_ref, ng_ref, wa_ref, wr_ref, wgu_ref, bg_ref, wc_ref, gng_ref, cw_ref,
                       wout_ref, mst_ref, lvl_ref, o_ref, sg_ref, sc_ref, st_scr, carry_scr):
    j = pl.program_id(1)

    @pl.when(j == 0)
    def _():
        st_scr[...] = jnp.zeros_like(st_scr)
        carry_scr[...] = jnp.zeros_like(carry_scr)

    def inputs(h):
        return _mixer_inputs(h, ng_ref, wa_ref, wr_ref, wgu_ref, bg_ref, wc_ref)

    def gla(q, k, v, la, st):
        return _gla_tile(q, k, v, la, st, mst_ref[...], lvl_ref[...])

    def output(h, og, g, cb, uc, tail):
        return _mixer_output(h, og, g, _short_conv(cb, uc, tail, cw_ref), ng_ref, gng_ref, wout_ref)

    h_a = h_ref[0:tm, :]
    h_b = h_ref[tm:2 * tm, :]
    q, k, v, g_a, la, cb_a, cc, ch = _finish(inputs(h_a))
    uc_a = cc * ch
    (q, k, v, g_b, la, cb_b, cc, ch), (og_a, st) = _run_together(inputs(h_b), gla(q, k, v, la, st_scr[...]))
    uc_b = cc * ch
    (og_b, st), out_a = _run_together(gla(q, k, v, la, st), output(h_a, og_a, g_a, cb_a, uc_a, carry_scr[...]))
    o_ref[0:tm, :] = out_a
    o_ref[tm:2 * tm, :] = _finish(output(h_b, og_b, g_b, cb_b, uc_b, uc_a[tm - SUBLANES:tm, :]))
    st_scr[...] = st
    carry_scr[...] = uc_b[tm - SUBLANES:tm, :]

    @pl.when(j == pl.num_programs(1) - 1)
    def _():
        sg_ref[0] = st
        sc_ref[0] = carry_scr[SUBLANES - (CONV_K - 1):SUBLANES, :]


def _mixer_prompt(h, batch, seq, tm, ng, wa, wr, wgu, bg, wc, gng, cw, wout):
    assert seq % (2 * tm) == 0 and tm % (2 * CHUNK) == 0
    nt = seq // (2 * tm)
    mst, lvl = _chunk_constants()
    consts = [ng, wa, wr, wgu, bg, wc, gng, cw, wout, mst, lvl]
    row_spec = pl.BlockSpec((2 * tm, D_MODEL), lambda b, j: (b * nt + j, 0))
    est = (8 * _nbytes((tm, D_MODEL), F32) + sum(_nbytes(c.shape, c.dtype) for c in consts)
           + 4 * _nbytes((tm, 3 * HV), F32) + 2 * _nbytes((tm, 3 * HK + 2 * HV), F32)
           + 8 * _nbytes((tm, D_MODEL), F32))
    return pl.pallas_call(
        functools.partial(_mixer_prompt_body, tm),
        grid=(batch, nt),
        in_specs=[row_spec] + [_resident(c.shape) for c in consts],
        out_specs=[row_spec,
                   pl.BlockSpec((1, HK, GLA_DV), lambda b, j: (b, 0, 0)),
                   pl.BlockSpec((1, CONV_K - 1, CONV_WIDTH), lambda b, j: (b, 0, 0))],
        out_shape=[jax.ShapeDtypeStruct((batch * seq, D_MODEL), F32),
                   jax.ShapeDtypeStruct((batch, HK, GLA_DV), F32),
                   jax.ShapeDtypeStruct((batch, CONV_K - 1, CONV_WIDTH), F32)],
        scratch_shapes=[pltpu.VMEM((HK, GLA_DV), F32), pltpu.VMEM((SUBLANES, CONV_WIDTH), F32)],
        compiler_params=pltpu.CompilerParams(
            dimension_semantics=("parallel", "arbitrary"), vmem_limit_bytes=_vmem_limit(est)),
        name="mixer_prompt",
    )(h, *consts)


def _mixer_sample_body(nb, bs, h_ref, s_ref, c0_ref, ng_ref, wa_ref, wr_ref, wgu_ref, bg_ref, wc_ref,
                       gng_ref, cw_ref, wout_ref, o_ref, so_ref, co_ref,
                       q_scr, k_scr, la3_scr, v_scr, g_scr, oc_scr, og_scr):
    i = pl.program_id(0)

    @pl.when(i == 0)
    def _():
        q, k, v, g, la, cb, cc, ch = _finish(
            _mixer_inputs(h_ref[...], ng_ref, wa_ref, wr_ref, wgu_ref, bg_ref, wc_ref))
        q_scr[...] = q.astype(BF16)
        k_scr[...] = k.astype(BF16)
        la3_scr[...] = jnp.concatenate(_split3(la), axis=0)
        v_scr[...] = v
        g_scr[...] = g
        uc = cc * ch
        c0 = c0_ref[...]
        old2 = c0[:, 0:CONV_WIDTH]
        old1 = c0[:, CONV_WIDTH:2 * CONV_WIDTH]
        y = cw_ref[0:1, :] * old2 + cw_ref[1:2, :] * old1 + cw_ref[2:3, :] * uc
        oc_scr[...] = cb * y
        co_ref[...] = jnp.concatenate([old1, uc], axis=1)

    first = pl.multiple_of(i * bs, bs)
    seq_row = lax.broadcasted_iota(jnp.int32, (nb, bs * LANES), 0)
    seq_col = lax.broadcasted_iota(jnp.int32, (nb, bs * LANES), 1) // LANES + first
    pick = jnp.where(seq_row == seq_col, 1.0, 0.0).astype(BF16)
    pick3 = jnp.concatenate([pick, pick, pick], axis=0)
    decay_t = jnp.exp(lax.dot_general(la3_scr[...], pick3, TN_DIMS, preferred_element_type=F32))
    k_t = lax.dot_general(k_scr[...], pick, TN_DIMS, preferred_element_type=F32)
    q_t = lax.dot_general(q_scr[...], pick, TN_DIMS, preferred_element_type=F32)
    v_blk = v_scr[pl.ds(first, bs), :]
    o_rows = []
    for t in range(bs):
        cols = slice(t * LANES, (t + 1) * LANES)
        v_b = jnp.concatenate(
            [jnp.broadcast_to(v_blk[t:t + 1, hd * GLA_DV:(hd + 1) * GLA_DV], (GLA_DK, GLA_DV))
             for hd in range(GLA_HEADS)], axis=0)
        s_new = decay_t[:, cols] * s_ref[t] + k_t[:, cols] * v_b
        so_ref[t] = s_new
        qs = q_t[:, cols] * s_new
        o_rows.append(jnp.concatenate(
            [jnp.sum(qs[hd * GLA_DK:(hd + 1) * GLA_DK, :], axis=0, keepdims=True)
             for hd in range(GLA_HEADS)], axis=1))
    og_scr[pl.ds(first, bs), :] = jnp.concatenate(o_rows, axis=0)

    @pl.when(i == pl.num_programs(0) - 1)
    def _():
        o_ref[...] = _finish(
            _mixer_output(h_ref[...], og_scr[...], g_scr[...], oc_scr[...], ng_ref, gng_ref, wout_ref))


def _mixer_sample(h, s0, c0, bs, ng, wa, wr, wgu, bg, wc, gng, cw, wout):
    nb = h.shape[0]
    assert nb == LANES and nb % bs == 0
    consts = [ng, wa, wr, wgu, bg, wc, gng, cw, wout]
    state_spec = pl.BlockSpec((bs, HK, GLA_DV), lambda i: (i, 0, 0))
    est = (sum(_nbytes(c.shape, c.dtype) for c in consts) + 4 * _nbytes((bs, HK, GLA_DV), F32)
           + 12 * _nbytes((nb, D_MODEL), F32) + 2 * _nbytes((nb, 3 * HV), F32))
    return pl.pallas_call(
        functools.partial(_mixer_sample_body, nb, bs),
        grid=(nb // bs,),
        in_specs=[_resident(h.shape), state_spec, _resident(c0.shape)] + [_resident(c.shape) for c in consts],
        out_specs=[pl.BlockSpec((nb, D_MODEL), lambda i: (0, 0)), state_spec,
                   pl.BlockSpec(c0.shape, lambda i: (0, 0))],
        out_shape=[jax.ShapeDtypeStruct((nb, D_MODEL), F32),
                   jax.ShapeDtypeStruct(s0.shape, F32),
                   jax.ShapeDtypeStruct(c0.shape, F32)],
        scratch_shapes=[pltpu.VMEM((nb, HK), BF16), pltpu.VMEM((nb, HK), BF16), pltpu.VMEM((3 * nb, HK), BF16),
                        pltpu.VMEM((nb, HV), F32), pltpu.VMEM((nb, HV), F32), pltpu.VMEM((nb, CONV_WIDTH), F32),
                        pltpu.VMEM((nb, HV), F32)],
        compiler_params=pltpu.CompilerParams(
            dimension_semantics=("arbitrary",), vmem_limit_bytes=_vmem_limit(est)),
        name="mixer_sample",
    )(h, s0, c0, *consts)


TM_PROMPT = 512
FFN_SIZES = (768, 256)
FFN_PLE_SIZES = (256, 256, 256, 256)
BS_SAMPLE = 32


def kernel(x_prompt, x_sample, state_gla, state_conv, p_prompt, p_sample, norm_g, w_in, w_gate_up, b_gate,
           gla_norm_g, conv_w, w_out, ffn1_gate, ffn1_up, ffn1_down, ffn2_gate, ffn2_up, ffn2_down,
           w_ple_proj, w_ple_gate):
    batch, seq, _ = x_prompt.shape
    nb = x_sample.shape[0]
    assert norm_g.shape[0] == 1 and x_sample.shape[1] == 1

    ng = norm_g[0]
    f1 = (_gain_into_rows(ffn1_gate[0], ng[0]), _gain_into_rows(ffn1_up[0], ng[0]), ffn1_down[0].astype(BF16))
    wgu = jnp.pad(w_gate_up[0], ((0, LANES - GLA_RANK), (0, 0))).astype(BF16)
    bg = b_gate[0].reshape(1, HK)
    gng = gla_norm_g[0].reshape(1, GLA_DV)
    cw = conv_w[0]
    r0 = 2 * HK + 2 * HV
    r1 = r0 + GLA_RANK
    jobs = [
        _PrepJob(ffn2_gate[0], gain=4), _PrepJob(ffn2_up[0], gain=4), _PrepJob(ffn2_down[0]),
        _PrepJob(jnp.transpose(w_in[0]), gain=2, transposed=True,
                 pieces=[(0, r0, r0), (r0, GLA_RANK, LANES), (r1, w_in.shape[2] - r1, w_in.shape[2] - r1)]),
        _PrepJob(w_out[0]), _PrepJob(w_ple_gate[0], gain=6), _PrepJob(w_ple_proj[0]),
    ]

    xp = x_prompt.reshape(batch * seq, D_MODEL)
    pp = p_prompt[0].reshape(batch * seq, PLE_DIM)
    hp, f2g, f2u, f2d, wa, wr, wc, wout, wpg, wpp = _ffn(xp, ng, *f1, 1, FFN_SIZES, prep=(ng.T, jobs))
    f2 = (f2g, f2u, f2d)
    mix_w = (ng, wa, wr, wgu, bg, wc, gng, cw, wout)
    hp, sg_p, sc_p = _mixer_prompt(hp, batch, seq, TM_PROMPT, *mix_w)
    hp = _ffn(hp, ng, *f2, 5, FFN_PLE_SIZES, ple_args=(pp, wpp, wpg))

    xs = x_sample.reshape(nb, D_MODEL)
    ps = p_sample[0].reshape(nb, PLE_DIM)
    s0 = state_gla[0].reshape(nb, HK, GLA_DV)
    c0 = state_conv[0].reshape(nb, (CONV_K - 1) * CONV_WIDTH)
    hs = _ffn(xs, ng, *f1, 1, (nb,))
    hs, sg_s, sc_s = _mixer_sample(hs, s0, c0, BS_SAMPLE, *mix_w)
    hs = _ffn(hs, ng, *f2, 5, (nb,), ple_args=(ps, wpp, wpg))

    return (hp.reshape(batch, seq, D_MODEL),
            hs.reshape(nb, 1, D_MODEL),
            sg_p.reshape(1, batch, GLA_HEADS, GLA_DK, GLA_DV),
            sc_p.reshape(1, batch, CONV_K - 1, CONV_WIDTH),
            sg_s.reshape(1, nb, GLA_HEADS, GLA_DK, GLA_DV),
            sc_s.reshape(1, nb, CONV_K - 1, CONV_WIDTH))
```
